```python
import math
import jax, jax.numpy as jnp
from jax import lax
import numpy as np

D_MODEL = 1024
BATCH = 8
SEQ = 2048
DEPTH = 2
DEC_BATCH = 32
DEC_SEQ = 8
PAST_LEN = 8192
PAGE_SIZE = 128

HEAD_DIM = 64
N_MIX_HEADS = D_MODEL // HEAD_DIM
A_HEADS = 6
B_HEADS = 5
C_HEADS = N_MIX_HEADS - A_HEADS - B_HEADS
C_DK = 64
C_DV = HEAD_DIM
D_MIX = (A_HEADS + B_HEADS) * HEAD_DIM + C_HEADS * C_DV
D_FF = 2816
MOBA_BLOCK = 256
MOBA_TOPK = 3
MOBA_Q_BLOCK = 16
FOX_Q_BLOCK = 128
HGRN_CHUNK = 64
ROPE_THETA = 10000.0
FOX_BIAS_INIT = 2.0
EPS = 1e-6
NEG_BIG = -1e30
IN_SPLIT_SIZES = (A_HEADS * HEAD_DIM,) * 3 + (B_HEADS * HEAD_DIM,) * 3 + (B_HEADS,) + (C_HEADS * C_DK, C_HEADS * C_DK, C_HEADS * C_DV, C_HEADS * C_DV)
D_IN = sum(IN_SPLIT_SIZES)

kernel_name = 'hymba_moba_fox_hgrn2_macaron_step'


def _rmsnorm(x, g):
    x32 = x.astype(jnp.float32)
    y = x32 * lax.rsqrt(jnp.mean(x32 * x32, axis=-1, keepdims=True) + EPS)
    return (y * g.astype(jnp.float32)).astype(x.dtype)


def _swiglu(h, wg, wu, wd):
    return (jax.nn.silu(h @ wg) * (h @ wu)) @ wd


def _rope(x, pos0):
    T, hd = x.shape[1], x.shape[-1]
    half = hd // 2
    inv = ROPE_THETA ** (-jnp.arange(half, dtype=jnp.float32) * 2.0 / hd)
    ang = (pos0 + jnp.arange(T, dtype=jnp.float32))[:, None] * inv[None, :]
    cos = jnp.cos(ang)[None, :, None, :]
    sin = jnp.sin(ang)[None, :, None, :]
    x32 = x.astype(jnp.float32)
    x1, x2 = x32[..., :half], x32[..., half:]
    return jnp.concatenate([x1 * cos - x2 * sin, x1 * sin + x2 * cos], axis=-1).astype(x.dtype)


def _to_blocks(x, qb):
    B, T, H, d = x.shape
    return x.reshape(B, T // qb, qb, H, d).transpose(1, 0, 3, 2, 4)


def _from_blocks(o):
    n, B, H, qb, d = o.shape
    return o.transpose(1, 0, 3, 2, 4).reshape(B, n * qb, H, d)


def _moba_attend(q, k_all, v_all, pos0):
    B, T, H, hd = q.shape
    L = k_all.shape[1]
    nb = -(-L // MOBA_BLOCK)
    pad = nb * MOBA_BLOCK - L

    def blocks(a):
        a = jnp.pad(a, ((0, 0), (0, pad), (0, 0), (0, 0)))
        return a.reshape(B, nb, MOBA_BLOCK, H, hd).transpose(0, 3, 1, 2, 4)

    kb, vb = blocks(k_all), blocks(v_all)
    kmean = jnp.mean(kb.astype(jnp.float32), axis=3)
    topk = min(MOBA_TOPK, nb)
    qb = math.gcd(T, MOBA_Q_BLOCK)
    b_ix = jnp.arange(B)[:, None, None, None]
    h_ix = jnp.arange(H)[None, :, None, None]
    blk_ix = jnp.arange(nb)
    in_blk = jnp.arange(MOBA_BLOCK)
    scale = hd ** -0.5

    def step(args):
        qc, i = args
        t = pos0 + i * qb + jnp.arange(qb)
        own = t // MOBA_BLOCK
        gate = jnp.einsum('bhqd,bhnd->bhqn', qc.astype(jnp.float32), kmean)
        gate = jnp.where(blk_ix[None, :] < own[:, None], gate, NEG_BIG)
        _, sel = lax.top_k(gate, topk)
        idx = jnp.concatenate([sel.astype(jnp.int32), jnp.broadcast_to(own[None, None, :, None], (B, H, qb, 1)).astype(jnp.int32)], axis=-1)
        valid = jnp.concatenate([sel < own[:, None], jnp.ones((B, H, qb, 1), dtype=bool)], axis=-1)
        ks = kb[b_ix, h_ix, idx]
        vs = vb[b_ix, h_ix, idx]
        kpos = idx[..., None] * MOBA_BLOCK + in_blk
        mask = valid[..., None] & (kpos <= t[None, None, :, None, None])
        s = jnp.einsum('bhqd,bhqnkd->bhqnk', qc, ks).astype(jnp.float32) * scale
        s = jnp.where(mask, s, NEG_BIG)
        p = jax.nn.softmax(s.reshape(B, H, qb, -1), axis=-1).reshape(s.shape)
        return jnp.einsum('bhqnk,bhqnkd->bhqd', p.astype(vs.dtype), vs)

    o = lax.map(step, (_to_blocks(q, qb), jnp.arange(T // qb)))
    return _from_blocks(o)


def _fox_attend(q, k_all, v_all, logf_all, pos0):
    B, T, H, hd = q.shape
    L = k_all.shape[1]
    c = jnp.cumsum(logf_all.astype(jnp.float32), axis=1).transpose(0, 2, 1)
    kh = k_all.transpose(0, 2, 1, 3)
    vh = v_all.transpose(0, 2, 1, 3)
    qb = math.gcd(T, FOX_Q_BLOCK)
    kpos = jnp.arange(L)
    scale = hd ** -0.5

    def step(args):
        qc, i = args
        start = pos0 + i * qb
        t = start + jnp.arange(qb)
        ct = lax.dynamic_slice_in_dim(c, start, qb, axis=2)
        s = jnp.einsum('bhqd,bhkd->bhqk', qc, kh).astype(jnp.float32) * scale + ct[..., None] - c[:, :, None, :]
        s = jnp.where(kpos[None, :] <= t[:, None], s, NEG_BIG)
        p = jax.nn.softmax(s, axis=-1)
        return jnp.einsum('bhqk,bhkd->bhqd', p.astype(vh.dtype), vh)

    o = lax.map(step, (_to_blocks(q, qb), jnp.arange(T // qb)))
    return _from_blocks(o)


def _hgrn2_recurrence(q, k, v, logf, s0):
    T = q.shape[1]
    L = math.gcd(T, HGRN_CHUNK)
    causal = jnp.tril(jnp.ones((L, L), dtype=bool))

    def chunks(a):
        return _to_blocks(a.astype(jnp.float32), L)

    def step(S, inp):
        qc, kc, vc, gc = inp
        cum = jnp.cumsum(gc, axis=2)
        o_inter = jnp.einsum('bhld,bhde->bhle', qc * jnp.exp(cum), S)
        diff = cum[:, :, :, None, :] - cum[:, :, None, :, :]
        decay = jnp.exp(jnp.where(causal[None, None, :, :, None], diff, NEG_BIG))
        A = jnp.einsum('bhtd,bhtsd,bhsd->bhts', qc, decay, kc)
        o = o_inter + jnp.einsum('bhts,bhse->bhte', A, vc)
        last = cum[:, :, -1:, :]
        S_new = jnp.exp(last[:, :, 0, :])[..., None] * S + jnp.einsum('bhsd,bhse->bhde', kc * jnp.exp(last - cum), vc)
        return S_new, o

    S, o = lax.scan(step, s0.astype(jnp.float32), (chunks(q), chunks(k), chunks(v), chunks(logf)))
    return _from_blocks(o), S


def _token_mix(h, w_in, fox_b, lb, hgrn_g, w_out, past):
    B, T, _ = h.shape
    pos0 = 0 if past is None else past[0].shape[1]
    split_at = np.cumsum(IN_SPLIT_SIZES)[:-1].tolist()
    qa, ka, va, qf, kf, vf, ff, qc, fc, ic, gc = jnp.split(h @ w_in, split_at, axis=-1)

    def heads(a, n):
        return a.reshape(B, T, n, -1)

    qa = _rope(heads(qa, A_HEADS), pos0)
    ka = _rope(heads(ka, A_HEADS), pos0)
    va = heads(va, A_HEADS)
    qf, kf, vf = heads(qf, B_HEADS), heads(kf, B_HEADS), heads(vf, B_HEADS)
    logf_f = jax.nn.log_sigmoid(ff.astype(jnp.float32) + fox_b.astype(jnp.float32))
    lb32 = lb.astype(jnp.float32).reshape(C_HEADS, C_DK)
    fc32 = heads(fc, C_HEADS).astype(jnp.float32)
    f_c = lb32 + (1.0 - lb32) * jax.nn.sigmoid(fc32)
    logf_c = jnp.log(f_c)
    kc = (1.0 - lb32) * jax.nn.sigmoid(-fc32)
    qc = jax.nn.silu(heads(qc, C_HEADS).astype(jnp.float32))
    ic = heads(ic, C_HEADS)

    if past is None:
        ka_all, va_all, kf_all, vf_all, lf_all = ka, va, kf, vf, logf_f
        s0 = jnp.zeros((B, C_HEADS, C_DK, C_DV), dtype=jnp.float32)
    else:
        pk_a, pv_a, pk_f, pv_f, plf, s0 = past
        ka_all = jnp.concatenate([pk_a.astype(ka.dtype), ka], axis=1)
        va_all = jnp.concatenate([pv_a.astype(va.dtype), va], axis=1)
        kf_all = jnp.concatenate([pk_f.astype(kf.dtype), kf], axis=1)
        vf_all = jnp.concatenate([pv_f.astype(vf.dtype), vf], axis=1)
        lf_all = jnp.concatenate([plf.astype(jnp.float32), logf_f], axis=1)

    oa = _moba_attend(qa, ka_all, va_all, pos0)
    of = _fox_attend(qf, kf_all, vf_all, lf_all, pos0)
    oc, s_new = _hgrn2_recurrence(qc, kc, ic, logf_c, s0)
    oc = _rmsnorm(oc, hgrn_g) * jax.nn.silu(heads(gc, C_HEADS).astype(jnp.float32))
    o = jnp.concatenate([oa.reshape(B, T, -1), of.reshape(B, T, -1), oc.reshape(B, T, -1).astype(h.dtype)], axis=-1) @ w_out
    new = (ka, va, kf, vf, logf_f.astype(h.dtype), s_new.astype(h.dtype))
    return o, new


def setup_inputs(seed: int = 0) -> dict:
    key = jax.random.key(seed)
    ks = jax.random.split(key, 24)
    nrm = jax.random.normal
    n_pages = PAST_LEN // PAGE_SIZE
    n_pool = (DEC_BATCH * n_pages * 5) // 4
    page_table = jax.random.permutation(ks[8], n_pool)[:DEC_BATCH * n_pages].reshape(DEC_BATCH, n_pages).astype(jnp.int32)
    return {
        'x_prompt': nrm(ks[0], (BATCH, SEQ, D_MODEL), jnp.float32),
        'x_sample': nrm(ks[1], (DEC_BATCH, DEC_SEQ, D_MODEL), jnp.float32),
        'cache_moba_k': nrm(ks[2], (DEPTH, n_pool, PAGE_SIZE, A_HEADS, HEAD_DIM), jnp.float32),
        'cache_moba_v': nrm(ks[3], (DEPTH, n_pool, PAGE_SIZE, A_HEADS, HEAD_DIM), jnp.float32),
        'cache_fox_k': nrm(ks[4], (DEPTH, n_pool, PAGE_SIZE, B_HEADS, HEAD_DIM), jnp.float32),
        'cache_fox_v': nrm(ks[5], (DEPTH, n_pool, PAGE_SIZE, B_HEADS, HEAD_DIM), jnp.float32),
        'cache_fox_logf': jax.nn.log_sigmoid(FOX_BIAS_INIT + nrm(ks[6], (DEPTH, n_pool, PAGE_SIZE, B_HEADS), jnp.float32)),
        'state_hgrn': 0.5 * nrm(ks[7], (DEPTH, DEC_BATCH, C_HEADS, C_DK, C_DV), jnp.float32),
        'page_table': page_table,
        'norm_ffn1_g': 1.0 + 0.02 * nrm(ks[9], (DEPTH, D_MODEL), jnp.float32),
        'ffn1_w_gate': nrm(ks[10], (DEPTH, D_MODEL, D_FF), jnp.float32) * D_MODEL ** -0.5,
        'ffn1_w_up': nrm(ks[11], (DEPTH, D_MODEL, D_FF), jnp.float32) * D_MODEL ** -0.5,
        'ffn1_w_down': nrm(ks[12], (DEPTH, D_FF, D_MODEL), jnp.float32) * D_FF ** -0.5,
        'norm_mix_g': 1.0 + 0.02 * nrm(ks[13], (DEPTH, D_MODEL), jnp.float32),
        'w_in': nrm(ks[14], (DEPTH, D_MODEL, D_IN), jnp.float32) * D_MODEL ** -0.5,
        'fox_f_bias': FOX_BIAS_INIT + 0.1 * nrm(ks[15], (DEPTH, B_HEADS), jnp.float32),
        'hgrn_lb_logits': 0.1 * nrm(ks[16], (DEPTH, C_HEADS * C_DK), jnp.float32),
        'hgrn_norm_g': 1.0 + 0.02 * nrm(ks[17], (DEPTH, C_DV), jnp.float32),
        'w_out': nrm(ks[18], (DEPTH, D_MIX, D_MODEL), jnp.float32) * D_MIX ** -0.5,
        'norm_ffn2_g': 1.0 + 0.02 * nrm(ks[19], (DEPTH, D_MODEL), jnp.float32),
        'ffn2_w_gate': nrm(ks[20], (DEPTH, D_MODEL, D_FF), jnp.float32) * D_MODEL ** -0.5,
        'ffn2_w_up': nrm(ks[21], (DEPTH, D_MODEL, D_FF), jnp.float32) * D_MODEL ** -0.5,
        'ffn2_w_down': nrm(ks[22], (DEPTH, D_FF, D_MODEL), jnp.float32) * D_FF ** -0.5,
        'norm_final_g': 1.0 + 0.02 * nrm(ks[23], (D_MODEL,), jnp.float32),
    }


def reference(x_prompt, x_sample, cache_moba_k, cache_moba_v, cache_fox_k, cache_fox_v, cache_fox_logf, state_hgrn, page_table, norm_ffn1_g, ffn1_w_gate, ffn1_w_up, ffn1_w_down, norm_mix_g, w_in, fox_f_bias, hgrn_lb_logits, hgrn_norm_g, w_out, norm_ffn2_g, ffn2_w_gate, ffn2_w_up, ffn2_w_down, norm_final_g):
    lb_p = jax.nn.softmax(hgrn_lb_logits.astype(jnp.float32), axis=0)
    lower_bounds = jnp.cumsum(lb_p, axis=0) - lb_p[0:1]
    n_seq, n_pages = page_table.shape

    def paged(cache_l):
        rows = cache_l[page_table]
        return rows.reshape((n_seq, n_pages * cache_l.shape[1]) + cache_l.shape[2:])

    def trunk(x, sample):
        new = []
        for l in range(DEPTH):
            x = x + 0.5 * _swiglu(_rmsnorm(x, norm_ffn1_g[l]), ffn1_w_gate[l], ffn1_w_up[l], ffn1_w_down[l])
            past = None
            if sample:
                past = (paged(cache_moba_k[l]), paged(cache_moba_v[l]), paged(cache_fox_k[l]), paged(cache_fox_v[l]), paged(cache_fox_logf[l]), state_hgrn[l])
            o, rows = _token_mix(_rmsnorm(x, norm_mix_g[l]), w_in[l], fox_f_bias[l], lower_bounds[l], hgrn_norm_g[l], w_out[l], past)
            x = x + o
            x = x + 0.5 * _swiglu(_rmsnorm(x, norm_ffn2_g[l]), ffn2_w_gate[l], ffn2_w_up[l], ffn2_w_down[l])
            new.append(rows)
        y = _rmsnorm(x, norm_final_g)
        return y, [jnp.stack(z) for z in zip(*new)]

    y_prompt, (mk_p, mv_p, fk_p, fv_p, fl_p, hs_p) = trunk(x_prompt, False)
    y_sample, (mk_s, mv_s, fk_s, fv_s, fl_s, hs_s) = trunk(x_sample, True)
    return (y_prompt, y_sample, mk_p, mv_p, fk_p, fv_p, fl_p, hs_p, mk_s, mv_s, fk_s, fv_s, fl_s, hs_s)
```

```python
import functools
import math

import numpy as np
import jax
import jax.numpy as jnp
from jax import lax
from jax.experimental import pallas as pl
from jax.experimental.pallas import tpu as pltpu

F32 = jnp.float32
BF16 = jnp.bfloat16
HIGHEST = lax.Precision.HIGHEST

HEAD_DIM = 64
A_HEADS = 6
B_HEADS = 5
C_HEADS = 5
A_W = A_HEADS * HEAD_DIM
B_W = B_HEADS * HEAD_DIM
C_W = C_HEADS * HEAD_DIM
D_FF_CHUNK = 256
MOBA_BLOCK = 256
MOBA_TOPK = 3
ROPE_THETA = 10000.0
EPS = 1e-6
NEG_BIG = -1e30

LANES = 128
SUBLANES = 8
VMEM_PHYSICAL = 64 * 1024 * 1024
VMEM_CAP = VMEM_PHYSICAL - 8 * 1024 * 1024

SEG_W = 384
SEG_QA, SEG_KA, SEG_VA, SEG_QF, SEG_KF, SEG_VF, SEG_QC, SEG_FC, SEG_IC, SEG_GC = (
    i * SEG_W for i in range(10))
SEG_FF = 10 * SEG_W
D_IN_PAD = SEG_FF + LANES


def _vmem_limit(estimate_bytes):
    return int(min(max(2 * estimate_bytes, 32 * 1024 * 1024), VMEM_CAP))


def _dot(a, b, precision=None):
    return jnp.dot(a, b, preferred_element_type=F32, precision=precision)


def _dot_nt(a, b, precision=None):
    return lax.dot_general(a, b, (((1,), (1,)), ((), ())),
                           preferred_element_type=F32, precision=precision)


def _dot_tn(a, b, precision=None):
    return lax.dot_general(a, b, (((0,), (0,)), ((), ())),
                           preferred_element_type=F32, precision=precision)


def _rms(x, g):
    ms = jnp.mean(x * x, axis=-1, keepdims=True)
    return x * lax.rsqrt(ms + EPS) * g


def _sigmoid(x):
    return 1.0 / (1.0 + jnp.exp(-x))


def _silu(x):
    return x * _sigmoid(x)


def _log_sigmoid(x):
    return jnp.minimum(x, 0.0) - jnp.log1p(jnp.exp(-jnp.abs(x)))


def _head_cols(ref, rows, h, width):
    c0 = (h // 2) * LANES
    if c0 + LANES <= width:
        win = ref[rows, c0:c0 + LANES]
        off = (h % 2) * HEAD_DIM
        return win[:, off:off + HEAD_DIM]
    return ref[rows, c0:c0 + HEAD_DIM]


def _ffn_body(*refs, has_mix, has_final, d_ff):
    it = iter(refs)
    x_ref = next(it)
    if has_mix:
        oa_ref, of_ref, oc_ref, woa_ref, wof_ref, woc_ref = (next(it) for _ in range(6))
    g_ref, wg_ref, wu_ref, wd_ref = (next(it) for _ in range(4))
    gf_ref = next(it) if has_final else None
    out_ref = next(it)

    x = x_ref[...]
    if has_mix:
        x = (x + _dot(oa_ref[...].astype(BF16), woa_ref[...])
             + _dot(of_ref[...].astype(BF16), wof_ref[...])
             + _dot(oc_ref[...].astype(BF16), woc_ref[...]))
    h = _rms(x, g_ref[...]).astype(BF16)
    acc = jnp.zeros(x.shape, F32)
    for c0 in range(0, d_ff, D_FF_CHUNK):
        gate = _dot(h, wg_ref[:, c0:c0 + D_FF_CHUNK])
        up = _dot(h, wu_ref[:, c0:c0 + D_FF_CHUNK])
        act = (_silu(gate) * up).astype(BF16)
        acc = acc + _dot(act, wd_ref[c0:c0 + D_FF_CHUNK, :])
    y = x + 0.5 * acc
    if has_final:
        y = _rms(y, gf_ref[...])
    out_ref[...] = y


def _ffn(x, g, wg, wu, wd, mix=None, g_final=None, tm=512):
    n, d = x.shape
    d_ff = wg.shape[1]
    tm = min(tm, n)
    assert n % tm == 0 and d_ff % D_FF_CHUNK == 0
    row = lambda w: pl.BlockSpec((tm, w), lambda i: (i, 0))
    full = lambda a: pl.BlockSpec(a.shape, lambda i: (0,) * a.ndim, pipeline_mode=pl.Buffered(1))
    args, specs = [x], [row(d)]
    if mix is not None:
        oa, of, oc, woa, wof, woc = mix
        args += [oa, of, oc, woa, wof, woc]
        specs += [row(oa.shape[1]), row(of.shape[1]), row(oc.shape[1]), full(woa), full(wof), full(woc)]
    args += [g, wg, wu, wd]
    specs += [full(g), full(wg), full(wu), full(wd)]
    if g_final is not None:
        args.append(g_final)
        specs.append(full(g_final))
    weights = 2 * 3 * d * d_ff + (2 * d * d if mix is not None else 0)
    tiles = 2 * 2 * tm * d * 4 * (2 if mix is not None else 1) + 6 * tm * d * 4
    return pl.pallas_call(
        functools.partial(_ffn_body, has_mix=mix is not None, has_final=g_final is not None, d_ff=d_ff),
        out_shape=jax.ShapeDtypeStruct((n, d), F32),
        grid=(n // tm,),
        in_specs=specs,
        out_specs=row(d),
        compiler_params=pltpu.CompilerParams(
            dimension_semantics=("arbitrary",), vmem_limit_bytes=_vmem_limit(weights + tiles)),
        name="ffn_mix" if mix is not None else "ffn",
    )(*args)


def _inproj_body(x_ref, g_ref, w_ref, cos_ref, slo_ref, shi_ref, fb_ref, lbl_ref,
                 qa_ref, ka_ref, va_ref, qf_ref, kf_ref, vf_ref, lf_ref,
                 qc_ref, kc_ref, lc_ref, ic_ref, gc_ref, *, layer):
    h = _rms(x_ref[...], g_ref[...]).astype(BF16)

    def proj(c0, width=SEG_W):
        return _dot(h, w_ref[:, c0:c0 + width])

    cos, slo, shi = cos_ref[...], slo_ref[...], shi_ref[...]

    def rope(p):
        parts = []
        for c in range(SEG_W // LANES):
            pc = p[:, c * LANES:(c + 1) * LANES]
            parts.append(pc * cos + pltpu.roll(pc, LANES - HEAD_DIM // 2, 1) * slo
                         + pltpu.roll(pc, HEAD_DIM // 2, 1) * shi)
        return jnp.concatenate(parts, axis=1)

    qa_ref[...] = rope(proj(SEG_QA))
    ka_ref[...] = rope(proj(SEG_KA))
    va_ref[...] = proj(SEG_VA)
    qf_ref[...] = proj(SEG_QF)[:, :B_W]
    kf_ref[...] = proj(SEG_KF)[:, :B_W]
    vf_ref[...] = proj(SEG_VF)[:, :B_W]
    lf_ref[...] = _log_sigmoid(proj(SEG_FF, LANES) + fb_ref[...])

    lbl = lbl_ref[...]
    e = jnp.exp(lbl - jnp.max(lbl, axis=0, keepdims=True))
    psm = e / jnp.sum(e, axis=0, keepdims=True)
    lb = jnp.zeros((1, SEG_W), F32)
    for j in range(1, layer + 1):
        lb = lb + psm[j:j + 1, :]
    fc = proj(SEG_FC)
    lc_ref[...] = jnp.log(lb + (1.0 - lb) * _sigmoid(fc))[:, :C_W]
    kc_ref[...] = ((1.0 - lb) * _sigmoid(-fc))[:, :C_W]
    qc_ref[...] = _silu(proj(SEG_QC))[:, :C_W]
    ic_ref[...] = proj(SEG_IC)[:, :C_W]
    gc_ref[...] = _silu(proj(SEG_GC))[:, :C_W]


def _inproj(x, g, w_pad, rope_tabs, fb, lbl, layer, tm=256):
    n, d = x.shape
    tm = min(tm, n)
    assert n % tm == 0 and rope_tabs[0].shape[0] % tm == 0
    row = lambda w: pl.BlockSpec((tm, w), lambda i: (i, 0))
    full = lambda a: pl.BlockSpec(a.shape, lambda i: (0,) * a.ndim, pipeline_mode=pl.Buffered(1))
    n_tab = rope_tabs[0].shape[0] // tm
    tab = pl.BlockSpec((tm, LANES), lambda i: (i % n_tab, 0))
    widths = [A_W, A_W, A_W, B_W, B_W, B_W, LANES, C_W, C_W, C_W, C_W, C_W]
    est = 2 * d * D_IN_PAD + 2 * tm * 4 * (d + sum(widths) + 3 * LANES) + 8 * tm * SEG_W * 4
    return pl.pallas_call(
        functools.partial(_inproj_body, layer=layer),
        out_shape=[jax.ShapeDtypeStruct((n, w), F32) for w in widths],
        grid=(n // tm,),
        in_specs=[row(d), full(g), full(w_pad), tab, tab, tab, full(fb), full(lbl)],
        out_specs=[row(w) for w in widths],
        compiler_params=pltpu.CompilerParams(
            dimension_semantics=("arbitrary",), vmem_limit_bytes=_vmem_limit(est)),
        name="inproj",
    )(x, g, w_pad, *rope_tabs, fb, lbl)


def _topk_select(gate, allowed, lane_idx, n_lanes, topk):
    g = jnp.where(allowed, gate, NEG_BIG)
    sel = jnp.zeros(gate.shape, F32)
    lane_f = lane_idx.astype(F32)
    for _ in range(topk):
        m = jnp.max(g, axis=1, keepdims=True)
        idx = jnp.min(jnp.where(g == m, lane_f, float(n_lanes)), axis=1, keepdims=True)
        hit = lane_f == idx
        sel = jnp.where(hit & allowed, 1.0, sel)
        g = jnp.where(hit, -jnp.inf, g)
    return sel


def _softmax_first(s):
    m = jnp.max(s, axis=1, keepdims=True)
    p = jnp.exp(s - m)
    return m, jnp.sum(p, axis=1, keepdims=True), p


def _softmax_next(s, m, l, acc, v):
    m_new = jnp.maximum(m, jnp.max(s, axis=1, keepdims=True))
    alpha = jnp.exp(m - m_new)
    p = jnp.exp(s - m_new)
    l_new = alpha * l + jnp.sum(p, axis=1, keepdims=True)
    acc_new = alpha * acc + _dot(p.astype(BF16), v)
    return m_new, l_new, acc_new


def _moba_prompt_body(q_ref, k_ref, v_ref, o_ref, kmean_ref, *, blk):
    i = pl.program_id(1)
    nb = k_ref.shape[0] // blk
    scale = HEAD_DIM ** -0.5

    @pl.when(i == 0)
    def _():
        for n in range(nb):
            kmean_ref[n:n + 1, :] = jnp.sum(k_ref[n * blk:(n + 1) * blk, :], axis=0, keepdims=True) * (1.0 / blk)

    row = lax.broadcasted_iota(jnp.int32, (blk, blk), 0)
    col = lax.broadcasted_iota(jnp.int32, (blk, blk), 1)
    causal = col <= row
    lane_blk = lax.broadcasted_iota(jnp.int32, (blk, nb), 1)
    past = lane_blk < i
    own_rows = pl.ds(pl.multiple_of(i * blk, blk), blk)
    all_rows = slice(None)
    outs = []
    for h in range(A_HEADS):
        qh = _head_cols(q_ref, all_rows, h, A_W)
        gate = _dot_nt(qh, _head_cols(kmean_ref, all_rows, h, A_W), precision=HIGHEST)
        sel = _topk_select(gate, past, lane_blk, nb, MOBA_TOPK)
        qb = (qh * scale).astype(BF16)
        kd = _head_cols(k_ref, own_rows, h, A_W).astype(BF16)
        vd = _head_cols(v_ref, own_rows, h, A_W).astype(BF16)
        m0, l0, p0 = _softmax_first(jnp.where(causal, _dot_nt(qb, kd), NEG_BIG))
        acc0 = _dot(p0.astype(BF16), vd)

        def body(n, carry, h=h, qb=qb, sel=sel):
            rows = pl.ds(pl.multiple_of(n * blk, blk), blk)
            kb = _head_cols(k_ref, rows, h, A_W).astype(BF16)
            vb = _head_cols(v_ref, rows, h, A_W).astype(BF16)
            sel_n = jnp.sum(jnp.where(lane_blk == n, sel, 0.0), axis=1, keepdims=True)
            s = jnp.where(sel_n > 0.0, _dot_nt(qb, kb), NEG_BIG)
            return _softmax_next(s, *carry, vb)

        _, l, acc = lax.fori_loop(0, i, body, (m0, l0, acc0))
        outs.append(acc / l)
    o_ref[...] = jnp.concatenate(outs, axis=1)


def _moba_prompt(q, k, v, batch, seq):
    blk = MOBA_BLOCK
    assert seq % blk == 0
    nq = seq // blk
    est = 2 * 2 * seq * A_W * 4 + 4 * blk * A_W * 4 + 16 * blk * blk * 4
    return pl.pallas_call(
        functools.partial(_moba_prompt_body, blk=blk),
        out_shape=jax.ShapeDtypeStruct(q.shape, F32),
        grid=(batch, nq),
        in_specs=[pl.BlockSpec((blk, A_W), lambda b, i: (b * nq + i, 0)),
                  pl.BlockSpec((seq, A_W), lambda b, i: (b, 0)),
                  pl.BlockSpec((seq, A_W), lambda b, i: (b, 0))],
        out_specs=pl.BlockSpec((blk, A_W), lambda b, i: (b * nq + i, 0)),
        scratch_shapes=[pltpu.VMEM((seq // blk, A_W), F32)],
        compiler_params=pltpu.CompilerParams(
            dimension_semantics=("arbitrary", "arbitrary"), vmem_limit_bytes=_vmem_limit(est)),
        name="moba_prompt",
    )(q, k, v)


def _fox_prompt_body(q_ref, k_ref, v_ref, lf_ref, o_ref, ccol_ref, crow_ref, *, blk):
    i = pl.program_id(1)
    nb = k_ref.shape[0] // blk
    scale = HEAD_DIM ** -0.5
    row = lax.broadcasted_iota(jnp.int32, (blk, blk), 0)
    col = lax.broadcasted_iota(jnp.int32, (blk, blk), 1)
    causal = col <= row

    @pl.when(i == 0)
    def _():
        tri = causal.astype(F32)
        carry = jnp.zeros((1, LANES), F32)
        for n in range(nb):
            c = _dot(tri, lf_ref[n * blk:(n + 1) * blk, :], precision=HIGHEST) + carry
            ccol_ref[n * blk:(n + 1) * blk, :] = c
            crow_ref[n] = c.T
            carry = c[blk - 1:blk, :]

    own_rows = pl.ds(pl.multiple_of(i * blk, blk), blk)
    all_rows = slice(None)
    cq_all = ccol_ref[own_rows, :]
    outs = []
    for h in range(B_HEADS):
        qb = (_head_cols(q_ref, all_rows, h, B_W) * scale).astype(BF16)
        cq = cq_all[:, h:h + 1]
        kd = _head_cols(k_ref, own_rows, h, B_W).astype(BF16)
        vd = _head_cols(v_ref, own_rows, h, B_W).astype(BF16)
        s = _dot_nt(qb, kd) + (cq - crow_ref[i, h:h + 1, :])
        m0, l0, p0 = _softmax_first(jnp.where(causal, s, NEG_BIG))
        acc0 = _dot(p0.astype(BF16), vd)

        def body(n, carry, h=h, qb=qb, cq=cq):
            rows = pl.ds(pl.multiple_of(n * blk, blk), blk)
            kb = _head_cols(k_ref, rows, h, B_W).astype(BF16)
            vb = _head_cols(v_ref, rows, h, B_W).astype(BF16)
            s = _dot_nt(qb, kb) + (cq - crow_ref[n, h:h + 1, :])
            return _softmax_next(s, *carry, vb)

        _, l, acc = lax.fori_loop(0, i, body, (m0, l0, acc0))
        outs.append(acc / l)
    o_ref[...] = jnp.concatenate(outs, axis=1)


def _fox_prompt(q, k, v, lf, batch, seq, blk=256):
    assert seq % blk == 0
    nq = seq // blk
    est = 2 * 2 * seq * B_W * 4 + 3 * seq * LANES * 4 + seq * LANES * 4 + 16 * blk * blk * 4
    return pl.pallas_call(
        functools.partial(_fox_prompt_body, blk=blk),
        out_shape=jax.ShapeDtypeStruct(q.shape, F32),
        grid=(batch, nq),
        in_specs=[pl.BlockSpec((blk, B_W), lambda b, i: (b * nq + i, 0)),
                  pl.BlockSpec((seq, B_W), lambda b, i: (b, 0)),
                  pl.BlockSpec((seq, B_W), lambda b, i: (b, 0)),
                  pl.BlockSpec((seq, LANES), lambda b, i: (b, 0))],
        out_specs=pl.BlockSpec((blk, B_W), lambda b, i: (b * nq + i, 0)),
        scratch_shapes=[pltpu.VMEM((seq, LANES), F32), pltpu.VMEM((nq, LANES, blk), F32)],
        compiler_params=pltpu.CompilerParams(
            dimension_semantics=("arbitrary", "arbitrary"), vmem_limit_bytes=_vmem_limit(est)),
        name="fox_prompt",
    )(q, k, v, lf)


def _hgrn_body(q_ref, k_ref, lf_ref, v_ref, gs_ref, gn_ref, st0_ref, o_ref, st_ref, m_scr, *, sub):
    seq = q_ref.shape[0]
    r_i = lax.broadcasted_iota(jnp.int32, (sub, sub), 0)
    c_i = lax.broadcasted_iota(jnp.int32, (sub, sub), 1)
    tri = (c_i <= r_i).astype(F32)
    e_i = lax.broadcasted_iota(jnp.int32, (C_W, C_W), 0) // HEAD_DIM
    d_i = lax.broadcasted_iota(jnp.int32, (C_W, C_W), 1) // HEAD_DIM
    same_head = e_i == d_i
    ones_bd = same_head.astype(BF16)
    row_id = lax.broadcasted_iota(jnp.int32, (sub, C_W), 0)

    st_ref[...] = st0_ref[...]

    def chunk(c, _):
        rows = pl.ds(pl.multiple_of(c * sub, sub), sub)
        lf, q, k, v = lf_ref[rows, :], q_ref[rows, :], k_ref[rows, :], v_ref[rows, :]
        cum = _dot(tri, lf, precision=HIGHEST)
        last = cum[sub - 1:sub, :]
        st = st_ref[...]
        o = _dot_nt((q * jnp.exp(cum)).astype(BF16), st.astype(BF16))
        for s in range(sub):
            dec = jnp.exp(jnp.where(row_id >= s, cum - cum[s:s + 1, :], NEG_BIG))
            m_scr[s * sub:(s + 1) * sub, :] = q * k[s:s + 1, :] * dec
        w = _dot(m_scr[...].astype(BF16), ones_bd)
        for s in range(sub):
            o = o + w[s * sub:(s + 1) * sub, :] * v[s:s + 1, :]
        o_ref[rows, :] = o
        kd = k * jnp.exp(last - cum)
        upd = _dot_tn(v.astype(BF16), kd.astype(BF16))
        st_ref[...] = st * jnp.exp(last) + jnp.where(same_head, upd, 0.0)
        return 0

    lax.fori_loop(0, seq // sub, chunk, 0)

    tile = min(seq, 256)
    gn = gn_ref[...]
    for t0 in range(0, seq, tile):
        o = o_ref[t0:t0 + tile, :]
        sq = o * o
        hi = sq.astype(BF16)
        lo = (sq - hi.astype(F32)).astype(BF16)
        ms = (_dot(hi, ones_bd) + _dot(lo, ones_bd)) * (1.0 / HEAD_DIM)
        o_ref[t0:t0 + tile, :] = o * lax.rsqrt(ms + EPS) * gn * gs_ref[t0:t0 + tile, :]


def _hgrn(q, k, lf, v, gs, gn, st0, batch, seq):
    sub = math.gcd(seq, 16)
    row = pl.BlockSpec((seq, C_W), lambda b: (b, 0))
    st_spec = pl.BlockSpec((None, C_W, C_W), lambda b: (b, 0, 0))
    est = 2 * 6 * seq * C_W * 4 + 8 * C_W * C_W * 4 + sub * sub * C_W * 4
    return pl.pallas_call(
        functools.partial(_hgrn_body, sub=sub),
        out_shape=[jax.ShapeDtypeStruct((batch * seq, C_W), F32),
                   jax.ShapeDtypeStruct((batch, C_W, C_W), F32)],
        grid=(batch,),
        in_specs=[row, row, row, row, row, pl.BlockSpec((1, C_W), lambda b: (0, 0)), st_spec],
        out_specs=[row, st_spec],
        scratch_shapes=[pltpu.VMEM((sub * sub, C_W), F32)],
        compiler_params=pltpu.CompilerParams(
            dimension_semantics=("arbitrary",), vmem_limit_bytes=_vmem_limit(est)),
        name="hgrn",
    )(q, k, lf, v, gs, gn, st0)


def _block_diag_q(q, heads, scale):
    t, w = q.shape
    rep = jnp.concatenate([q] * heads, axis=0)
    r = lax.broadcasted_iota(jnp.int32, (heads * t, w), 0) // t
    c = lax.broadcasted_iota(jnp.int32, (heads * t, w), 1) // HEAD_DIM
    return jnp.where(r == c, rep * scale, 0.0)


def _collapse_heads(o, heads, t):
    w = o.shape[1]
    r = lax.broadcasted_iota(jnp.int32, o.shape, 0) // t
    c = lax.broadcasted_iota(jnp.int32, o.shape, 1) // HEAD_DIM
    o = jnp.where(r == c, o, 0.0)
    out = o[0:t, :]
    for h in range(1, heads):
        out = out + o[h * t:(h + 1) * t, :]
    return out


def _moba_decode_body(pt_ref, q_ref, kn_ref, vn_ref, *rest, pp, page, n_steps, t_new):
    k_refs, v_refs = rest[:pp], rest[pp:2 * pp]
    o_ref, qbd_ref, s_ref, p_ref, kmean_ref, acc_ref, l_ref = rest[2 * pp:]
    ph, j = pl.program_id(1), pl.program_id(2)
    rows = A_HEADS * t_new
    n_pages = n_steps * pp
    per_blk = MOBA_BLOCK // page
    n_blk = n_pages // per_blk
    scale = HEAD_DIM ** -0.5

    @pl.when((ph == 0) & (j == 0))
    def _():
        qbd_ref[...] = _block_diag_q(q_ref[...], A_HEADS, 1.0)

    @pl.when(ph == 0)
    def _():
        qb = (qbd_ref[...] * scale).astype(BF16)
        for g in range(pp // per_blk):
            ksum = jnp.zeros((1, A_W), F32)
            for u in range(per_blk):
                jj = g * per_blk + u
                kp = k_refs[jj][...]
                s_ref[j * pp + jj] = _dot_nt(qb, kp.astype(BF16))
                ksum = ksum + jnp.sum(kp, axis=0, keepdims=True)
            kmean_ref[pl.ds(j * (pp // per_blk) + g, 1), :] = ksum * (1.0 / MOBA_BLOCK)

    @pl.when((ph == 0) & (j == n_steps - 1))
    def _():
        qbd = qbd_ref[...]
        gate = _dot_nt(qbd, kmean_ref[...], precision=HIGHEST)
        lane_blk = lax.broadcasted_iota(jnp.int32, (rows, n_blk), 1)
        sel = _topk_select(gate, lane_blk >= 0, lane_blk, n_blk, MOBA_TOPK)
        qb = (qbd * scale).astype(BF16)
        s_own = _dot_nt(qb, kn_ref[...].astype(BF16))
        qi = lax.broadcasted_iota(jnp.int32, (rows, t_new), 0) % t_new
        kj = lax.broadcasted_iota(jnp.int32, (rows, t_new), 1)
        s_own = jnp.where(kj <= qi, s_own, NEG_BIG)
        mx = jnp.max(s_own, axis=1, keepdims=True)
        for pg in range(n_pages):
            n = pg // per_blk
            sm = jnp.where(sel[:, n:n + 1] > 0.0, s_ref[pg], NEG_BIG)
            s_ref[pg] = sm
            mx = jnp.maximum(mx, jnp.max(sm, axis=1, keepdims=True))
        p_own = jnp.exp(s_own - mx)
        l = jnp.sum(p_own, axis=1, keepdims=True)
        for pg in range(n_pages):
            p = jnp.exp(s_ref[pg] - mx)
            l = l + jnp.sum(p, axis=1, keepdims=True)
            p_ref[pg] = p.astype(BF16)
        l_ref[...] = l
        acc_ref[...] = _dot(p_own.astype(BF16), vn_ref[...].astype(BF16))

    @pl.when(ph == 1)
    def _():
        acc = acc_ref[...]
        for jj in range(pp):
            acc = acc + _dot(p_ref[j * pp + jj], v_refs[jj][...].astype(BF16))
        acc_ref[...] = acc

    @pl.when((ph == 1) & (j == n_steps - 1))
    def _():
        o_ref[...] = _collapse_heads(acc_ref[...] / l_ref[...], A_HEADS, t_new)


def _page_specs(block, layer, pp, n_steps, first_phase):
    specs = []
    for jj in range(pp):
        def imap(b, ph, j, pt, jj=jj):
            if first_phase:
                step = jnp.where(ph == 0, j, n_steps - 1)
            else:
                step = jnp.where(ph == 0, 0, j)
            return (layer, pt[b, step * pp + jj], 0, 0)
        specs.append(pl.BlockSpec((None, None) + block, imap))
    return specs


def _moba_decode(page_table, q, k_new, v_new, cache_k, cache_v, layer, n_seq, t_new, pp=8):
    n_pages = page_table.shape[1]
    page = cache_k.shape[2]
    assert MOBA_BLOCK % page == 0 and n_pages % pp == 0 and pp % (MOBA_BLOCK // page) == 0
    n_steps = n_pages // pp
    past = n_pages * page
    rows = A_HEADS * t_new
    new = pl.BlockSpec((t_new, A_W), lambda b, ph, j, pt: (b, 0))
    est = 2 * 2 * pp * page * A_W * 4 + rows * past * 6 + 4 * MOBA_BLOCK * A_W * 4
    grid_spec = pltpu.PrefetchScalarGridSpec(
        num_scalar_prefetch=1,
        grid=(n_seq, 2, n_steps),
        in_specs=[new, new, new]
        + _page_specs((page, A_W), layer, pp, n_steps, True)
        + _page_specs((page, A_W), layer, pp, n_steps, False),
        out_specs=new,
        scratch_shapes=[pltpu.VMEM((rows, A_W), F32), pltpu.VMEM((n_pages, rows, page), F32),
                        pltpu.VMEM((n_pages, rows, page), BF16), pltpu.VMEM((past // MOBA_BLOCK, A_W), F32),
                        pltpu.VMEM((rows, A_W), F32), pltpu.VMEM((rows, 1), F32)],
    )
    return pl.pallas_call(
        functools.partial(_moba_decode_body, pp=pp, page=page, n_steps=n_steps, t_new=t_new),
        out_shape=jax.ShapeDtypeStruct((n_seq * t_new, A_W), F32),
        grid_spec=grid_spec,
        compiler_params=pltpu.CompilerParams(
            dimension_semantics=("arbitrary", "arbitrary", "arbitrary"), vmem_limit_bytes=_vmem_limit(est)),
        name="moba_decode",
    )(page_table, q, k_new, v_new, *([cache_k] * pp), *([cache_v] * pp))


def _rows_per_head(x, heads, t):
    return jnp.concatenate([jnp.broadcast_to(x[h:h + 1, :], (t, x.shape[1])) for h in range(heads)], axis=0)


def _fox_decode_body(pt_ref, q_ref, kn_ref, vn_ref, lfn_ref, *rest, pp, page, n_steps, t_new):
    k_refs, v_refs, lf_refs = rest[:pp], rest[pp:2 * pp], rest[2 * pp:3 * pp]
    o_ref, qbd_ref, s_ref, p_ref, carry_ref, acc_ref, l_ref = rest[3 * pp:]
    ph, j = pl.program_id(1), pl.program_id(2)
    rows = B_HEADS * t_new
    n_pages = n_steps * pp
    past = n_pages * page
    scale = HEAD_DIM ** -0.5

    @pl.when((ph == 0) & (j == 0))
    def _():
        qbd_ref[...] = _block_diag_q(q_ref[...], B_HEADS, scale).astype(BF16)
        carry_ref[...] = jnp.zeros(carry_ref.shape, F32)

    @pl.when(ph == 0)
    def _():
        qb = qbd_ref[...]
        r = lax.broadcasted_iota(jnp.int32, (page, page), 0)
        c = lax.broadcasted_iota(jnp.int32, (page, page), 1)
        upper = (r <= c).astype(F32)
        carry = carry_ref[...]
        for jj in range(pp):
            lf = lf_refs[jj][...]
            cpage = _dot(lf, upper, precision=HIGHEST) + carry
            carry = carry + jnp.sum(lf, axis=1, keepdims=True)
            s_ref[j * pp + jj] = (_dot_nt(qb, k_refs[jj][...].astype(BF16))
                                  - _rows_per_head(cpage, B_HEADS, t_new))
        carry_ref[...] = carry

    @pl.when((ph == 0) & (j == n_steps - 1))
    def _():
        qb = qbd_ref[...]
        r8 = lax.broadcasted_iota(jnp.int32, (t_new, t_new), 0)
        c8 = lax.broadcasted_iota(jnp.int32, (t_new, t_new), 1)
        cum_new = _dot(lfn_ref[...], (r8 <= c8).astype(F32), precision=HIGHEST)
        cn = _rows_per_head(cum_new, B_HEADS, t_new)
        qi = lax.broadcasted_iota(jnp.int32, (rows, t_new), 0) % t_new
        kj = lax.broadcasted_iota(jnp.int32, (rows, t_new), 1)
        cq_new = jnp.sum(jnp.where(kj == qi, cn, 0.0), axis=1, keepdims=True)
        ct = _rows_per_head(carry_ref[...], B_HEADS, t_new) + cq_new
        s_own = _dot_nt(qb, kn_ref[...].astype(BF16)) + (cq_new - cn)
        s_own = jnp.where(kj <= qi, s_own, NEG_BIG)
        mx = jnp.max(s_own, axis=1, keepdims=True)
        for pg in range(n_pages):
            sm = s_ref[pg] + ct
            s_ref[pg] = sm
            mx = jnp.maximum(mx, jnp.max(sm, axis=1, keepdims=True))
        p_own = jnp.exp(s_own - mx)
        l = jnp.sum(p_own, axis=1, keepdims=True)
        for pg in range(n_pages):
            p = jnp.exp(s_ref[pg] - mx)
            l = l + jnp.sum(p, axis=1, keepdims=True)
            p_ref[pg] = p.astype(BF16)
        l_ref[...] = l
        acc_ref[...] = _dot(p_own.astype(BF16), vn_ref[...].astype(BF16))

    @pl.when(ph == 1)
    def _():
        acc = acc_ref[...]
        for jj in range(pp):
            acc = acc + _dot(p_ref[j * pp + jj], v_refs[jj][...].astype(BF16))
        acc_ref[...] = acc

    @pl.when((ph == 1) & (j == n_steps - 1))
    def _():
        o_ref[...] = _collapse_heads(acc_ref[...] / l_ref[...], B_HEADS, t_new)


def _fox_decode(page_table, q, k_new, v_new, lf_new_t, cache_k, cache_v, cache_lf_t, layer, n_seq, t_new, pp=8):
    n_pages = page_table.shape[1]
    page = cache_k.shape[2]
    assert n_pages % pp == 0
    n_steps = n_pages // pp
    rows = B_HEADS * t_new
    new = pl.BlockSpec((t_new, B_W), lambda b, ph, j, pt: (b, 0))
    est = 2 * 2 * pp * page * (B_W + 8) * 4 + rows * n_pages * page * 6
    grid_spec = pltpu.PrefetchScalarGridSpec(
        num_scalar_prefetch=1,
        grid=(n_seq, 2, n_steps),
        in_specs=[new, new, new, pl.BlockSpec((None, B_HEADS, t_new), lambda b, ph, j, pt: (b, 0, 0))]
        + _page_specs((page, B_W), layer, pp, n_steps, True)
        + _page_specs((page, B_W), layer, pp, n_steps, False)
        + _page_specs((B_HEADS, page), layer, pp, n_steps, True),
        out_specs=new,
        scratch_shapes=[pltpu.VMEM((rows, B_W), BF16), pltpu.VMEM((n_pages, rows, page), F32),
                        pltpu.VMEM((n_pages, rows, page), BF16), pltpu.VMEM((B_HEADS, 1), F32),
                        pltpu.VMEM((rows, B_W), F32), pltpu.VMEM((rows, 1), F32)],
    )
    return pl.pallas_call(
        functools.partial(_fox_decode_body, pp=pp, page=page, n_steps=n_steps, t_new=t_new),
        out_shape=jax.ShapeDtypeStruct((n_seq * t_new, B_W), F32),
        grid_spec=grid_spec,
        compiler_params=pltpu.CompilerParams(
            dimension_semantics=("arbitrary", "arbitrary", "arbitrary"), vmem_limit_bytes=_vmem_limit(est)),
        name="fox_decode",
    )(page_table, q, k_new, v_new, lf_new_t, *([cache_k] * pp), *([cache_v] * pp), *([cache_lf_t] * pp))


def _pad_w_in(w):
    sizes = (A_W,) * 3 + (B_W,) * 3 + (B_HEADS,) + (C_W,) * 4
    offs = np.cumsum((0,) + sizes)
    segs = [SEG_QA, SEG_KA, SEG_VA, SEG_QF, SEG_KF, SEG_VF, SEG_FF, SEG_QC, SEG_FC, SEG_IC, SEG_GC]
    out = jnp.zeros((w.shape[0], D_IN_PAD), BF16)
    for s, o, n in zip(segs, offs[:-1], sizes):
        out = out.at[:, s:s + n].set(w[:, o:o + n].astype(BF16))
    return out


def _rope_tables(pos0, t, reps):
    half = HEAD_DIM // 2
    inv = ROPE_THETA ** (-jnp.arange(half, dtype=F32) * 2.0 / HEAD_DIM)
    ang = (pos0 + jnp.arange(t, dtype=F32))[:, None] * inv[None, :]
    cos, sin, zero = jnp.cos(ang), jnp.sin(ang), jnp.zeros((t, half), F32)
    heads_per_vreg = LANES // HEAD_DIM
    tabs = (jnp.concatenate([cos, cos] * heads_per_vreg, axis=1),
            jnp.concatenate([-sin, zero] * heads_per_vreg, axis=1),
            jnp.concatenate([zero, sin] * heads_per_vreg, axis=1))
    return tuple(jnp.tile(tb, (reps, 1)) for tb in tabs)


def _pad_lanes(a, width):
    return jnp.pad(a, ((0, 0), (0, width - a.shape[1])))


def _state_to_bd(s):
    b, h, dk, dv = s.shape
    st = jnp.swapaxes(s, 2, 3)
    eye = jnp.eye(h, dtype=s.dtype)
    return (st[:, :, :, None, :] * eye[None, :, None, :, None]).reshape(b, h * dv, h * dk)


def _bd_to_state(st, h):
    b = st.shape[0]
    blocks = st.reshape(b, h, HEAD_DIM, h, HEAD_DIM)
    diag = jnp.stack([blocks[:, i, :, i, :] for i in range(h)], axis=1)
    return jnp.swapaxes(diag, 2, 3)


def kernel(x_prompt, x_sample, cache_moba_k, cache_moba_v, cache_fox_k, cache_fox_v, cache_fox_logf, state_hgrn, page_table, norm_ffn1_g, ffn1_w_gate, ffn1_w_up, ffn1_w_down, norm_mix_g, w_in, fox_f_bias, hgrn_lb_logits, hgrn_norm_g, w_out, norm_ffn2_g, ffn2_w_gate, ffn2_w_up, ffn2_w_down, norm_final_g):
    depth = w_in.shape[0]
    batch, seq, d = x_prompt.shape
    n_seq, t_new, _ = x_sample.shape
    n_pool, page = cache_moba_k.shape[1], cache_moba_k.shape[2]
    past_len = page_table.shape[1] * page

    bf = lambda a: a.astype(BF16)
    row = lambda a: a.reshape(1, -1)
    w1g, w1u, w1d = bf(ffn1_w_gate), bf(ffn1_w_up), bf(ffn1_w_down)
    w2g, w2u, w2d = bf(ffn2_w_gate), bf(ffn2_w_up), bf(ffn2_w_down)
    wo = bf(w_out)
    w_in_pad = [_pad_w_in(w_in[l]) for l in range(depth)]
    fb = [_pad_lanes(row(fox_f_bias[l]), LANES) for l in range(depth)]
    lbl = _pad_lanes(hgrn_lb_logits.astype(F32), SEG_W)
    gn = jnp.tile(hgrn_norm_g, (1, C_HEADS))
    ck = cache_moba_k.reshape(depth, n_pool, page, A_W)
    cv = cache_moba_v.reshape(depth, n_pool, page, A_W)
    fk = cache_fox_k.reshape(depth, n_pool, page, B_W)
    fv = cache_fox_v.reshape(depth, n_pool, page, B_W)
    flt = jnp.swapaxes(cache_fox_logf, 2, 3)

    def trunk(x, sample):
        if sample:
            nb, t = n_seq, t_new
            tabs = _rope_tables(float(past_len), t, nb)
        else:
            nb, t = batch, seq
            tabs = _rope_tables(0.0, t, 1)
        new = []
        for l in range(depth):
            if l == 0:
                x = _ffn(x, row(norm_ffn1_g[l]), w1g[l], w1u[l], w1d[l])
            (qa, ka, va, qf, kf, vf, lf, qc, kc, lc, ic, gc) = _inproj(
                x, row(norm_mix_g[l]), w_in_pad[l], tabs, fb[l], lbl, l)
            if sample:
                oa = _moba_decode(page_table, qa, ka, va, ck, cv, l, nb, t)
                lf_t = jnp.swapaxes(lf[:, :B_HEADS].reshape(nb, t, B_HEADS), 1, 2)
                of = _fox_decode(page_table, qf, kf, vf, lf_t, fk, fv, flt, l, nb, t)
                st0 = _state_to_bd(state_hgrn[l].astype(F32))
            else:
                oa = _moba_prompt(qa, ka, va, nb, t)
                of = _fox_prompt(qf, kf, vf, lf, nb, t)
                st0 = jnp.zeros((nb, C_W, C_W), F32)
            oc, st = _hgrn(qc, kc, lc, ic, gc, row(gn[l]), st0, nb, t)
            mix = (oa, of, oc, wo[l, :A_W], wo[l, A_W:A_W + B_W], wo[l, A_W + B_W:])
            last = l == depth - 1
            x = _ffn(x, row(norm_ffn2_g[l]), w2g[l], w2u[l], w2d[l], mix=mix,
                     g_final=row(norm_final_g) if last else None)
            if not last:
                x = _ffn(x, row(norm_ffn1_g[l + 1]), w1g[l + 1], w1u[l + 1], w1d[l + 1])
            new.append((ka.reshape(nb, t, A_HEADS, HEAD_DIM), va.reshape(nb, t, A_HEADS, HEAD_DIM),
                        kf.reshape(nb, t, B_HEADS, HEAD_DIM), vf.reshape(nb, t, B_HEADS, HEAD_DIM),
                        lf[:, :B_HEADS].reshape(nb, t, B_HEADS), _bd_to_state(st, C_HEADS)))
        return x.reshape(nb, t, d), [jnp.stack(z) for z in zip(*new)]

    y_p, (mk_p, mv_p, fk_p, fv_p, fl_p, hs_p) = trunk(x_prompt.reshape(batch * seq, d), False)
    y_s, (mk_s, mv_s, fk_s, fv_s, fl_s, hs_s) = trunk(x_sample.reshape(n_seq * t_new, d), True)
    return (y_p, y_s, mk_p, mv_p, fk_p, fv_p, fl_p, hs_p, mk_s, mv_s, fk_s, fv_s, fl_s, hs_s)
```

```python
import functools
import math

import numpy as np
import jax
import jax.numpy as jnp
from jax import lax
from jax.experimental import pallas as pl
from jax.experimental.pallas import tpu as pltpu

F32 = jnp.float32
BF16 = jnp.bfloat16
HIGHEST = lax.Precision.HIGHEST

HEAD_DIM = 64
A_HEADS = 6
B_HEADS = 5
C_HEADS = 5
A_W = A_HEADS * HEAD_DIM
B_W = B_HEADS * HEAD_DIM
C_W = C_HEADS * HEAD_DIM
D_FF_CHUNK = 256
MOBA_BLOCK = 256
MOBA_TOPK = 3
ROPE_THETA = 10000.0
EPS = 1e-6
NEG_BIG = -1e30

LANES = 128
SUBLANES = 8
VMEM_PHYSICAL = 64 * 1024 * 1024
VMEM_CAP = VMEM_PHYSICAL - 8 * 1024 * 1024

SEG_W = 384
SEG_QA, SEG_KA, SEG_VA, SEG_QF, SEG_KF, SEG_VF, SEG_QC, SEG_FC, SEG_IC, SEG_GC = (
    i * SEG_W for i in range(10))
SEG_FF = 10 * SEG_W
D_IN_PAD = SEG_FF + LANES


def _vmem_limit(estimate_bytes):
    return int(min(max(2 * estimate_bytes, 32 * 1024 * 1024), VMEM_CAP))


def _dot(a, b, precision=None):
    return jnp.dot(a, b, preferred_element_type=F32, precision=precision)


def _dot_nt(a, b, precision=None):
    return lax.dot_general(a, b, (((1,), (1,)), ((), ())),
                           preferred_element_type=F32, precision=precision)


def _dot_tn(a, b, precision=None):
    return lax.dot_general(a, b, (((0,), (0,)), ((), ())),
                           preferred_element_type=F32, precision=precision)


def _rms(x, g):
    ms = jnp.mean(x * x, axis=-1, keepdims=True)
    return x * lax.rsqrt(ms + EPS) * g


def _sigmoid(x):
    return 1.0 / (1.0 + jnp.exp(-x))


def _silu(x):
    return x * _sigmoid(x)


def _log_sigmoid(x):
    return jnp.minimum(x, 0.0) - jnp.log1p(jnp.exp(-jnp.abs(x)))


def _ffn_body(*refs, has_mix, has_final, d_ff):
    it = iter(refs)
    x_ref = next(it)
    if has_mix:
        oa_ref, of_ref, oc_ref, woa_ref, wof_ref, woc_ref = (next(it) for _ in range(6))
    g_ref, wg_ref, wu_ref, wd_ref = (next(it) for _ in range(4))
    gf_ref = next(it) if has_final else None
    out_ref = next(it)

    x = x_ref[...]
    if has_mix:
        x = (x + _dot(oa_ref[...].astype(BF16), woa_ref[...])
             + _dot(of_ref[...].astype(BF16), wof_ref[...])
             + _dot(oc_ref[...].astype(BF16), woc_ref[...]))
    h = _rms(x, g_ref[...]).astype(BF16)
    acc = jnp.zeros(x.shape, F32)
    for c0 in range(0, d_ff, D_FF_CHUNK):
        gate = _dot(h, wg_ref[:, c0:c0 + D_FF_CHUNK])
        up = _dot(h, wu_ref[:, c0:c0 + D_FF_CHUNK])
        act = (_silu(gate) * up).astype(BF16)
        acc = acc + _dot(act, wd_ref[c0:c0 + D_FF_CHUNK, :])
    y = x + 0.5 * acc
    if has_final:
        y = _rms(y, gf_ref[...])
    out_ref[...] = y


def _ffn(x, g, wg, wu, wd, mix=None, g_final=None, tm=512):
    n, d = x.shape
    d_ff = wg.shape[1]
    tm = min(tm, n)
    assert n % tm == 0 and d_ff % D_FF_CHUNK == 0
    row = lambda w: pl.BlockSpec((tm, w), lambda i: (i, 0))
    full = lambda a: pl.BlockSpec(a.shape, lambda i: (0,) * a.ndim, pipeline_mode=pl.Buffered(1))
    args, specs = [x], [row(d)]
    if mix is not None:
        oa, of, oc, woa, wof, woc = mix
        args += [oa, of, oc, woa, wof, woc]
        specs += [row(oa.shape[1]), row(of.shape[1]), row(oc.shape[1]), full(woa), full(wof), full(woc)]
    args += [g, wg, wu, wd]
    specs += [full(g), full(wg), full(wu), full(wd)]
    if g_final is not None:
        args.append(g_final)
        specs.append(full(g_final))
    weights = 2 * 3 * d * d_ff + (2 * d * d if mix is not None else 0)
    tiles = 2 * 2 * tm * d * 4 * (2 if mix is not None else 1) + 6 * tm * d * 4
    return pl.pallas_call(
        functools.partial(_ffn_body, has_mix=mix is not None, has_final=g_final is not None, d_ff=d_ff),
        out_shape=jax.ShapeDtypeStruct((n, d), F32),
        grid=(n // tm,),
        in_specs=specs,
        out_specs=row(d),
        compiler_params=pltpu.CompilerParams(
            dimension_semantics=("arbitrary",), vmem_limit_bytes=_vmem_limit(weights + tiles)),
        name="ffn_mix" if mix is not None else "ffn",
    )(*args)


def _inproj_body(x_ref, g_ref, w_ref, cos_ref, slo_ref, shi_ref, fb_ref, lbl_ref,
                 qa_ref, ka_ref, va_ref, qf_ref, kf_ref, vf_ref, lf_ref,
                 qc_ref, kc_ref, lc_ref, ic_ref, gc_ref, *, layer):
    h = _rms(x_ref[...], g_ref[...]).astype(BF16)

    def proj(c0, width=SEG_W):
        return _dot(h, w_ref[:, c0:c0 + width])

    cos, slo, shi = cos_ref[...], slo_ref[...], shi_ref[...]

    def rope(p):
        parts = []
        for c in range(SEG_W // LANES):
            pc = p[:, c * LANES:(c + 1) * LANES]
            parts.append(pc * cos + pltpu.roll(pc, LANES - HEAD_DIM // 2, 1) * slo
                         + pltpu.roll(pc, HEAD_DIM // 2, 1) * shi)
        return jnp.concatenate(parts, axis=1)

    qa_ref[...] = rope(proj(SEG_QA))
    ka_ref[...] = rope(proj(SEG_KA))
    va_ref[...] = proj(SEG_VA)
    qf_ref[...] = proj(SEG_QF)[:, :B_W]
    kf_ref[...] = proj(SEG_KF)[:, :B_W]
    vf_ref[...] = proj(SEG_VF)[:, :B_W]
    lf_ref[...] = _log_sigmoid(proj(SEG_FF, LANES) + fb_ref[...])

    lbl = lbl_ref[...]
    e = jnp.exp(lbl - jnp.max(lbl, axis=0, keepdims=True))
    psm = e / jnp.sum(e, axis=0, keepdims=True)
    lb = jnp.zeros((1, SEG_W), F32)
    for j in range(1, layer + 1):
        lb = lb + psm[j:j + 1, :]
    fc = proj(SEG_FC)
    lc_ref[...] = jnp.log(lb + (1.0 - lb) * _sigmoid(fc))[:, :C_W]
    kc_ref[...] = ((1.0 - lb) * _sigmoid(-fc))[:, :C_W]
    qc_ref[...] = _silu(proj(SEG_QC))[:, :C_W]
    ic_ref[...] = proj(SEG_IC)[:, :C_W]
    gc_ref[...] = _silu(proj(SEG_GC))[:, :C_W]


def _inproj(x, g, w_pad, rope_tabs, fb, lbl, layer, tm=256):
    n, d = x.shape
    tm = min(tm, n)
    assert n % tm == 0 and rope_tabs[0].shape[0] % tm == 0
    row = lambda w: pl.BlockSpec((tm, w), lambda i: (i, 0))
    full = lambda a: pl.BlockSpec(a.shape, lambda i: (0,) * a.ndim, pipeline_mode=pl.Buffered(1))
    n_tab = rope_tabs[0].shape[0] // tm
    tab = pl.BlockSpec((tm, LANES), lambda i: (i % n_tab, 0))
    widths = [A_W, A_W, A_W, B_W, B_W, B_W, LANES, C_W, C_W, C_W, C_W, C_W]
    est = 2 * d * D_IN_PAD + 2 * tm * 4 * (d + sum(widths) + 3 * LANES) + 8 * tm * SEG_W * 4
    return pl.pallas_call(
        functools.partial(_inproj_body, layer=layer),
        out_shape=[jax.ShapeDtypeStruct((n, w), F32) for w in widths],
        grid=(n // tm,),
        in_specs=[row(d), full(g), full(w_pad), tab, tab, tab, full(fb), full(lbl)],
        out_specs=[row(w) for w in widths],
        compiler_params=pltpu.CompilerParams(
            dimension_semantics=("arbitrary",), vmem_limit_bytes=_vmem_limit(est)),
        name="inproj",
    )(x, g, w_pad, *rope_tabs, fb, lbl)


def _topk_select(gate, allowed, lane_idx, n_lanes, topk):
    g = jnp.where(allowed, gate, NEG_BIG)
    sel = jnp.zeros(gate.shape, F32)
    lane_f = lane_idx.astype(F32)
    for _ in range(topk):
        m = jnp.max(g, axis=1, keepdims=True)
        idx = jnp.min(jnp.where(g == m, lane_f, float(n_lanes)), axis=1, keepdims=True)
        hit = lane_f == idx
        sel = jnp.where(hit & allowed, 1.0, sel)
        g = jnp.where(hit, -jnp.inf, g)
    return sel


def _windows(width):
    return [(c0, min(LANES, width - c0), min(LANES, width - c0) // HEAD_DIM) for c0 in range(0, width, LANES)]


def _split_heads(win, n_heads):
    if n_heads == 1:
        return [win]
    low = lax.broadcasted_iota(jnp.int32, win.shape, 1) < HEAD_DIM
    return [jnp.where(low, win, 0.0), jnp.where(low, 0.0, win)]


def _flash_step(rows, mask_fn, qm_ref, k_ref, v_ref, m_ref, l_ref, acc_ref, width):
    h = 0
    for w_i, (c0, w, n_heads) in enumerate(_windows(width)):
        kb = k_ref[rows, c0:c0 + w].astype(BF16)
        vb = v_ref[rows, c0:c0 + w].astype(BF16)
        alphas, pvs = [], []
        for _ in range(n_heads):
            s = mask_fn(h, _dot_nt(qm_ref[h, :, :w], kb))
            m_old = m_ref[h]
            m_new = jnp.maximum(m_old, jnp.max(s, axis=1, keepdims=True))
            alpha = jnp.exp(m_old - m_new)
            p = jnp.exp(s - m_new)
            l_ref[h] = alpha * l_ref[h] + jnp.sum(p, axis=1, keepdims=True)
            m_ref[h] = m_new
            alphas.append(alpha)
            pvs.append(_dot(p.astype(BF16), vb))
            h += 1
        if n_heads == 2:
            low = lax.broadcasted_iota(jnp.int32, pvs[0].shape, 1) < HEAD_DIM
            alpha, pv = jnp.where(low, alphas[0], alphas[1]), jnp.where(low, pvs[0], pvs[1])
        else:
            alpha, pv = alphas[0], pvs[0]
        acc_ref[w_i, :, :w] = acc_ref[w_i, :, :w] * alpha + pv


def _flash_init(m_ref, l_ref, acc_ref):
    m_ref[...] = jnp.full(m_ref.shape, -jnp.inf, F32)
    l_ref[...] = jnp.zeros(l_ref.shape, F32)
    acc_ref[...] = jnp.zeros(acc_ref.shape, F32)


def _flash_finish(o_ref, l_ref, acc_ref, width):
    h = 0
    for w_i, (c0, w, n_heads) in enumerate(_windows(width)):
        if n_heads == 2:
            low = lax.broadcasted_iota(jnp.int32, (o_ref.shape[0], w), 1) < HEAD_DIM
            l = jnp.where(low, l_ref[h], l_ref[h + 1])
        else:
            l = l_ref[h]
        o_ref[:, c0:c0 + w] = acc_ref[w_i, :, :w] / l
        h += n_heads


def _moba_prompt_body(q_ref, k_ref, v_ref, o_ref, kmean_ref, qm_ref, sel_ref, m_ref, l_ref, acc_ref, *, blk):
    i = pl.program_id(1)
    nb = k_ref.shape[0] // blk
    scale = HEAD_DIM ** -0.5

    @pl.when(i == 0)
    def _():
        for n in range(nb):
            kmean_ref[n:n + 1, :] = jnp.sum(k_ref[n * blk:(n + 1) * blk, :], axis=0, keepdims=True) * (1.0 / blk)

    row = lax.broadcasted_iota(jnp.int32, (blk, blk), 0)
    col = lax.broadcasted_iota(jnp.int32, (blk, blk), 1)
    causal = col <= row
    lane_blk = lax.broadcasted_iota(jnp.int32, (blk, nb), 1)
    past = lane_blk < i

    h = 0
    for c0, w, n_heads in _windows(A_W):
        for qh in _split_heads(q_ref[:, c0:c0 + w], n_heads):
            gate = _dot_nt(qh, kmean_ref[:, c0:c0 + w], precision=HIGHEST)
            sel_ref[h] = _topk_select(gate, past, lane_blk, nb, MOBA_TOPK)
            qm_ref[h, :, :w] = (qh * scale).astype(BF16)
            h += 1
    _flash_init(m_ref, l_ref, acc_ref)

    def own_mask(h, s):
        return jnp.where(causal, s, NEG_BIG)

    _flash_step(pl.ds(pl.multiple_of(i * blk, blk), blk), own_mask, qm_ref, k_ref, v_ref,
                m_ref, l_ref, acc_ref, A_W)

    def body(n, carry):
        def past_mask(h, s):
            sel_n = jnp.sum(jnp.where(lane_blk == n, sel_ref[h], 0.0), axis=1, keepdims=True)
            return jnp.where(sel_n > 0.0, s, NEG_BIG)
        _flash_step(pl.ds(pl.multiple_of(n * blk, blk), blk), past_mask, qm_ref, k_ref, v_ref,
                    m_ref, l_ref, acc_ref, A_W)
        return carry

    lax.fori_loop(0, i, body, 0)
    _flash_finish(o_ref, l_ref, acc_ref, A_W)


def _flash_scratch(heads, width, blk):
    return [pltpu.VMEM((heads, blk, LANES), BF16),
            pltpu.VMEM((heads, blk, 1), F32), pltpu.VMEM((heads, blk, 1), F32),
            pltpu.VMEM((len(_windows(width)), blk, LANES), F32)]


def _moba_prompt(q, k, v, batch, seq):
    blk = MOBA_BLOCK
    assert seq % blk == 0
    nq = seq // blk
    est = 2 * 2 * seq * A_W * 4 + 4 * blk * A_W * 4 + 24 * blk * blk * 4 + 3 * A_HEADS * blk * LANES * 4
    qm, m, l, acc = _flash_scratch(A_HEADS, A_W, blk)
    return pl.pallas_call(
        functools.partial(_moba_prompt_body, blk=blk),
        out_shape=jax.ShapeDtypeStruct(q.shape, F32),
        grid=(batch, nq),
        in_specs=[pl.BlockSpec((blk, A_W), lambda b, i: (b * nq + i, 0)),
                  pl.BlockSpec((seq, A_W), lambda b, i: (b, 0)),
                  pl.BlockSpec((seq, A_W), lambda b, i: (b, 0))],
        out_specs=pl.BlockSpec((blk, A_W), lambda b, i: (b * nq + i, 0)),
        scratch_shapes=[pltpu.VMEM((nq, A_W), F32), qm, pltpu.VMEM((A_HEADS, blk, nq), F32), m, l, acc],
        compiler_params=pltpu.CompilerParams(
            dimension_semantics=("arbitrary", "arbitrary"), vmem_limit_bytes=_vmem_limit(est)),
        name="moba_prompt",
    )(q, k, v)


def _fox_prompt_body(q_ref, k_ref, v_ref, lf_ref, o_ref, ccol_ref, crow_ref, qm_ref, m_ref, l_ref, acc_ref, *, blk):
    i = pl.program_id(1)
    nb = k_ref.shape[0] // blk
    scale = HEAD_DIM ** -0.5
    row = lax.broadcasted_iota(jnp.int32, (blk, blk), 0)
    col = lax.broadcasted_iota(jnp.int32, (blk, blk), 1)
    causal = col <= row

    @pl.when(i == 0)
    def _():
        tri = causal.astype(F32)
        carry = jnp.zeros((1, LANES), F32)
        for n in range(nb):
            c = _dot(tri, lf_ref[n * blk:(n + 1) * blk, :], precision=HIGHEST) + carry
            ccol_ref[n * blk:(n + 1) * blk, :] = c
            crow_ref[n] = c.T
            carry = c[blk - 1:blk, :]

    h = 0
    for c0, w, n_heads in _windows(B_W):
        for qh in _split_heads(q_ref[:, c0:c0 + w], n_heads):
            qm_ref[h, :, :w] = (qh * scale).astype(BF16)
            h += 1
    _flash_init(m_ref, l_ref, acc_ref)
    cq_all = ccol_ref[pl.ds(pl.multiple_of(i * blk, blk), blk), :]

    def bias(h, n, s):
        return s + (cq_all[:, h:h + 1] - crow_ref[n, h:h + 1, :])

    _flash_step(pl.ds(pl.multiple_of(i * blk, blk), blk),
                lambda h, s: jnp.where(causal, bias(h, i, s), NEG_BIG),
                qm_ref, k_ref, v_ref, m_ref, l_ref, acc_ref, B_W)

    def body(n, carry):
        _flash_step(pl.ds(pl.multiple_of(n * blk, blk), blk), lambda h, s: bias(h, n, s),
                    qm_ref, k_ref, v_ref, m_ref, l_ref, acc_ref, B_W)
        return carry

    lax.fori_loop(0, i, body, 0)
    _flash_finish(o_ref, l_ref, acc_ref, B_W)


def _fox_prompt(q, k, v, lf, batch, seq, blk=256):
    assert seq % blk == 0
    nq = seq // blk
    est = (2 * 2 * seq * B_W * 4 + 3 * seq * LANES * 4 + seq * LANES * 4 + 24 * blk * blk * 4
           + 3 * B_HEADS * blk * LANES * 4)
    return pl.pallas_call(
        functools.partial(_fox_prompt_body, blk=blk),
        out_shape=jax.ShapeDtypeStruct(q.shape, F32),
        grid=(batch, nq),
        in_specs=[pl.BlockSpec((blk, B_W), lambda b, i: (b * nq + i, 0)),
                  pl.BlockSpec((seq, B_W), lambda b, i: (b, 0)),
                  pl.BlockSpec((seq, B_W), lambda b, i: (b, 0)),
                  pl.BlockSpec((seq, LANES), lambda b, i: (b, 0))],
        out_specs=pl.BlockSpec((blk, B_W), lambda b, i: (b * nq + i, 0)),
        scratch_shapes=[pltpu.VMEM((seq, LANES), F32), pltpu.VMEM((nq, LANES, blk), F32)]
        + _flash_scratch(B_HEADS, B_W, blk),
        compiler_params=pltpu.CompilerParams(
            dimension_semantics=("arbitrary", "arbitrary"), vmem_limit_bytes=_vmem_limit(est)),
        name="fox_prompt",
    )(q, k, v, lf)


def _hgrn_body(q_ref, k_ref, lf_ref, v_ref, gs_ref, gn_ref, st0_ref, o_ref, st_ref, m_scr, *, sub):
    seq = q_ref.shape[0]
    r_i = lax.broadcasted_iota(jnp.int32, (sub, sub), 0)
    c_i = lax.broadcasted_iota(jnp.int32, (sub, sub), 1)
    tri = (c_i <= r_i).astype(F32)
    e_i = lax.broadcasted_iota(jnp.int32, (C_W, C_W), 0) // HEAD_DIM
    d_i = lax.broadcasted_iota(jnp.int32, (C_W, C_W), 1) // HEAD_DIM
    same_head = e_i == d_i
    ones_bd = same_head.astype(BF16)
    row_id = lax.broadcasted_iota(jnp.int32, (sub, C_W), 0)

    st_ref[...] = st0_ref[...]

    def chunk(c, _):
        rows = pl.ds(pl.multiple_of(c * sub, sub), sub)
        lf, q, k, v = lf_ref[rows, :], q_ref[rows, :], k_ref[rows, :], v_ref[rows, :]
        cum = _dot(tri, lf, precision=HIGHEST)
        last = cum[sub - 1:sub, :]
        st = st_ref[...]
        o = _dot_nt((q * jnp.exp(cum)).astype(BF16), st.astype(BF16))
        for s in range(sub):
            dec = jnp.exp(jnp.where(row_id >= s, cum - cum[s:s + 1, :], NEG_BIG))
            m_scr[s * sub:(s + 1) * sub, :] = q * k[s:s + 1, :] * dec
        w = _dot(m_scr[...].astype(BF16), ones_bd)
        for s in range(sub):
            o = o + w[s * sub:(s + 1) * sub, :] * v[s:s + 1, :]
        o_ref[rows, :] = o
        kd = k * jnp.exp(last - cum)
        upd = _dot_tn(v.astype(BF16), kd.astype(BF16))
        st_ref[...] = st * jnp.exp(last) + jnp.where(same_head, upd, 0.0)
        return 0

    lax.fori_loop(0, seq // sub, chunk, 0)

    tile = min(seq, 256)
    gn = gn_ref[...]
    for t0 in range(0, seq, tile):
        o = o_ref[t0:t0 + tile, :]
        sq = o * o
        hi = sq.astype(BF16)
        lo = (sq - hi.astype(F32)).astype(BF16)
        ms = (_dot(hi, ones_bd) + _dot(lo, ones_bd)) * (1.0 / HEAD_DIM)
        o_ref[t0:t0 + tile, :] = o * lax.rsqrt(ms + EPS) * gn * gs_ref[t0:t0 + tile, :]


def _hgrn(q, k, lf, v, gs, gn, st0, batch, seq):
    sub = math.gcd(seq, 16)
    row = pl.BlockSpec((seq, C_W), lambda b: (b, 0))
    st_spec = pl.BlockSpec((None, C_W, C_W), lambda b: (b, 0, 0))
    est = 2 * 6 * seq * C_W * 4 + 8 * C_W * C_W * 4 + sub * sub * C_W * 4
    return pl.pallas_call(
        functools.partial(_hgrn_body, sub=sub),
        out_shape=[jax.ShapeDtypeStruct((batch * seq, C_W), F32),
                   jax.ShapeDtypeStruct((batch, C_W, C_W), F32)],
        grid=(batch,),
        in_specs=[row, row, row, row, row, pl.BlockSpec((1, C_W), lambda b: (0, 0)), st_spec],
        out_specs=[row, st_spec],
        scratch_shapes=[pltpu.VMEM((sub * sub, C_W), F32)],
        compiler_params=pltpu.CompilerParams(
            dimension_semantics=("arbitrary",), vmem_limit_bytes=_vmem_limit(est)),
        name="hgrn",
    )(q, k, lf, v, gs, gn, st0)


def _block_diag_q(q, heads, scale):
    t, w = q.shape
    rep = jnp.concatenate([q] * heads, axis=0)
    r = lax.broadcasted_iota(jnp.int32, (heads * t, w), 0) // t
    c = lax.broadcasted_iota(jnp.int32, (heads * t, w), 1) // HEAD_DIM
    return jnp.where(r == c, rep * scale, 0.0)


def _collapse_heads(o, heads, t):
    r = lax.broadcasted_iota(jnp.int32, o.shape, 0) // t
    c = lax.broadcasted_iota(jnp.int32, o.shape, 1) // HEAD_DIM
    o = jnp.where(r == c, o, 0.0)
    out = o[0:t, :]
    for h in range(1, heads):
        out = out + o[h * t:(h + 1) * t, :]
    return out


def _rows_per_head(x, heads, t):
    return jnp.concatenate([jnp.broadcast_to(x[h:h + 1, :], (t, x.shape[1])) for h in range(heads)], axis=0)


def _decode_softmax(s_own, s_ref, p_ref, l_ref, n_pages, page_fn):
    mx = jnp.max(s_own, axis=1, keepdims=True)
    for pg in range(n_pages):
        sm = page_fn(pg, s_ref[pg])
        s_ref[pg] = sm
        mx = jnp.maximum(mx, jnp.max(sm, axis=1, keepdims=True))
    p_own = jnp.exp(s_own - mx)
    l = jnp.sum(p_own, axis=1, keepdims=True)
    for pg in range(n_pages):
        p = jnp.exp(s_ref[pg] - mx)
        l = l + jnp.sum(p, axis=1, keepdims=True)
        p_ref[pg] = p.astype(BF16)
    l_ref[...] = l
    return p_own


def _decode_values(j, pp, p_ref, v_refs, acc_ref, width):
    acc = acc_ref[...]
    for jj in range(pp):
        vt = v_refs[jj][...].reshape(width, -1).astype(BF16)
        acc = acc + _dot_nt(p_ref[j * pp + jj], vt)
    acc_ref[...] = acc


def _moba_decode_body(pt_ref, q_ref, kn_ref, vn_ref, *rest, pp, n_steps, t_new):
    k_refs, v_refs = rest[:pp], rest[pp:2 * pp]
    o_ref, qbd_ref, s_ref, p_ref, kmean_ref, acc_ref, l_ref = rest[2 * pp:]
    ph, j = pl.program_id(1), pl.program_id(2)
    page = k_refs[0].shape[-1]
    rows = A_HEADS * t_new
    n_pages = n_steps * pp
    per_blk = MOBA_BLOCK // page
    n_blk = n_pages // per_blk
    scale = HEAD_DIM ** -0.5

    @pl.when((ph == 0) & (j == 0))
    def _():
        qbd_ref[...] = _block_diag_q(q_ref[...], A_HEADS, 1.0)
        kmean_ref[...] = jnp.zeros(kmean_ref.shape, F32)

    @pl.when(ph == 0)
    def _():
        qb = (qbd_ref[...] * scale).astype(BF16)
        lane = lax.broadcasted_iota(jnp.int32, (A_W, LANES), 1)
        kmean = kmean_ref[...]
        for g in range(pp // per_blk):
            ksum = jnp.zeros((A_W, page), F32)
            for u in range(per_blk):
                jj = g * per_blk + u
                kt = k_refs[jj][...].reshape(A_W, page)
                s_ref[j * pp + jj] = _dot(qb, kt.astype(BF16))
                ksum = ksum + kt
            kcol = jnp.sum(ksum, axis=1, keepdims=True) * (1.0 / MOBA_BLOCK)
            kmean = kmean + jnp.where(lane == j * (pp // per_blk) + g, kcol, 0.0)
        kmean_ref[...] = kmean

    @pl.when((ph == 0) & (j == n_steps - 1))
    def _():
        qbd = qbd_ref[...]
        gate = _dot(qbd, kmean_ref[...], precision=HIGHEST)
        lane_blk = lax.broadcasted_iota(jnp.int32, (rows, LANES), 1)
        sel = _topk_select(gate, lane_blk < n_blk, lane_blk, LANES, MOBA_TOPK)
        qb = (qbd * scale).astype(BF16)
        s_own = _dot_nt(qb, kn_ref[...].astype(BF16))
        qi = lax.broadcasted_iota(jnp.int32, (rows, t_new), 0) % t_new
        kj = lax.broadcasted_iota(jnp.int32, (rows, t_new), 1)
        s_own = jnp.where(kj <= qi, s_own, NEG_BIG)

        def page_fn(pg, s):
            n = pg // per_blk
            return jnp.where(sel[:, n:n + 1] > 0.0, s, NEG_BIG)

        p_own = _decode_softmax(s_own, s_ref, p_ref, l_ref, n_pages, page_fn)
        acc_ref[...] = _dot(p_own.astype(BF16), vn_ref[...].astype(BF16))

    @pl.when(ph == 1)
    def _():
        _decode_values(j, pp, p_ref, v_refs, acc_ref, A_W)

    @pl.when((ph == 1) & (j == n_steps - 1))
    def _():
        o_ref[...] = _collapse_heads(acc_ref[...] / l_ref[...], A_HEADS, t_new)


def _page_specs(block, index_fn, pp, n_steps, first_phase):
    specs = []
    for jj in range(pp):
        def imap(b, ph, j, pt, jj=jj):
            if first_phase:
                step = jnp.where(ph == 0, j, n_steps - 1)
            else:
                step = jnp.where(ph == 0, 0, j)
            return index_fn(pt[b, step * pp + jj])
        specs.append(pl.BlockSpec(block, imap))
    return specs


def _moba_decode(page_table, q, k_new, v_new, cache_kt, cache_vt, layer, n_seq, t_new, pp=16):
    n_pages = page_table.shape[1]
    page = cache_kt.shape[-1]
    assert MOBA_BLOCK % page == 0 and n_pages % pp == 0 and pp % (MOBA_BLOCK // page) == 0
    assert n_pages * page // MOBA_BLOCK <= LANES
    n_steps = n_pages // pp
    rows = A_HEADS * t_new
    new = pl.BlockSpec((t_new, A_W), lambda b, ph, j, pt: (b, 0))
    blk = (None, None, A_HEADS, HEAD_DIM, page)
    idx = lambda p: (layer, p, 0, 0, 0)
    est = 2 * 2 * pp * page * A_W * 4 + rows * n_pages * page * 6 + 4 * LANES * A_W * 4
    grid_spec = pltpu.PrefetchScalarGridSpec(
        num_scalar_prefetch=1,
        grid=(n_seq, 2, n_steps),
        in_specs=[new, new, new]
        + _page_specs(blk, idx, pp, n_steps, True) + _page_specs(blk, idx, pp, n_steps, False),
        out_specs=new,
        scratch_shapes=[pltpu.VMEM((rows, A_W), F32), pltpu.VMEM((n_pages, rows, page), F32),
                        pltpu.VMEM((n_pages, rows, page), BF16), pltpu.VMEM((A_W, LANES), F32),
                        pltpu.VMEM((rows, A_W), F32), pltpu.VMEM((rows, 1), F32)],
    )
    return pl.pallas_call(
        functools.partial(_moba_decode_body, pp=pp, n_steps=n_steps, t_new=t_new),
        out_shape=jax.ShapeDtypeStruct((n_seq * t_new, A_W), F32),
        grid_spec=grid_spec,
        compiler_params=pltpu.CompilerParams(
            dimension_semantics=("arbitrary", "arbitrary", "arbitrary"), vmem_limit_bytes=_vmem_limit(est)),
        name="moba_decode",
    )(page_table, q, k_new, v_new, *([cache_kt] * pp), *([cache_vt] * pp))


def _fox_decode_body(pt_ref, q_ref, kn_ref, vn_ref, lfn_ref, *rest, pp, n_steps, t_new):
    k_refs, v_refs, lf_refs = rest[:pp], rest[pp:2 * pp], rest[2 * pp:3 * pp]
    o_ref, qbd_ref, s_ref, p_ref, carry_ref, acc_ref, l_ref = rest[3 * pp:]
    ph, j = pl.program_id(1), pl.program_id(2)
    page = k_refs[0].shape[-1]
    rows = B_HEADS * t_new
    n_pages = n_steps * pp
    scale = HEAD_DIM ** -0.5

    @pl.when((ph == 0) & (j == 0))
    def _():
        qbd_ref[...] = _block_diag_q(q_ref[...], B_HEADS, scale).astype(BF16)
        carry_ref[...] = jnp.zeros(carry_ref.shape, F32)

    @pl.when(ph == 0)
    def _():
        qb = qbd_ref[...]
        r = lax.broadcasted_iota(jnp.int32, (page, page), 0)
        c = lax.broadcasted_iota(jnp.int32, (page, page), 1)
        upper = (r <= c).astype(F32)
        carry = carry_ref[...]
        for jj in range(pp):
            lf = lf_refs[jj][...]
            cpage = _dot(lf, upper, precision=HIGHEST) + carry
            carry = carry + jnp.sum(lf, axis=1, keepdims=True)
            kt = k_refs[jj][...].reshape(B_W, page).astype(BF16)
            s_ref[j * pp + jj] = _dot(qb, kt) - _rows_per_head(cpage, B_HEADS, t_new)
        carry_ref[...] = carry

    @pl.when((ph == 0) & (j == n_steps - 1))
    def _():
        qb = qbd_ref[...]
        r8 = lax.broadcasted_iota(jnp.int32, (t_new, t_new), 0)
        c8 = lax.broadcasted_iota(jnp.int32, (t_new, t_new), 1)
        cum_new = _dot(lfn_ref[...], (r8 <= c8).astype(F32), precision=HIGHEST)
        cn = _rows_per_head(cum_new, B_HEADS, t_new)
        qi = lax.broadcasted_iota(jnp.int32, (rows, t_new), 0) % t_new
        kj = lax.broadcasted_iota(jnp.int32, (rows, t_new), 1)
        cq_new = jnp.sum(jnp.where(kj == qi, cn, 0.0), axis=1, keepdims=True)
        ct = _rows_per_head(carry_ref[...], B_HEADS, t_new) + cq_new
        s_own = _dot_nt(qb, kn_ref[...].astype(BF16)) + (cq_new - cn)
        s_own = jnp.where(kj <= qi, s_own, NEG_BIG)
        p_own = _decode_softmax(s_own, s_ref, p_ref, l_ref, n_pages, lambda pg, s: s + ct)
        acc_ref[...] = _dot(p_own.astype(BF16), vn_ref[...].astype(BF16))

    @pl.when(ph == 1)
    def _():
        _decode_values(j, pp, p_ref, v_refs, acc_ref, B_W)

    @pl.when((ph == 1) & (j == n_steps - 1))
    def _():
        o_ref[...] = _collapse_heads(acc_ref[...] / l_ref[...], B_HEADS, t_new)


def _fox_decode(page_table, q, k_new, v_new, lf_new_t, cache_kt, cache_vt, cache_lf_t, layer, n_seq, t_new, pp=16):
    n_pages = page_table.shape[1]
    page = cache_kt.shape[-1]
    assert n_pages % pp == 0
    n_steps = n_pages // pp
    rows = B_HEADS * t_new
    new = pl.BlockSpec((t_new, B_W), lambda b, ph, j, pt: (b, 0))
    blk = (None, None, B_HEADS, HEAD_DIM, page)
    idx = lambda p: (layer, p, 0, 0, 0)
    est = 2 * 2 * pp * page * (B_W + 8) * 4 + rows * n_pages * page * 6
    grid_spec = pltpu.PrefetchScalarGridSpec(
        num_scalar_prefetch=1,
        grid=(n_seq, 2, n_steps),
        in_specs=[new, new, new, pl.BlockSpec((None, B_HEADS, t_new), lambda b, ph, j, pt: (b, 0, 0))]
        + _page_specs(blk, idx, pp, n_steps, True) + _page_specs(blk, idx, pp, n_steps, False)
        + _page_specs((None, None, B_HEADS, page), lambda p: (layer, p, 0, 0), pp, n_steps, True),
        out_specs=new,
        scratch_shapes=[pltpu.VMEM((rows, B_W), BF16), pltpu.VMEM((n_pages, rows, page), F32),
                        pltpu.VMEM((n_pages, rows, page), BF16), pltpu.VMEM((B_HEADS, 1), F32),
                        pltpu.VMEM((rows, B_W), F32), pltpu.VMEM((rows, 1), F32)],
    )
    return pl.pallas_call(
        functools.partial(_fox_decode_body, pp=pp, n_steps=n_steps, t_new=t_new),
        out_shape=jax.ShapeDtypeStruct((n_seq * t_new, B_W), F32),
        grid_spec=grid_spec,
        compiler_params=pltpu.CompilerParams(
            dimension_semantics=("arbitrary", "arbitrary", "arbitrary"), vmem_limit_bytes=_vmem_limit(est)),
        name="fox_decode",
    )(page_table, q, k_new, v_new, lf_new_t, *([cache_kt] * pp), *([cache_vt] * pp), *([cache_lf_t] * pp))


def _pad_w_in(w):
    sizes = (A_W,) * 3 + (B_W,) * 3 + (B_HEADS,) + (C_W,) * 4
    offs = np.cumsum((0,) + sizes)
    segs = [SEG_QA, SEG_KA, SEG_VA, SEG_QF, SEG_KF, SEG_VF, SEG_FF, SEG_QC, SEG_FC, SEG_IC, SEG_GC]
    out = jnp.zeros((w.shape[0], D_IN_PAD), BF16)
    for s, o, n in zip(segs, offs[:-1], sizes):
        out = out.at[:, s:s + n].set(w[:, o:o + n].astype(BF16))
    return out


def _rope_tables(pos0, t, reps):
    half = HEAD_DIM // 2
    inv = ROPE_THETA ** (-jnp.arange(half, dtype=F32) * 2.0 / HEAD_DIM)
    ang = (pos0 + jnp.arange(t, dtype=F32))[:, None] * inv[None, :]
    cos, sin, zero = jnp.cos(ang), jnp.sin(ang), jnp.zeros((t, half), F32)
    heads_per_vreg = LANES // HEAD_DIM
    tabs = (jnp.concatenate([cos, cos] * heads_per_vreg, axis=1),
            jnp.concatenate([-sin, zero] * heads_per_vreg, axis=1),
            jnp.concatenate([zero, sin] * heads_per_vreg, axis=1))
    return tuple(jnp.tile(tb, (reps, 1)) for tb in tabs)


def _pad_lanes(a, width):
    return jnp.pad(a, ((0, 0), (0, width - a.shape[1])))


def _state_to_bd(s):
    b, h, dk, dv = s.shape
    st = jnp.swapaxes(s, 2, 3)
    eye = jnp.eye(h, dtype=s.dtype)
    return (st[:, :, :, None, :] * eye[None, :, None, :, None]).reshape(b, h * dv, h * dk)


def _bd_to_state(st, h):
    b = st.shape[0]
    blocks = st.reshape(b, h, HEAD_DIM, h, HEAD_DIM)
    diag = jnp.stack([blocks[:, i, :, i, :] for i in range(h)], axis=1)
    return jnp.swapaxes(diag, 2, 3)


def kernel(x_prompt, x_sample, cache_moba_k, cache_moba_v, cache_fox_k, cache_fox_v, cache_fox_logf, state_hgrn, page_table, norm_ffn1_g, ffn1_w_gate, ffn1_w_up, ffn1_w_down, norm_mix_g, w_in, fox_f_bias, hgrn_lb_logits, hgrn_norm_g, w_out, norm_ffn2_g, ffn2_w_gate, ffn2_w_up, ffn2_w_down, norm_final_g):
    depth = w_in.shape[0]
    batch, seq, d = x_prompt.shape
    n_seq, t_new, _ = x_sample.shape
    page = cache_moba_k.shape[2]
    past_len = page_table.shape[1] * page

    bf = lambda a: a.astype(BF16)
    row = lambda a: a.reshape(1, -1)
    w1g, w1u, w1d = bf(ffn1_w_gate), bf(ffn1_w_up), bf(ffn1_w_down)
    w2g, w2u, w2d = bf(ffn2_w_gate), bf(ffn2_w_up), bf(ffn2_w_down)
    wo = bf(w_out)
    w_in_pad = [_pad_w_in(w_in[l]) for l in range(depth)]
    fb = [_pad_lanes(row(fox_f_bias[l]), LANES) for l in range(depth)]
    lbl = _pad_lanes(hgrn_lb_logits.astype(F32), SEG_W)
    gn = jnp.tile(hgrn_norm_g, (1, C_HEADS))
    to_pages = lambda c: jnp.transpose(c, (0, 1, 3, 4, 2))
    ckt, cvt, fkt, fvt = (to_pages(c) for c in (cache_moba_k, cache_moba_v, cache_fox_k, cache_fox_v))
    flt = jnp.swapaxes(cache_fox_logf, 2, 3)

    def trunk(x, sample):
        if sample:
            nb, t = n_seq, t_new
            tabs = _rope_tables(float(past_len), t, nb)
        else:
            nb, t = batch, seq
            tabs = _rope_tables(0.0, t, 1)
        new = []
        for l in range(depth):
            if l == 0:
                x = _ffn(x, row(norm_ffn1_g[l]), w1g[l], w1u[l], w1d[l])
            (qa, ka, va, qf, kf, vf, lf, qc, kc, lc, ic, gc) = _inproj(
                x, row(norm_mix_g[l]), w_in_pad[l], tabs, fb[l], lbl, l)
            if sample:
                oa = _moba_decode(page_table, qa, ka, va, ckt, cvt, l, nb, t)
                lf_t = jnp.swapaxes(lf[:, :B_HEADS].reshape(nb, t, B_HEADS), 1, 2)
                of = _fox_decode(page_table, qf, kf, vf, lf_t, fkt, fvt, flt, l, nb, t)
                st0 = _state_to_bd(state_hgrn[l].astype(F32))
            else:
                oa = _moba_prompt(qa, ka, va, nb, t)
                of = _fox_prompt(qf, kf, vf, lf, nb, t)
                st0 = jnp.zeros((nb, C_W, C_W), F32)
            oc, st = _hgrn(qc, kc, lc, ic, gc, row(gn[l]), st0, nb, t)
            mix = (oa, of, oc, wo[l, :A_W], wo[l, A_W:A_W + B_W], wo[l, A_W + B_W:])
            last = l == depth - 1
            x = _ffn(x, row(norm_ffn2_g[l]), w2g[l], w2u[l], w2d[l], mix=mix,
                     g_final=row(norm_final_g) if last else None)
            if not last:
                x = _ffn(x, row(norm_ffn1_g[l + 1]), w1g[l + 1], w1u[l + 1], w1d[l + 1])
            new.append((ka.reshape(nb, t, A_HEADS, HEAD_DIM), va.reshape(nb, t, A_HEADS, HEAD_DIM),
                        kf.reshape(nb, t, B_HEADS, HEAD_DIM), vf.reshape(nb, t, B_HEADS, HEAD_DIM),
                        lf[:, :B_HEADS].reshape(nb, t, B_HEADS), _bd_to_state(st, C_HEADS)))
        return x.reshape(nb, t, d), [jnp.stack(z) for z in zip(*new)]

    y_p, (mk_p, mv_p, fk_p, fv_p, fl_p, hs_p) = trunk(x_prompt.reshape(batch * seq, d), False)
    y_s, (mk_s, mv_s, fk_s, fv_s, fl_s, hs_s) = trunk(x_sample.reshape(n_seq * t_new, d), True)
    return (y_p, y_s, mk_p, mv_p, fk_p, fv_p, fl_p, hs_p, mk_s, mv_s, fk_s, fv_s, fl_s, hs_s)
```

```python
import functools
import math

import numpy as np
import jax
import jax.numpy as jnp
from jax import lax
from jax.experimental import pallas as pl
from jax.experimental.pallas import tpu as pltpu

F32 = jnp.float32
BF16 = jnp.bfloat16
HIGHEST = lax.Precision.HIGHEST

HEAD_DIM = 64
A_HEADS = 6
B_HEADS = 5
C_HEADS = 5
A_W = A_HEADS * HEAD_DIM
B_W = B_HEADS * HEAD_DIM
C_W = C_HEADS * HEAD_DIM
D_FF_CHUNK = 256
MOBA_BLOCK = 256
MOBA_TOPK = 3
ROPE_THETA = 10000.0
EPS = 1e-6
NEG_BIG = -1e30

LANES = 128
SUBLANES = 8
MXU_W = 256
VMEM_PHYSICAL = 64 * 1024 * 1024
VMEM_CAP = VMEM_PHYSICAL - 8 * 1024 * 1024

SEG_W = 384
SEG_QA, SEG_KA, SEG_VA, SEG_QF, SEG_KF, SEG_VF, SEG_QC, SEG_FC, SEG_IC, SEG_GC = (
    i * SEG_W for i in range(10))
SEG_FF = 10 * SEG_W
D_IN_PAD = SEG_FF + LANES


def _vmem_limit(estimate_bytes):
    return int(min(max(2 * estimate_bytes, 32 * 1024 * 1024), VMEM_CAP))


def _dot(a, b, precision=None):
    return jnp.dot(a, b, preferred_element_type=F32, precision=precision)


def _dot_nt(a, b, precision=None):
    return lax.dot_general(a, b, (((1,), (1,)), ((), ())),
                           preferred_element_type=F32, precision=precision)


def _dot_tn(a, b, precision=None):
    return lax.dot_general(a, b, (((0,), (0,)), ((), ())),
                           preferred_element_type=F32, precision=precision)


def _rms(x, g):
    ms = jnp.mean(x * x, axis=-1, keepdims=True)
    return x * lax.rsqrt(ms + EPS) * g


def _sigmoid(x):
    return 1.0 / (1.0 + jnp.exp(-x))


def _silu(x):
    return x * _sigmoid(x)


def _log_sigmoid(x):
    return jnp.minimum(x, 0.0) - jnp.log1p(jnp.exp(-jnp.abs(x)))


def _ffn_body(*refs, has_mix, has_final, d_ff):
    it = iter(refs)
    x_ref = next(it)
    if has_mix:
        oa_ref, of_ref, oc_ref, woa_ref, wof_ref, woc_ref = (next(it) for _ in range(6))
    g_ref, wg_ref, wu_ref, wd_ref = (next(it) for _ in range(4))
    gf_ref = next(it) if has_final else None
    out_ref = next(it)

    x = x_ref[...]
    if has_mix:
        x = (x + _dot(oa_ref[...].astype(BF16), woa_ref[...])
             + _dot(of_ref[...].astype(BF16), wof_ref[...])
             + _dot(oc_ref[...].astype(BF16), woc_ref[...]))
    h = _rms(x, g_ref[...]).astype(BF16)
    acc = jnp.zeros(x.shape, F32)
    for c0 in range(0, d_ff, D_FF_CHUNK):
        gate = _dot(h, wg_ref[:, c0:c0 + D_FF_CHUNK])
        up = _dot(h, wu_ref[:, c0:c0 + D_FF_CHUNK])
        act = (_silu(gate) * up).astype(BF16)
        acc = acc + _dot(act, wd_ref[c0:c0 + D_FF_CHUNK, :])
    y = x + 0.5 * acc
    if has_final:
        y = _rms(y, gf_ref[...])
    out_ref[...] = y


def _ffn(x, g, wg, wu, wd, mix=None, g_final=None, tm=512):
    n, d = x.shape
    d_ff = wg.shape[1]
    tm = min(tm, n)
    assert n % tm == 0 and d_ff % D_FF_CHUNK == 0
    row = lambda w: pl.BlockSpec((tm, w), lambda i: (i, 0))
    full = lambda a: pl.BlockSpec(a.shape, lambda i: (0,) * a.ndim, pipeline_mode=pl.Buffered(1))
    args, specs = [x], [row(d)]
    if mix is not None:
        oa, of, oc, woa, wof, woc = mix
        args += [oa, of, oc, woa, wof, woc]
        specs += [row(oa.shape[1]), row(of.shape[1]), row(oc.shape[1]), full(woa), full(wof), full(woc)]
    args += [g, wg, wu, wd]
    specs += [full(g), full(wg), full(wu), full(wd)]
    if g_final is not None:
        args.append(g_final)
        specs.append(full(g_final))
    weights = 2 * 3 * d * d_ff + (2 * d * d if mix is not None else 0)
    tiles = 2 * 2 * tm * d * 4 * (2 if mix is not None else 1) + 6 * tm * d * 4
    return pl.pallas_call(
        functools.partial(_ffn_body, has_mix=mix is not None, has_final=g_final is not None, d_ff=d_ff),
        out_shape=jax.ShapeDtypeStruct((n, d), F32),
        grid=(n // tm,),
        in_specs=specs,
        out_specs=row(d),
        compiler_params=pltpu.CompilerParams(
            dimension_semantics=("arbitrary",), vmem_limit_bytes=_vmem_limit(weights + tiles)),
        name="ffn_mix" if mix is not None else "ffn",
    )(*args)


def _inproj_body(x_ref, g_ref, w_ref, cos_ref, slo_ref, shi_ref, fb_ref, lbl_ref, *out_refs, layer, transposed):
    if transposed:
        (qa_ref, ka_ref, kat_ref, vat_ref, qf_ref, kf_ref, kft_ref, vft_ref, lf_ref,
         qc_ref, kc_ref, lc_ref, ic_ref, gc_ref, stage_ref) = out_refs
    else:
        (qa_ref, ka_ref, va_ref, qf_ref, kf_ref, vf_ref, lf_ref,
         qc_ref, kc_ref, lc_ref, ic_ref, gc_ref) = out_refs
    h = _rms(x_ref[...], g_ref[...]).astype(BF16)

    def proj(c0, width=SEG_W):
        return _dot(h, w_ref[:, c0:c0 + width])

    cos, slo, shi = cos_ref[...], slo_ref[...], shi_ref[...]

    def rope(p):
        parts = []
        for c in range(SEG_W // LANES):
            pc = p[:, c * LANES:(c + 1) * LANES]
            parts.append(pc * cos + pltpu.roll(pc, LANES - HEAD_DIM // 2, 1) * slo
                         + pltpu.roll(pc, HEAD_DIM // 2, 1) * shi)
        return jnp.concatenate(parts, axis=1)

    qa_ref[...] = rope(proj(SEG_QA))
    ka = rope(proj(SEG_KA))
    ka_ref[...] = ka
    if transposed:
        def transposed_tile(p):
            stage_ref[...] = p
            return stage_ref[...].T

        kat_ref[...] = ka.T
        vat_ref[...] = transposed_tile(proj(SEG_VA))
        qf_ref[...] = proj(SEG_QF)
        kf_ref[...] = proj(SEG_KF)
        kft_ref[...] = kf_ref[...].T[:B_W, :]
        vft_ref[...] = transposed_tile(proj(SEG_VF))[:B_W, :]
    else:
        va_ref[...] = proj(SEG_VA)
        qf_ref[...] = proj(SEG_QF)[:, :B_W]
        kf_ref[...] = proj(SEG_KF)[:, :B_W]
        vf_ref[...] = proj(SEG_VF)[:, :B_W]
    lf_ref[...] = _log_sigmoid(proj(SEG_FF, LANES) + fb_ref[...])

    lbl = lbl_ref[...]
    e = jnp.exp(lbl - jnp.max(lbl, axis=0, keepdims=True))
    psm = e / jnp.sum(e, axis=0, keepdims=True)
    lb = jnp.zeros((1, SEG_W), F32)
    for j in range(1, layer + 1):
        lb = lb + psm[j:j + 1, :]
    fc = proj(SEG_FC)
    lc_ref[...] = jnp.log(lb + (1.0 - lb) * _sigmoid(fc))[:, :C_W]
    kc_ref[...] = ((1.0 - lb) * _sigmoid(-fc))[:, :C_W]
    qc_ref[...] = _silu(proj(SEG_QC))[:, :C_W]
    ic_ref[...] = proj(SEG_IC)[:, :C_W]
    gc_ref[...] = _silu(proj(SEG_GC))[:, :C_W]


def _inproj(x, g, w_pad, rope_tabs, fb, lbl, layer, seq_t=None, tm=256):
    n, d = x.shape
    tm = min(tm, n)
    assert n % tm == 0 and rope_tabs[0].shape[0] % tm == 0
    row = lambda w: pl.BlockSpec((tm, w), lambda i: (i, 0))
    full = lambda a: pl.BlockSpec(a.shape, lambda i: (0,) * a.ndim, pipeline_mode=pl.Buffered(1))
    n_tab = rope_tabs[0].shape[0] // tm
    tab = pl.BlockSpec((tm, LANES), lambda i: (i % n_tab, 0))
    if seq_t is None:
        widths = [A_W, A_W, A_W, B_W, B_W, B_W, LANES, C_W, C_W, C_W, C_W, C_W]
        shapes = [(n, w) for w in widths]
        out_specs = [row(w) for w in widths]
    else:
        assert seq_t % tm == 0
        per_seq = seq_t // tm
        col = lambda w: pl.BlockSpec((None, w, tm), lambda i: (i // per_seq, 0, i % per_seq))
        kinds = [(row, A_W), (row, A_W), (col, A_W), (col, A_W), (row, SEG_W), (row, SEG_W), (col, B_W),
                 (col, B_W), (row, LANES), (row, C_W), (row, C_W), (row, C_W), (row, C_W), (row, C_W)]
        widths = [w for _, w in kinds]
        shapes = [(n, w) if f is row else (n // seq_t, w, seq_t) for f, w in kinds]
        out_specs = [f(w) for f, w in kinds]
    est = 2 * d * D_IN_PAD + 2 * tm * 4 * (d + sum(widths) + 3 * LANES) + 8 * tm * SEG_W * 4
    return pl.pallas_call(
        functools.partial(_inproj_body, layer=layer, transposed=seq_t is not None),
        out_shape=[jax.ShapeDtypeStruct(s, F32) for s in shapes],
        grid=(n // tm,),
        in_specs=[row(d), full(g), full(w_pad), tab, tab, tab, full(fb), full(lbl)],
        out_specs=out_specs,
        scratch_shapes=[pltpu.VMEM((tm, SEG_W), F32)] if seq_t is not None else [],
        compiler_params=pltpu.CompilerParams(
            dimension_semantics=("arbitrary",), vmem_limit_bytes=_vmem_limit(est)),
        name="inproj",
    )(x, g, w_pad, *rope_tabs, fb, lbl)


def _topk_mask(gate, allowed, topk, axis):
    n = gate.shape[axis]
    pos = lax.broadcasted_iota(jnp.int32, gate.shape, axis).astype(F32)
    g = jnp.where(allowed, gate, NEG_BIG)
    sel = jnp.zeros(gate.shape, F32)
    for _ in range(topk):
        m = jnp.max(g, axis=axis, keepdims=True)
        idx = jnp.min(jnp.where(g == m, pos, float(n)), axis=axis, keepdims=True)
        hit = pos == idx
        sel = jnp.where(hit & allowed, 1.0, sel)
        g = jnp.where(hit, -jnp.inf, g)
    return sel


def _own_lanes(shape, h):
    return lax.broadcasted_iota(jnp.int32, shape, 1) // HEAD_DIM == h % 2


def _flash_step(heads, score_fn, value_fn, s_ref, p_ref, m_ref, l_ref, acc_ref):
    for h in range(heads):
        s_ref[h] = score_fn(h)
    alphas = []
    for h in range(heads):
        m_old = m_ref[h]
        m_new = jnp.maximum(m_old, jnp.max(s_ref[h], axis=0, keepdims=True))
        alpha = jnp.exp(m_old - m_new)
        p = jnp.exp(s_ref[h] - m_new)
        l_ref[h] = alpha * l_ref[h] + jnp.sum(p, axis=0, keepdims=True)
        m_ref[h] = m_new
        p_ref[h] = p.astype(BF16)
        alphas.append(alpha)
    for h in range(heads):
        acc_ref[h] = acc_ref[h] * alphas[h] + _dot(value_fn(h), p_ref[h])


def _flash_init(m_ref, l_ref, acc_ref):
    m_ref[...] = jnp.full(m_ref.shape, -jnp.inf, F32)
    l_ref[...] = jnp.zeros(l_ref.shape, F32)
    acc_ref[...] = jnp.zeros(acc_ref.shape, F32)


def _flash_finish(o_ref, l_ref, acc_ref, heads):
    blk = acc_ref.shape[2]
    parts = [acc_ref[h] / l_ref[h] for h in range(heads)]
    parts += [jnp.zeros((HEAD_DIM, blk), F32)] * (SEG_W // HEAD_DIM - heads)
    o_ref[...] = jnp.concatenate(parts, axis=0).T[:, :o_ref.shape[1]]


def _flash_scratch(heads, blk):
    return [pltpu.VMEM((heads, blk, blk), F32), pltpu.VMEM((heads, blk, blk), BF16),
            pltpu.VMEM((heads, 1, blk), F32), pltpu.VMEM((heads, 1, blk), F32),
            pltpu.VMEM((heads, HEAD_DIM, blk), F32)]


def _moba_prompt_body(q_ref, k_ref, vt_ref, o_ref, kmean_ref, k16_ref, vt16_ref, qm_ref, sel_ref,
                      s_ref, p_ref, m_ref, l_ref, acc_ref, *, blk):
    i = pl.program_id(1)
    nb = k_ref.shape[0] // blk
    scale = HEAD_DIM ** -0.5

    @pl.when(i == 0)
    def _():
        for n in range(nb):
            kb = k_ref[n * blk:(n + 1) * blk, :]
            kmean_ref[n:n + 1, :] = jnp.sum(kb, axis=0, keepdims=True) * (1.0 / blk)
            k16_ref[n * blk:(n + 1) * blk, :] = kb.astype(BF16)
            vt16_ref[n] = vt_ref[:, n * blk:(n + 1) * blk].astype(BF16)

    past = lax.broadcasted_iota(jnp.int32, (nb, blk), 0) < i
    for h in range(A_HEADS):
        c0 = (h // 2) * LANES
        qwin = q_ref[:, c0:c0 + LANES]
        kmean_h = jnp.where(_own_lanes((nb, LANES), h), kmean_ref[:, c0:c0 + LANES], 0.0)
        gate = _dot_nt(kmean_h, qwin, precision=HIGHEST)
        sel_ref[h] = _topk_mask(gate, past, MOBA_TOPK, axis=0)
        qm_ref[h] = jnp.where(_own_lanes((blk, LANES), h), qwin * scale, 0.0).astype(BF16)
    _flash_init(m_ref, l_ref, acc_ref)
    causal = (lax.broadcasted_iota(jnp.int32, (blk, blk), 0)
              <= lax.broadcasted_iota(jnp.int32, (blk, blk), 1))

    def step(n, mask_fn):
        rows = pl.ds(pl.multiple_of(n * blk, blk), blk)
        _flash_step(
            A_HEADS,
            lambda h: mask_fn(h, _dot_nt(k16_ref[rows, (h // 2) * LANES:(h // 2 + 1) * LANES], qm_ref[h])),
            lambda h: vt16_ref[n, h * HEAD_DIM:(h + 1) * HEAD_DIM, :],
            s_ref, p_ref, m_ref, l_ref, acc_ref)

    step(i, lambda h, st: jnp.where(causal, st, NEG_BIG))

    def body(n, carry):
        step(n, lambda h, st: jnp.where(sel_ref[h, pl.ds(n, 1), :] > 0.0, st, NEG_BIG))
        return carry

    lax.fori_loop(0, i, body, 0)
    _flash_finish(o_ref, l_ref, acc_ref, A_HEADS)


def _moba_prompt(q, k, vt, batch, seq):
    blk = MOBA_BLOCK
    assert seq % blk == 0
    nq = seq // blk
    est = 2 * 2 * seq * A_W * 4 + 2 * seq * A_W * 2 + 4 * blk * A_W * 4 + 24 * blk * blk * 4
    return pl.pallas_call(
        functools.partial(_moba_prompt_body, blk=blk),
        out_shape=jax.ShapeDtypeStruct(q.shape, F32),
        grid=(batch, nq),
        in_specs=[pl.BlockSpec((blk, A_W), lambda b, i: (b * nq + i, 0)),
                  pl.BlockSpec((seq, A_W), lambda b, i: (b, 0)),
                  pl.BlockSpec((None, A_W, seq), lambda b, i: (b, 0, 0))],
        out_specs=pl.BlockSpec((blk, A_W), lambda b, i: (b * nq + i, 0)),
        scratch_shapes=[pltpu.VMEM((nq, A_W), F32), pltpu.VMEM((seq, A_W), BF16),
                        pltpu.VMEM((nq, A_W, blk), BF16), pltpu.VMEM((A_HEADS, blk, LANES), BF16),
                        pltpu.VMEM((A_HEADS, nq, blk), F32)] + _flash_scratch(A_HEADS, blk),
        compiler_params=pltpu.CompilerParams(
            dimension_semantics=("arbitrary", "arbitrary"), vmem_limit_bytes=_vmem_limit(est)),
        name="moba_prompt",
    )(q, k, vt)


FOX_AUG = 8


def _fox_prompt_body(q_ref, k_ref, vt_ref, lf_ref, o_ref, bq_ref, kaug_ref, vt16_ref, qaug_ref,
                     s_ref, p_ref, m_ref, l_ref, acc_ref, *, blk):
    i = pl.program_id(1)
    nb = k_ref.shape[0] // blk
    scale = HEAD_DIM ** -0.5
    key_i = lax.broadcasted_iota(jnp.int32, (blk, blk), 0)
    qry_i = lax.broadcasted_iota(jnp.int32, (blk, blk), 1)

    @pl.when(i == 0)
    def _():
        tri = (qry_i <= key_i).astype(F32)
        head = lax.broadcasted_iota(jnp.int32, (LANES, LANES), 0)
        lane = lax.broadcasted_iota(jnp.int32, (LANES, LANES), 1)
        place = lambda off: ((lane == head * FOX_AUG + off) & (head < B_HEADS)).astype(BF16)
        lane1 = lax.broadcasted_iota(jnp.int32, (1, LANES), 1)
        used = lane1 < B_HEADS * FOX_AUG
        ones_q = (used & (lane1 % FOX_AUG < 3)).astype(F32)
        ones_k = (used & (lane1 % FOX_AUG >= 3) & (lane1 % FOX_AUG < 6)).astype(F32)
        carry = jnp.zeros((1, LANES), F32)
        for n in range(nb):
            rows = slice(n * blk, (n + 1) * blk)
            c = _dot(tri, lf_ref[rows, :], precision=HIGHEST) + carry
            carry = c[blk - 1:blk, :]
            c1 = c.astype(BF16)
            r1 = c - c1.astype(F32)
            c2 = r1.astype(BF16)
            c3 = (r1 - c2.astype(F32)).astype(BF16)
            pieces = (c1, c2, c3)
            bq = ones_q + sum(_dot(pieces[j], place(3 + j)) for j in range(3))
            ak = ones_k - sum(_dot(pieces[j], place(j)) for j in range(3))
            bq_ref[rows, :] = bq.astype(BF16)
            for w in range(SEG_W // LANES):
                kaug_ref[w, rows, :LANES] = k_ref[rows, w * LANES:(w + 1) * LANES].astype(BF16)
                kaug_ref[w, rows, LANES:] = ak.astype(BF16)
            vt16_ref[n] = vt_ref[:, rows].astype(BF16)

    bq_rows = bq_ref[pl.ds(pl.multiple_of(i * blk, blk), blk), :]
    group = lax.broadcasted_iota(jnp.int32, (blk, LANES), 1) // FOX_AUG
    for h in range(B_HEADS):
        c0 = (h // 2) * LANES
        qaug_ref[h, :, :LANES] = jnp.where(_own_lanes((blk, LANES), h), q_ref[:, c0:c0 + LANES] * scale,
                                           0.0).astype(BF16)
        qaug_ref[h, :, LANES:] = jnp.where(group == h, bq_rows, jnp.zeros_like(bq_rows))
    _flash_init(m_ref, l_ref, acc_ref)
    causal = key_i <= qry_i

    def step(n, diag):
        rows = pl.ds(pl.multiple_of(n * blk, blk), blk)

        def scores(h):
            st = _dot_nt(kaug_ref[h // 2, rows, :], qaug_ref[h])
            return jnp.where(causal, st, NEG_BIG) if diag else st

        _flash_step(B_HEADS, scores, lambda h: vt16_ref[n, h * HEAD_DIM:(h + 1) * HEAD_DIM, :],
                    s_ref, p_ref, m_ref, l_ref, acc_ref)

    step(i, True)

    def body(n, carry):
        step(n, False)
        return carry

    lax.fori_loop(0, i, body, 0)
    _flash_finish(o_ref, l_ref, acc_ref, B_HEADS)


def _fox_prompt(q, k, vt, lf, batch, seq, blk=256):
    assert seq % blk == 0
    nq = seq // blk
    est = (2 * seq * (SEG_W + B_W + LANES) * 4 + seq * (3 * 2 * LANES + LANES + B_W) * 2
           + 4 * blk * SEG_W * 4 + 24 * blk * blk * 4)
    return pl.pallas_call(
        functools.partial(_fox_prompt_body, blk=blk),
        out_shape=jax.ShapeDtypeStruct((batch * seq, B_W), F32),
        grid=(batch, nq),
        in_specs=[pl.BlockSpec((blk, SEG_W), lambda b, i: (b * nq + i, 0)),
                  pl.BlockSpec((seq, SEG_W), lambda b, i: (b, 0)),
                  pl.BlockSpec((None, B_W, seq), lambda b, i: (b, 0, 0)),
                  pl.BlockSpec((seq, LANES), lambda b, i: (b, 0))],
        out_specs=pl.BlockSpec((blk, B_W), lambda b, i: (b * nq + i, 0)),
        scratch_shapes=[pltpu.VMEM((seq, LANES), BF16), pltpu.VMEM((SEG_W // LANES, seq, 2 * LANES), BF16),
                        pltpu.VMEM((nq, B_W, blk), BF16), pltpu.VMEM((B_HEADS, blk, 2 * LANES), BF16)]
        + _flash_scratch(B_HEADS, blk),
        compiler_params=pltpu.CompilerParams(
            dimension_semantics=("arbitrary", "arbitrary"), vmem_limit_bytes=_vmem_limit(est)),
        name="fox_prompt",
    )(q, k, vt, lf)


def _hgrn_body(q_ref, k_ref, lf_ref, v_ref, gs_ref, gn_ref, st0_ref, o_ref, st_ref,
               qd_ref, kd_ref, v16_ref, dl_ref, upd_ref, *, sub, tile):
    seq = q_ref.shape[0]
    n_sub = tile // sub
    t_i = lax.broadcasted_iota(jnp.int32, (tile, tile), 0)
    s_i = lax.broadcasted_iota(jnp.int32, (tile, tile), 1)
    same_sub = t_i // sub == s_i // sub
    tri_blk = (same_sub & (s_i <= t_i)).astype(BF16)
    e_i = lax.broadcasted_iota(jnp.int32, (C_W, C_W), 0) // HEAD_DIM
    d_i = lax.broadcasted_iota(jnp.int32, (C_W, C_W), 1) // HEAD_DIM
    same_head = e_i == d_i
    ones_bd = same_head.astype(BF16)
    row_in_sub = lax.broadcasted_iota(jnp.int32, (n_sub, sub, C_W), 1)
    split = lambda a: a.reshape(n_sub, sub, C_W)

    def tile_rows(t):
        return pl.ds(pl.multiple_of(t * tile, tile), tile)

    def decay_pass(t, _):
        rows = tile_rows(t)
        lf, q, k, v = lf_ref[rows, :], q_ref[rows, :], k_ref[rows, :], v_ref[rows, :]
        l1 = lf.astype(BF16)
        r1 = lf - l1.astype(F32)
        l2 = r1.astype(BF16)
        l3 = (r1 - l2.astype(F32)).astype(BF16)
        cum = _dot(tri_blk, l1) + _dot(tri_blk, l2) + _dot(tri_blk, l3)
        cum3, q3, k3, v3 = split(cum), split(q), split(k), split(v)
        last3 = cum3[:, sub - 1:sub, :]
        qd_ref[rows, :] = (q * jnp.exp(cum)).astype(BF16)
        kd_ref[rows, :] = (k3 * jnp.exp(last3 - cum3)).reshape(tile, C_W).astype(BF16)
        v16_ref[rows, :] = v.astype(BF16)
        dl_ref[rows, :] = jnp.broadcast_to(jnp.exp(last3), (n_sub, sub, C_W)).reshape(tile, C_W)
        o3 = jnp.zeros((n_sub, sub, C_W), F32)
        for s in range(sub):
            dec = jnp.exp(jnp.where(row_in_sub >= s, cum3 - cum3[:, s:s + 1, :], NEG_BIG))
            m = (q3 * k3[:, s:s + 1, :] * dec).reshape(tile, C_W).astype(BF16)
            w = jnp.concatenate([_dot(m[:, :MXU_W], ones_bd[:MXU_W, :MXU_W]),
                                 _dot(m[:, MXU_W:], ones_bd[MXU_W:, MXU_W:])], axis=1)
            o3 = o3 + split(w) * v3[:, s:s + 1, :]
        o_ref[rows, :] = o3.reshape(tile, C_W)
        return 0

    lax.fori_loop(0, seq // tile, decay_pass, 0)
    st_ref[...] = st0_ref[...]

    group = upd_ref.shape[0]

    def chunks(g, _):
        for j in range(group):
            rows = pl.ds(pl.multiple_of((g * group + j) * sub, sub), sub)
            upd_ref[j] = jnp.where(same_head, _dot_tn(v16_ref[rows, :], kd_ref[rows, :]), 0.0)
        for j in range(group):
            c = g * group + j
            rows = pl.ds(pl.multiple_of(c * sub, sub), sub)
            st = st_ref[...]
            o_ref[rows, :] = o_ref[rows, :] + _dot_nt(qd_ref[rows, :], st.astype(BF16))
            st_ref[...] = st * dl_ref[pl.ds(c * sub, 1), :] + upd_ref[j]
        return 0

    lax.fori_loop(0, seq // (sub * group), chunks, 0)

    gn = gn_ref[...]

    def norm_pass(t, _):
        rows = tile_rows(t)
        o = o_ref[rows, :]
        sq = o * o
        hi = sq.astype(BF16)
        lo = (sq - hi.astype(F32)).astype(BF16)
        ms = (_dot(hi, ones_bd) + _dot(lo, ones_bd)) * (1.0 / HEAD_DIM)
        o_ref[rows, :] = o * lax.rsqrt(ms + EPS) * gn * gs_ref[rows, :]
        return 0

    lax.fori_loop(0, seq // tile, norm_pass, 0)


def _hgrn(q, k, lf, v, gs, gn, st0, batch, seq):
    sub = math.gcd(seq, 16)
    tile = min(seq, 256)
    assert seq % tile == 0 and tile % sub == 0
    row = pl.BlockSpec((seq, C_W), lambda b: (b, 0))
    st_spec = pl.BlockSpec((None, C_W, C_W), lambda b: (b, 0, 0))
    group = math.gcd(seq // sub, 8)
    est = (2 * 6 * seq * C_W * 4 + (8 + group) * C_W * C_W * 4 + seq * C_W * (3 * 2 + 4)
           + 24 * tile * C_W * 4)
    return pl.pallas_call(
        functools.partial(_hgrn_body, sub=sub, tile=tile),
        out_shape=[jax.ShapeDtypeStruct((batch * seq, C_W), F32),
                   jax.ShapeDtypeStruct((batch, C_W, C_W), F32)],
        grid=(batch,),
        in_specs=[row, row, row, row, row, pl.BlockSpec((1, C_W), lambda b: (0, 0)), st_spec],
        out_specs=[row, st_spec],
        scratch_shapes=[pltpu.VMEM((seq, C_W), BF16), pltpu.VMEM((seq, C_W), BF16),
                        pltpu.VMEM((seq, C_W), BF16), pltpu.VMEM((seq, C_W), F32),
                        pltpu.VMEM((group, C_W, C_W), F32)],
        compiler_params=pltpu.CompilerParams(
            dimension_semantics=("arbitrary",), vmem_limit_bytes=_vmem_limit(est)),
        name="hgrn",
    )(q, k, lf, v, gs, gn, st0)


def _block_diag_q(q, heads, scale):
    t, w = q.shape
    rep = jnp.concatenate([q] * heads, axis=0)
    r = lax.broadcasted_iota(jnp.int32, (heads * t, w), 0) // t
    c = lax.broadcasted_iota(jnp.int32, (heads * t, w), 1) // HEAD_DIM
    return jnp.where(r == c, rep * scale, 0.0)


def _collapse_heads(o, heads, t):
    r = lax.broadcasted_iota(jnp.int32, o.shape, 0) // t
    c = lax.broadcasted_iota(jnp.int32, o.shape, 1) // HEAD_DIM
    o = jnp.where(r == c, o, 0.0)
    out = o[0:t, :]
    for h in range(1, heads):
        out = out + o[h * t:(h + 1) * t, :]
    return out


def _rows_per_head(x, heads, t):
    return jnp.concatenate([jnp.broadcast_to(x[h:h + 1, :], (t, x.shape[1])) for h in range(heads)], axis=0)


def _decode_softmax(s_own, s_ref, p_ref, l_ref, n_pages, page_fn):
    mx = jnp.max(s_own, axis=1, keepdims=True)
    for pg in range(n_pages):
        sm = page_fn(pg, s_ref[pg])
        s_ref[pg] = sm
        mx = jnp.maximum(mx, jnp.max(sm, axis=1, keepdims=True))
    p_own = jnp.exp(s_own - mx)
    l = jnp.sum(p_own, axis=1, keepdims=True)
    for pg in range(n_pages):
        p = jnp.exp(s_ref[pg] - mx)
        l = l + jnp.sum(p, axis=1, keepdims=True)
        p_ref[pg] = p.astype(BF16)
    l_ref[...] = l
    return p_own


def _decode_values(j, pp, p_ref, v_refs, acc_ref, width):
    acc = acc_ref[...]
    for jj in range(pp):
        vt = v_refs[jj][...].reshape(width, -1).astype(BF16)
        acc = acc + _dot_nt(p_ref[j * pp + jj], vt)
    acc_ref[...] = acc


def _moba_decode_body(pt_ref, q_ref, kn_ref, vn_ref, *rest, pp, n_steps, t_new):
    k_refs, v_refs = rest[:pp], rest[pp:2 * pp]
    o_ref, qbd_ref, s_ref, p_ref, kmean_ref, acc_ref, l_ref = rest[2 * pp:]
    ph, j = pl.program_id(1), pl.program_id(2)
    page = k_refs[0].shape[-1]
    rows = A_HEADS * t_new
    n_pages = n_steps * pp
    per_blk = MOBA_BLOCK // page
    n_blk = n_pages // per_blk
    scale = HEAD_DIM ** -0.5

    @pl.when((ph == 0) & (j == 0))
    def _():
        qbd_ref[...] = _block_diag_q(q_ref[...], A_HEADS, 1.0)
        kmean_ref[...] = jnp.zeros(kmean_ref.shape, F32)

    @pl.when(ph == 0)
    def _():
        qb = (qbd_ref[...] * scale).astype(BF16)
        lane = lax.broadcasted_iota(jnp.int32, (A_W, LANES), 1)
        kmean = kmean_ref[...]
        for g in range(pp // per_blk):
            ksum = jnp.zeros((A_W, page), F32)
            for u in range(per_blk):
                jj = g * per_blk + u
                kt = k_refs[jj][...].reshape(A_W, page)
                s_ref[j * pp + jj] = _dot(qb, kt.astype(BF16))
                ksum = ksum + kt
            kcol = jnp.sum(ksum, axis=1, keepdims=True) * (1.0 / MOBA_BLOCK)
            kmean = kmean + jnp.where(lane == j * (pp // per_blk) + g, kcol, 0.0)
        kmean_ref[...] = kmean

    @pl.when((ph == 0) & (j == n_steps - 1))
    def _():
        qbd = qbd_ref[...]
        gate = _dot(qbd, kmean_ref[...], precision=HIGHEST)
        lane_blk = lax.broadcasted_iota(jnp.int32, (rows, LANES), 1)
        sel = _topk_mask(gate, lane_blk < n_blk, MOBA_TOPK, axis=1)
        qb = (qbd * scale).astype(BF16)
        s_own = _dot_nt(qb, kn_ref[...].astype(BF16))
        qi = lax.broadcasted_iota(jnp.int32, (rows, t_new), 0) % t_new
        kj = lax.broadcasted_iota(jnp.int32, (rows, t_new), 1)
        s_own = jnp.where(kj <= qi, s_own, NEG_BIG)

        def page_fn(pg, s):
            n = pg // per_blk
            return jnp.where(sel[:, n:n + 1] > 0.0, s, NEG_BIG)

        p_own = _decode_softmax(s_own, s_ref, p_ref, l_ref, n_pages, page_fn)
        acc_ref[...] = _dot(p_own.astype(BF16), vn_ref[...].astype(BF16))

    @pl.when(ph == 1)
    def _():
        _decode_values(j, pp, p_ref, v_refs, acc_ref, A_W)

    @pl.when((ph == 1) & (j == n_steps - 1))
    def _():
        o_ref[...] = _collapse_heads(acc_ref[...] / l_ref[...], A_HEADS, t_new)


def _page_specs(block, index_fn, pp, n_steps, first_phase):
    specs = []
    for jj in range(pp):
        def imap(b, ph, j, pt, jj=jj):
            if first_phase:
                step = jnp.where(ph == 0, j, n_steps - 1)
            else:
                step = jnp.where(ph == 0, 0, j)
            return index_fn(pt[b, step * pp + jj])
        specs.append(pl.BlockSpec(block, imap))
    return specs


def _moba_decode(page_table, q, k_new, v_new, cache_kt, cache_vt, layer, n_seq, t_new, pp=16):
    n_pages = page_table.shape[1]
    page = cache_kt.shape[-1]
    assert MOBA_BLOCK % page == 0 and n_pages % pp == 0 and pp % (MOBA_BLOCK // page) == 0
    assert n_pages * page // MOBA_BLOCK <= LANES
    n_steps = n_pages // pp
    rows = A_HEADS * t_new
    new = pl.BlockSpec((t_new, A_W), lambda b, ph, j, pt: (b, 0))
    blk = (None, None, A_HEADS, HEAD_DIM, page)
    idx = lambda p: (layer, p, 0, 0, 0)
    est = 2 * 2 * pp * page * A_W * 4 + rows * n_pages * page * 6 + 4 * LANES * A_W * 4
    grid_spec = pltpu.PrefetchScalarGridSpec(
        num_scalar_prefetch=1,
        grid=(n_seq, 2, n_steps),
        in_specs=[new, new, new]
        + _page_specs(blk, idx, pp, n_steps, True) + _page_specs(blk, idx, pp, n_steps, False),
        out_specs=new,
        scratch_shapes=[pltpu.VMEM((rows, A_W), F32), pltpu.VMEM((n_pages, rows, page), F32),
                        pltpu.VMEM((n_pages, rows, page), BF16), pltpu.VMEM((A_W, LANES), F32),
                        pltpu.VMEM((rows, A_W), F32), pltpu.VMEM((rows, 1), F32)],
    )
    return pl.pallas_call(
        functools.partial(_moba_decode_body, pp=pp, n_steps=n_steps, t_new=t_new),
        out_shape=jax.ShapeDtypeStruct((n_seq * t_new, A_W), F32),
        grid_spec=grid_spec,
        compiler_params=pltpu.CompilerParams(
            dimension_semantics=("arbitrary", "arbitrary", "arbitrary"), vmem_limit_bytes=_vmem_limit(est)),
        name="moba_decode",
    )(page_table, q, k_new, v_new, *([cache_kt] * pp), *([cache_vt] * pp))


def _fox_decode_body(pt_ref, q_ref, kn_ref, vn_ref, lfn_ref, *rest, pp, n_steps, t_new):
    k_refs, v_refs, lf_refs = rest[:pp], rest[pp:2 * pp], rest[2 * pp:3 * pp]
    o_ref, qbd_ref, s_ref, p_ref, carry_ref, acc_ref, l_ref = rest[3 * pp:]
    ph, j = pl.program_id(1), pl.program_id(2)
    page = k_refs[0].shape[-1]
    rows = B_HEADS * t_new
    n_pages = n_steps * pp
    scale = HEAD_DIM ** -0.5

    @pl.when((ph == 0) & (j == 0))
    def _():
        qbd_ref[...] = _block_diag_q(q_ref[...], B_HEADS, scale).astype(BF16)
        carry_ref[...] = jnp.zeros(carry_ref.shape, F32)

    @pl.when(ph == 0)
    def _():
        qb = qbd_ref[...]
        r = lax.broadcasted_iota(jnp.int32, (page, page), 0)
        c = lax.broadcasted_iota(jnp.int32, (page, page), 1)
        upper = (r <= c).astype(F32)
        carry = carry_ref[...]
        for jj in range(pp):
            lf = lf_refs[jj][...]
            cpage = _dot(lf, upper, precision=HIGHEST) + carry
            carry = carry + jnp.sum(lf, axis=1, keepdims=True)
            kt = k_refs[jj][...].reshape(B_W, page).astype(BF16)
            s_ref[j * pp + jj] = _dot(qb, kt) - _rows_per_head(cpage, B_HEADS, t_new)
        carry_ref[...] = carry

    @pl.when((ph == 0) & (j == n_steps - 1))
    def _():
        qb = qbd_ref[...]
        r8 = lax.broadcasted_iota(jnp.int32, (t_new, t_new), 0)
        c8 = lax.broadcasted_iota(jnp.int32, (t_new, t_new), 1)
        cum_new = _dot(lfn_ref[...], (r8 <= c8).astype(F32), precision=HIGHEST)
        cn = _rows_per_head(cum_new, B_HEADS, t_new)
        qi = lax.broadcasted_iota(jnp.int32, (rows, t_new), 0) % t_new
        kj = lax.broadcasted_iota(jnp.int32, (rows, t_new), 1)
        cq_new = jnp.sum(jnp.where(kj == qi, cn, 0.0), axis=1, keepdims=True)
        ct = _rows_per_head(carry_ref[...], B_HEADS, t_new) + cq_new
        s_own = _dot_nt(qb, kn_ref[...].astype(BF16)) + (cq_new - cn)
        s_own = jnp.where(kj <= qi, s_own, NEG_BIG)
        p_own = _decode_softmax(s_own, s_ref, p_ref, l_ref, n_pages, lambda pg, s: s + ct)
        acc_ref[...] = _dot(p_own.astype(BF16), vn_ref[...].astype(BF16))

    @pl.when(ph == 1)
    def _():
        _decode_values(j, pp, p_ref, v_refs, acc_ref, B_W)

    @pl.when((ph == 1) & (j == n_steps - 1))
    def _():
        o_ref[...] = _collapse_heads(acc_ref[...] / l_ref[...], B_HEADS, t_new)


def _fox_decode(page_table, q, k_new, v_new, lf_new_t, cache_kt, cache_vt, cache_lf_t, layer, n_seq, t_new, pp=16):
    n_pages = page_table.shape[1]
    page = cache_kt.shape[-1]
    assert n_pages % pp == 0
    n_steps = n_pages // pp
    rows = B_HEADS * t_new
    new = pl.BlockSpec((t_new, B_W), lambda b, ph, j, pt: (b, 0))
    blk = (None, None, B_HEADS, HEAD_DIM, page)
    idx = lambda p: (layer, p, 0, 0, 0)
    est = 2 * 2 * pp * page * (B_W + 8) * 4 + rows * n_pages * page * 6
    grid_spec = pltpu.PrefetchScalarGridSpec(
        num_scalar_prefetch=1,
        grid=(n_seq, 2, n_steps),
        in_specs=[new, new, new, pl.BlockSpec((None, B_HEADS, t_new), lambda b, ph, j, pt: (b, 0, 0))]
        + _page_specs(blk, idx, pp, n_steps, True) + _page_specs(blk, idx, pp, n_steps, False)
        + _page_specs((None, None, B_HEADS, page), lambda p: (layer, p, 0, 0), pp, n_steps, True),
        out_specs=new,
        scratch_shapes=[pltpu.VMEM((rows, B_W), BF16), pltpu.VMEM((n_pages, rows, page), F32),
                        pltpu.VMEM((n_pages, rows, page), BF16), pltpu.VMEM((B_HEADS, 1), F32),
                        pltpu.VMEM((rows, B_W), F32), pltpu.VMEM((rows, 1), F32)],
    )
    return pl.pallas_call(
        functools.partial(_fox_decode_body, pp=pp, n_steps=n_steps, t_new=t_new),
        out_shape=jax.ShapeDtypeStruct((n_seq * t_new, B_W), F32),
        grid_spec=grid_spec,
        compiler_params=pltpu.CompilerParams(
            dimension_semantics=("arbitrary", "arbitrary", "arbitrary"), vmem_limit_bytes=_vmem_limit(est)),
        name="fox_decode",
    )(page_table, q, k_new, v_new, lf_new_t, *([cache_kt] * pp), *([cache_vt] * pp), *([cache_lf_t] * pp))


def _pad_w_in(w):
    sizes = (A_W,) * 3 + (B_W,) * 3 + (B_HEADS,) + (C_W,) * 4
    offs = np.cumsum((0,) + sizes)
    segs = [SEG_QA, SEG_KA, SEG_VA, SEG_QF, SEG_KF, SEG_VF, SEG_FF, SEG_QC, SEG_FC, SEG_IC, SEG_GC]
    out = jnp.zeros((w.shape[0], D_IN_PAD), BF16)
    for s, o, n in zip(segs, offs[:-1], sizes):
        out = out.at[:, s:s + n].set(w[:, o:o + n].astype(BF16))
    return out


def _rope_tables(pos0, t, reps):
    half = HEAD_DIM // 2
    inv = ROPE_THETA ** (-jnp.arange(half, dtype=F32) * 2.0 / HEAD_DIM)
    ang = (pos0 + jnp.arange(t, dtype=F32))[:, None] * inv[None, :]
    cos, sin, zero = jnp.cos(ang), jnp.sin(ang), jnp.zeros((t, half), F32)
    heads_per_vreg = LANES // HEAD_DIM
    tabs = (jnp.concatenate([cos, cos] * heads_per_vreg, axis=1),
            jnp.concatenate([-sin, zero] * heads_per_vreg, axis=1),
            jnp.concatenate([zero, sin] * heads_per_vreg, axis=1))
    return tuple(jnp.tile(tb, (reps, 1)) for tb in tabs)


def _pad_lanes(a, width):
    return jnp.pad(a, ((0, 0), (0, width - a.shape[1])))


def _state_to_bd(s):
    b, h, dk, dv = s.shape
    st = jnp.swapaxes(s, 2, 3)
    eye = jnp.eye(h, dtype=s.dtype)
    return (st[:, :, :, None, :] * eye[None, :, None, :, None]).reshape(b, h * dv, h * dk)


def _bd_to_state(st, h):
    b = st.shape[0]
    blocks = st.reshape(b, h, HEAD_DIM, h, HEAD_DIM)
    diag = jnp.stack([blocks[:, i, :, i, :] for i in range(h)], axis=1)
    return jnp.swapaxes(diag, 2, 3)


def kernel(x_prompt, x_sample, cache_moba_k, cache_moba_v, cache_fox_k, cache_fox_v, cache_fox_logf, state_hgrn, page_table, norm_ffn1_g, ffn1_w_gate, ffn1_w_up, ffn1_w_down, norm_mix_g, w_in, fox_f_bias, hgrn_lb_logits, hgrn_norm_g, w_out, norm_ffn2_g, ffn2_w_gate, ffn2_w_up, ffn2_w_down, norm_final_g):
    depth = w_in.shape[0]
    batch, seq, d = x_prompt.shape
    n_seq, t_new, _ = x_sample.shape
    page = cache_moba_k.shape[2]
    past_len = page_table.shape[1] * page

    bf = lambda a: a.astype(BF16)
    row = lambda a: a.reshape(1, -1)
    w1g, w1u, w1d = bf(ffn1_w_gate), bf(ffn1_w_up), bf(ffn1_w_down)
    w2g, w2u, w2d = bf(ffn2_w_gate), bf(ffn2_w_up), bf(ffn2_w_down)
    wo = bf(w_out)
    w_in_pad = [_pad_w_in(w_in[l]) for l in range(depth)]
    fb = [_pad_lanes(row(fox_f_bias[l]), LANES) for l in range(depth)]
    lbl = _pad_lanes(hgrn_lb_logits.astype(F32), SEG_W)
    gn = jnp.tile(hgrn_norm_g, (1, C_HEADS))
    to_pages = lambda c: jnp.transpose(c, (0, 1, 3, 4, 2))
    ckt, cvt, fkt, fvt = (to_pages(c) for c in (cache_moba_k, cache_moba_v, cache_fox_k, cache_fox_v))
    flt = jnp.swapaxes(cache_fox_logf, 2, 3)

    def trunk(x, sample):
        if sample:
            nb, t = n_seq, t_new
            tabs = _rope_tables(float(past_len), t, nb)
        else:
            nb, t = batch, seq
            tabs = _rope_tables(0.0, t, 1)
        new = []
        for l in range(depth):
            if l == 0:
                x = _ffn(x, row(norm_ffn1_g[l]), w1g[l], w1u[l], w1d[l])
            args = (x, row(norm_mix_g[l]), w_in_pad[l], tabs, fb[l], lbl, l)
            heads_last = lambda a, h: a.reshape(nb, t, h, HEAD_DIM)
            if sample:
                qa, ka, va, qf, kf, vf, lf, qc, kc, lc, ic, gc = _inproj(*args)
                oa = _moba_decode(page_table, qa, ka, va, ckt, cvt, l, nb, t)
                lf_t = jnp.swapaxes(lf[:, :B_HEADS].reshape(nb, t, B_HEADS), 1, 2)
                of = _fox_decode(page_table, qf, kf, vf, lf_t, fkt, fvt, flt, l, nb, t)
                st0 = _state_to_bd(state_hgrn[l].astype(F32))
                rows_kv = (heads_last(ka, A_HEADS), heads_last(va, A_HEADS),
                           heads_last(kf, B_HEADS), heads_last(vf, B_HEADS))
            else:
                qa, ka, ka_t, va_t, qf, kf, kf_t, vf_t, lf, qc, kc, lc, ic, gc = _inproj(*args, seq_t=t)
                oa = _moba_prompt(qa, ka, va_t, nb, t)
                of = _fox_prompt(qf, kf, vf_t, lf, nb, t)
                st0 = jnp.zeros((nb, C_W, C_W), F32)
                from_t = lambda a, h: jnp.transpose(a.reshape(nb, h, HEAD_DIM, t), (0, 3, 1, 2))
                rows_kv = (from_t(ka_t, A_HEADS), from_t(va_t, A_HEADS),
                           from_t(kf_t, B_HEADS), from_t(vf_t, B_HEADS))
            oc, st = _hgrn(qc, kc, lc, ic, gc, row(gn[l]), st0, nb, t)
            mix = (oa, of, oc, wo[l, :A_W], wo[l, A_W:A_W + B_W], wo[l, A_W + B_W:])
            last = l == depth - 1
            x = _ffn(x, row(norm_ffn2_g[l]), w2g[l], w2u[l], w2d[l], mix=mix,
                     g_final=row(norm_final_g) if last else None)
            if not last:
                x = _ffn(x, row(norm_ffn1_g[l + 1]), w1g[l + 1], w1u[l + 1], w1d[l + 1])
            new.append(rows_kv + (lf[:, :B_HEADS].reshape(nb, t, B_HEADS), _bd_to_state(st, C_HEADS)))
        return x.reshape(nb, t, d), [jnp.stack(z) for z in zip(*new)]

    y_p, (mk_p, mv_p, fk_p, fv_p, fl_p, hs_p) = trunk(x_prompt.reshape(batch * seq, d), False)
    y_s, (mk_s, mv_s, fk_s, fv_s, fl_s, hs_s) = trunk(x_sample.reshape(n_seq * t_new, d), True)
    return (y_p, y_s, mk_p, mv_p, fk_p, fv_p, fl_p, hs_p, mk_s, mv_s, fk_s, fv_s, fl_s, hs_s)
```

```python
import functools
import math

import numpy as np
import jax
import jax.numpy as jnp
from jax import lax
from jax.experimental import pallas as pl
from jax.experimental.pallas import tpu as pltpu

F32 = jnp.float32
BF16 = jnp.bfloat16
HIGHEST = lax.Precision.HIGHEST

HEAD_DIM = 64
A_HEADS = 6
B_HEADS = 5
C_HEADS = 5
A_W = A_HEADS * HEAD_DIM
B_W = B_HEADS * HEAD_DIM
C_W = C_HEADS * HEAD_DIM
D_FF_CHUNK = 256
MOBA_BLOCK = 256
MOBA_TOPK = 3
ROPE_THETA = 10000.0
EPS = 1e-6
NEG_BIG = -1e30

LANES = 128
SUBLANES = 8
MXU_W = 256
VMEM_PHYSICAL = 64 * 1024 * 1024
VMEM_CAP = VMEM_PHYSICAL - 8 * 1024 * 1024

SEG_W = 384
SEG_QA, SEG_KA, SEG_VA, SEG_QF, SEG_KF, SEG_VF, SEG_QC, SEG_FC, SEG_IC, SEG_GC = (
    i * SEG_W for i in range(10))
SEG_FF = 10 * SEG_W
D_IN_PAD = SEG_FF + LANES


def _vmem_limit(estimate_bytes):
    return int(min(max(2 * estimate_bytes, 32 * 1024 * 1024), VMEM_CAP))


def _dot(a, b, precision=None):
    return jnp.dot(a, b, preferred_element_type=F32, precision=precision)


def _dot_nt(a, b, precision=None):
    return lax.dot_general(a, b, (((1,), (1,)), ((), ())),
                           preferred_element_type=F32, precision=precision)


def _dot_tn(a, b, precision=None):
    return lax.dot_general(a, b, (((0,), (0,)), ((), ())),
                           preferred_element_type=F32, precision=precision)


def _rms(x, g):
    ms = jnp.mean(x * x, axis=-1, keepdims=True)
    return x * lax.rsqrt(ms + EPS) * g


def _sigmoid(x):
    return 1.0 / (1.0 + jnp.exp(-x))


def _silu(x):
    return x * _sigmoid(x)


def _log_sigmoid(x):
    return jnp.minimum(x, 0.0) - jnp.log1p(jnp.exp(-jnp.abs(x)))


def _bf16_pieces(x):
    x1 = x.astype(BF16)
    r1 = x - x1.astype(F32)
    x2 = r1.astype(BF16)
    return x1, x2, (r1 - x2.astype(F32)).astype(BF16)


def _dot_sel(x, sel):
    return sum(_dot(p, sel) for p in _bf16_pieces(x))


def _sel_dot(sel, x):
    return sum(_dot(sel, p) for p in _bf16_pieces(x))


def _ffn_body(*refs, has_mix, has_final, d_ff):
    it = iter(refs)
    x_ref = next(it)
    if has_mix:
        oa_ref, of_ref, oc_ref, woa_ref, wof_ref, woc_ref = (next(it) for _ in range(6))
    g_ref, wg_ref, wu_ref, wd_ref = (next(it) for _ in range(4))
    gf_ref = next(it) if has_final else None
    out_ref = next(it)

    x = x_ref[...]
    if has_mix:
        x = (x + _dot(oa_ref[...].astype(BF16), woa_ref[...])
             + _dot(of_ref[...].astype(BF16), wof_ref[...])
             + _dot(oc_ref[...].astype(BF16), woc_ref[...]))
    h = _rms(x, g_ref[...]).astype(BF16)
    acc = jnp.zeros(x.shape, F32)
    for c0 in range(0, d_ff, D_FF_CHUNK):
        gate = _dot(h, wg_ref[:, c0:c0 + D_FF_CHUNK])
        up = _dot(h, wu_ref[:, c0:c0 + D_FF_CHUNK])
        act = (_silu(gate) * up).astype(BF16)
        acc = acc + _dot(act, wd_ref[c0:c0 + D_FF_CHUNK, :])
    y = x + 0.5 * acc
    if has_final:
        y = _rms(y, gf_ref[...])
    out_ref[...] = y


def _ffn(x, g, wg, wu, wd, mix=None, g_final=None, tm=512):
    n, d = x.shape
    d_ff = wg.shape[1]
    tm = min(tm, n)
    assert n % tm == 0 and d_ff % D_FF_CHUNK == 0
    row = lambda w: pl.BlockSpec((tm, w), lambda i: (i, 0))
    full = lambda a: pl.BlockSpec(a.shape, lambda i: (0,) * a.ndim, pipeline_mode=pl.Buffered(1))
    args, specs = [x], [row(d)]
    if mix is not None:
        oa, of, oc, woa, wof, woc = mix
        args += [oa, of, oc, woa, wof, woc]
        specs += [row(oa.shape[1]), row(of.shape[1]), row(oc.shape[1]), full(woa), full(wof), full(woc)]
    args += [g, wg, wu, wd]
    specs += [full(g), full(wg), full(wu), full(wd)]
    if g_final is not None:
        args.append(g_final)
        specs.append(full(g_final))
    weights = 2 * 3 * d * d_ff + (2 * d * d if mix is not None else 0)
    tiles = 2 * 2 * tm * d * 4 * (2 if mix is not None else 1) + 6 * tm * d * 4
    return pl.pallas_call(
        functools.partial(_ffn_body, has_mix=mix is not None, has_final=g_final is not None, d_ff=d_ff),
        out_shape=jax.ShapeDtypeStruct((n, d), F32),
        grid=(n // tm,),
        in_specs=specs,
        out_specs=row(d),
        compiler_params=pltpu.CompilerParams(
            dimension_semantics=("arbitrary",), vmem_limit_bytes=_vmem_limit(weights + tiles)),
        name="ffn_mix" if mix is not None else "ffn",
    )(*args)


def _inproj_body(x_ref, g_ref, w_ref, cos_ref, slo_ref, shi_ref, fb_ref, lbl_ref, *out_refs, layer, transposed):
    if transposed:
        (qa_ref, ka_ref, kat_ref, vat_ref, qf_ref, kf_ref, kft_ref, vft_ref, lf_ref,
         qc_ref, kc_ref, lc_ref, ic_ref, gc_ref, stage_ref) = out_refs
    else:
        (qa_ref, ka_ref, va_ref, qf_ref, kf_ref, vf_ref, lf_ref,
         qc_ref, kc_ref, lc_ref, ic_ref, gc_ref) = out_refs
    h = _rms(x_ref[...], g_ref[...]).astype(BF16)

    def proj(c0, width=SEG_W):
        return _dot(h, w_ref[:, c0:c0 + width])

    cos, slo, shi = cos_ref[...], slo_ref[...], shi_ref[...]

    def rope(p):
        parts = []
        for c in range(SEG_W // LANES):
            pc = p[:, c * LANES:(c + 1) * LANES]
            parts.append(pc * cos + pltpu.roll(pc, LANES - HEAD_DIM // 2, 1) * slo
                         + pltpu.roll(pc, HEAD_DIM // 2, 1) * shi)
        return jnp.concatenate(parts, axis=1)

    qa_ref[...] = rope(proj(SEG_QA))
    ka = rope(proj(SEG_KA))
    ka_ref[...] = ka
    if transposed:
        def transposed_tile(p):
            stage_ref[...] = p
            return stage_ref[...].T

        kat_ref[...] = ka.T
        vat_ref[...] = transposed_tile(proj(SEG_VA))
        qf_ref[...] = proj(SEG_QF)
        kf_ref[...] = proj(SEG_KF)
        kft_ref[...] = kf_ref[...].T[:B_W, :]
        vft_ref[...] = transposed_tile(proj(SEG_VF))[:B_W, :]
    else:
        va_ref[...] = proj(SEG_VA)
        qf_ref[...] = proj(SEG_QF)[:, :B_W]
        kf_ref[...] = proj(SEG_KF)[:, :B_W]
        vf_ref[...] = proj(SEG_VF)[:, :B_W]
    lf_ref[...] = _log_sigmoid(proj(SEG_FF, LANES) + fb_ref[...])

    lbl = lbl_ref[...]
    e = jnp.exp(lbl - jnp.max(lbl, axis=0, keepdims=True))
    psm = e / jnp.sum(e, axis=0, keepdims=True)
    lb = jnp.zeros((1, SEG_W), F32)
    for j in range(1, layer + 1):
        lb = lb + psm[j:j + 1, :]
    fc = proj(SEG_FC)
    lc_ref[...] = jnp.log(lb + (1.0 - lb) * _sigmoid(fc))[:, :C_W]
    kc_ref[...] = ((1.0 - lb) * _sigmoid(-fc))[:, :C_W]
    qc_ref[...] = _silu(proj(SEG_QC))[:, :C_W]
    ic_ref[...] = proj(SEG_IC)[:, :C_W]
    gc_ref[...] = _silu(proj(SEG_GC))[:, :C_W]


def _inproj(x, g, w_pad, rope_tabs, fb, lbl, layer, seq_t=None, tm=256):
    n, d = x.shape
    tm = min(tm, n)
    assert n % tm == 0 and rope_tabs[0].shape[0] % tm == 0
    row = lambda w: pl.BlockSpec((tm, w), lambda i: (i, 0))
    full = lambda a: pl.BlockSpec(a.shape, lambda i: (0,) * a.ndim, pipeline_mode=pl.Buffered(1))
    n_tab = rope_tabs[0].shape[0] // tm
    tab = pl.BlockSpec((tm, LANES), lambda i: (i % n_tab, 0))
    if seq_t is None:
        widths = [A_W, A_W, A_W, B_W, B_W, B_W, LANES, C_W, C_W, C_W, C_W, C_W]
        shapes = [(n, w) for w in widths]
        out_specs = [row(w) for w in widths]
    else:
        assert seq_t % tm == 0
        per_seq = seq_t // tm
        col = lambda w: pl.BlockSpec((None, w, tm), lambda i: (i // per_seq, 0, i % per_seq))
        kinds = [(row, A_W), (row, A_W), (col, A_W), (col, A_W), (row, SEG_W), (row, SEG_W), (col, B_W),
                 (col, B_W), (row, LANES), (row, C_W), (row, C_W), (row, C_W), (row, C_W), (row, C_W)]
        widths = [w for _, w in kinds]
        shapes = [(n, w) if f is row else (n // seq_t, w, seq_t) for f, w in kinds]
        out_specs = [f(w) for f, w in kinds]
    est = 2 * d * D_IN_PAD + 2 * tm * 4 * (d + sum(widths) + 3 * LANES) + 8 * tm * SEG_W * 4
    return pl.pallas_call(
        functools.partial(_inproj_body, layer=layer, transposed=seq_t is not None),
        out_shape=[jax.ShapeDtypeStruct(s, F32) for s in shapes],
        grid=(n // tm,),
        in_specs=[row(d), full(g), full(w_pad), tab, tab, tab, full(fb), full(lbl)],
        out_specs=out_specs,
        scratch_shapes=[pltpu.VMEM((tm, SEG_W), F32)] if seq_t is not None else [],
        compiler_params=pltpu.CompilerParams(
            dimension_semantics=("arbitrary",), vmem_limit_bytes=_vmem_limit(est)),
        name="inproj",
    )(x, g, w_pad, *rope_tabs, fb, lbl)


def _topk_mask(gate, allowed, topk, axis):
    n = gate.shape[axis]
    pos = lax.broadcasted_iota(jnp.int32, gate.shape, axis).astype(F32)
    g = jnp.where(allowed, gate, NEG_BIG)
    sel = jnp.zeros(gate.shape, F32)
    for _ in range(topk):
        m = jnp.max(g, axis=axis, keepdims=True)
        idx = jnp.min(jnp.where(g == m, pos, float(n)), axis=axis, keepdims=True)
        hit = pos == idx
        sel = jnp.where(hit & allowed, 1.0, sel)
        g = jnp.where(hit, -jnp.inf, g)
    return sel


def _own_lanes(shape, h):
    return lax.broadcasted_iota(jnp.int32, shape, 1) // HEAD_DIM == h % 2


def _flash_step(heads, score_fn, value_fn, s_ref, p_ref, m_ref, l_ref, acc_ref):
    for h in range(heads):
        s_ref[h] = score_fn(h)
    alphas = []
    for h in range(heads):
        m_old = m_ref[h]
        m_new = jnp.maximum(m_old, jnp.max(s_ref[h], axis=0, keepdims=True))
        alpha = jnp.exp(m_old - m_new)
        p = jnp.exp(s_ref[h] - m_new)
        l_ref[h] = alpha * l_ref[h] + jnp.sum(p, axis=0, keepdims=True)
        m_ref[h] = m_new
        p_ref[h] = p.astype(BF16)
        alphas.append(alpha)
    for h in range(heads):
        acc_ref[h] = acc_ref[h] * alphas[h] + _dot(value_fn(h), p_ref[h])


def _flash_init(m_ref, l_ref, acc_ref):
    m_ref[...] = jnp.full(m_ref.shape, -jnp.inf, F32)
    l_ref[...] = jnp.zeros(l_ref.shape, F32)
    acc_ref[...] = jnp.zeros(acc_ref.shape, F32)


def _flash_finish(o_ref, l_ref, acc_ref, heads):
    blk = acc_ref.shape[2]
    parts = [acc_ref[h] / l_ref[h] for h in range(heads)]
    parts += [jnp.zeros((HEAD_DIM, blk), F32)] * (SEG_W // HEAD_DIM - heads)
    o_ref[...] = jnp.concatenate(parts, axis=0).T[:, :o_ref.shape[1]]


def _flash_scratch(heads, blk):
    return [pltpu.VMEM((heads, blk, blk), F32), pltpu.VMEM((heads, blk, blk), BF16),
            pltpu.VMEM((heads, 1, blk), F32), pltpu.VMEM((heads, 1, blk), F32),
            pltpu.VMEM((heads, HEAD_DIM, blk), F32)]


def _moba_prompt_body(q_ref, k_ref, vt_ref, o_ref, kmean_ref, k16_ref, vt16_ref, qm_ref, sel_ref,
                      s_ref, p_ref, m_ref, l_ref, acc_ref, *, blk):
    i = pl.program_id(1)
    nb = k_ref.shape[0] // blk
    scale = HEAD_DIM ** -0.5

    @pl.when(i == 0)
    def _():
        for n in range(nb):
            kb = k_ref[n * blk:(n + 1) * blk, :]
            kmean_ref[n:n + 1, :] = jnp.sum(kb, axis=0, keepdims=True) * (1.0 / blk)
            k16_ref[n * blk:(n + 1) * blk, :] = kb.astype(BF16)
            vt16_ref[n] = vt_ref[:, n * blk:(n + 1) * blk].astype(BF16)

    past = lax.broadcasted_iota(jnp.int32, (nb, blk), 0) < i
    for h in range(A_HEADS):
        c0 = (h // 2) * LANES
        qwin = q_ref[:, c0:c0 + LANES]
        kmean_h = jnp.where(_own_lanes((nb, LANES), h), kmean_ref[:, c0:c0 + LANES], 0.0)
        gate = _dot_nt(kmean_h, qwin, precision=HIGHEST)
        sel_ref[h] = _topk_mask(gate, past, MOBA_TOPK, axis=0)
        qm_ref[h] = jnp.where(_own_lanes((blk, LANES), h), qwin * scale, 0.0).astype(BF16)
    _flash_init(m_ref, l_ref, acc_ref)
    causal = (lax.broadcasted_iota(jnp.int32, (blk, blk), 0)
              <= lax.broadcasted_iota(jnp.int32, (blk, blk), 1))

    def step(n, mask_fn):
        rows = pl.ds(pl.multiple_of(n * blk, blk), blk)
        _flash_step(
            A_HEADS,
            lambda h: mask_fn(h, _dot_nt(k16_ref[rows, (h // 2) * LANES:(h // 2 + 1) * LANES], qm_ref[h])),
            lambda h: vt16_ref[n, h * HEAD_DIM:(h + 1) * HEAD_DIM, :],
            s_ref, p_ref, m_ref, l_ref, acc_ref)

    step(i, lambda h, st: jnp.where(causal, st, NEG_BIG))

    def body(n, carry):
        step(n, lambda h, st: jnp.where(sel_ref[h, pl.ds(n, 1), :] > 0.0, st, NEG_BIG))
        return carry

    lax.fori_loop(0, i, body, 0)
    _flash_finish(o_ref, l_ref, acc_ref, A_HEADS)


def _moba_prompt(q, k, vt, batch, seq):
    blk = MOBA_BLOCK
    assert seq % blk == 0
    nq = seq // blk
    est = 2 * 2 * seq * A_W * 4 + 2 * seq * A_W * 2 + 4 * blk * A_W * 4 + 24 * blk * blk * 4
    return pl.pallas_call(
        functools.partial(_moba_prompt_body, blk=blk),
        out_shape=jax.ShapeDtypeStruct(q.shape, F32),
        grid=(batch, nq),
        in_specs=[pl.BlockSpec((blk, A_W), lambda b, i: (b * nq + i, 0)),
                  pl.BlockSpec((seq, A_W), lambda b, i: (b, 0)),
                  pl.BlockSpec((None, A_W, seq), lambda b, i: (b, 0, 0))],
        out_specs=pl.BlockSpec((blk, A_W), lambda b, i: (b * nq + i, 0)),
        scratch_shapes=[pltpu.VMEM((nq, A_W), F32), pltpu.VMEM((seq, A_W), BF16),
                        pltpu.VMEM((nq, A_W, blk), BF16), pltpu.VMEM((A_HEADS, blk, LANES), BF16),
                        pltpu.VMEM((A_HEADS, nq, blk), F32)] + _flash_scratch(A_HEADS, blk),
        compiler_params=pltpu.CompilerParams(
            dimension_semantics=("arbitrary", "arbitrary"), vmem_limit_bytes=_vmem_limit(est)),
        name="moba_prompt",
    )(q, k, vt)


FOX_AUG = 8


def _fox_prompt_body(q_ref, k_ref, vt_ref, lf_ref, o_ref, bq_ref, kaug_ref, vt16_ref, qaug_ref,
                     s_ref, p_ref, m_ref, l_ref, acc_ref, *, blk):
    i = pl.program_id(1)
    nb = k_ref.shape[0] // blk
    scale = HEAD_DIM ** -0.5
    key_i = lax.broadcasted_iota(jnp.int32, (blk, blk), 0)
    qry_i = lax.broadcasted_iota(jnp.int32, (blk, blk), 1)

    @pl.when(i == 0)
    def _():
        tri = (qry_i <= key_i).astype(F32)
        head = lax.broadcasted_iota(jnp.int32, (LANES, LANES), 0)
        lane = lax.broadcasted_iota(jnp.int32, (LANES, LANES), 1)
        place = lambda off: ((lane == head * FOX_AUG + off) & (head < B_HEADS)).astype(BF16)
        lane1 = lax.broadcasted_iota(jnp.int32, (1, LANES), 1)
        used = lane1 < B_HEADS * FOX_AUG
        ones_q = (used & (lane1 % FOX_AUG < 3)).astype(F32)
        ones_k = (used & (lane1 % FOX_AUG >= 3) & (lane1 % FOX_AUG < 6)).astype(F32)
        carry = jnp.zeros((1, LANES), F32)
        for n in range(nb):
            rows = slice(n * blk, (n + 1) * blk)
            c = _dot(tri, lf_ref[rows, :], precision=HIGHEST) + carry
            carry = c[blk - 1:blk, :]
            pieces = _bf16_pieces(c)
            bq = ones_q + sum(_dot(pieces[j], place(3 + j)) for j in range(3))
            ak = ones_k - sum(_dot(pieces[j], place(j)) for j in range(3))
            bq_ref[rows, :] = bq.astype(BF16)
            for w in range(SEG_W // LANES):
                kaug_ref[w, rows, :LANES] = k_ref[rows, w * LANES:(w + 1) * LANES].astype(BF16)
                kaug_ref[w, rows, LANES:] = ak.astype(BF16)
            vt16_ref[n] = vt_ref[:, rows].astype(BF16)

    bq_rows = bq_ref[pl.ds(pl.multiple_of(i * blk, blk), blk), :]
    group = lax.broadcasted_iota(jnp.int32, (blk, LANES), 1) // FOX_AUG
    for h in range(B_HEADS):
        c0 = (h // 2) * LANES
        qaug_ref[h, :, :LANES] = jnp.where(_own_lanes((blk, LANES), h), q_ref[:, c0:c0 + LANES] * scale,
                                           0.0).astype(BF16)
        qaug_ref[h, :, LANES:] = jnp.where(group == h, bq_rows, jnp.zeros_like(bq_rows))
    _flash_init(m_ref, l_ref, acc_ref)
    causal = key_i <= qry_i

    def step(n, diag):
        rows = pl.ds(pl.multiple_of(n * blk, blk), blk)

        def scores(h):
            st = _dot_nt(kaug_ref[h // 2, rows, :], qaug_ref[h])
            return jnp.where(causal, st, NEG_BIG) if diag else st

        _flash_step(B_HEADS, scores, lambda h: vt16_ref[n, h * HEAD_DIM:(h + 1) * HEAD_DIM, :],
                    s_ref, p_ref, m_ref, l_ref, acc_ref)

    step(i, True)

    def body(n, carry):
        step(n, False)
        return carry

    lax.fori_loop(0, i, body, 0)
    _flash_finish(o_ref, l_ref, acc_ref, B_HEADS)


def _fox_prompt(q, k, vt, lf, batch, seq, blk=256):
    assert seq % blk == 0
    nq = seq // blk
    est = (2 * seq * (SEG_W + B_W + LANES) * 4 + seq * (3 * 2 * LANES + LANES + B_W) * 2
           + 4 * blk * SEG_W * 4 + 24 * blk * blk * 4)
    return pl.pallas_call(
        functools.partial(_fox_prompt_body, blk=blk),
        out_shape=jax.ShapeDtypeStruct((batch * seq, B_W), F32),
        grid=(batch, nq),
        in_specs=[pl.BlockSpec((blk, SEG_W), lambda b, i: (b * nq + i, 0)),
                  pl.BlockSpec((seq, SEG_W), lambda b, i: (b, 0)),
                  pl.BlockSpec((None, B_W, seq), lambda b, i: (b, 0, 0)),
                  pl.BlockSpec((seq, LANES), lambda b, i: (b, 0))],
        out_specs=pl.BlockSpec((blk, B_W), lambda b, i: (b * nq + i, 0)),
        scratch_shapes=[pltpu.VMEM((seq, LANES), BF16), pltpu.VMEM((SEG_W // LANES, seq, 2 * LANES), BF16),
                        pltpu.VMEM((nq, B_W, blk), BF16), pltpu.VMEM((B_HEADS, blk, 2 * LANES), BF16)]
        + _flash_scratch(B_HEADS, blk),
        compiler_params=pltpu.CompilerParams(
            dimension_semantics=("arbitrary", "arbitrary"), vmem_limit_bytes=_vmem_limit(est)),
        name="fox_prompt",
    )(q, k, vt, lf)


def _hgrn_body(q_ref, k_ref, lf_ref, v_ref, gs_ref, gn_ref, st0_ref, o_ref, sto_ref, st_ref,
               qd_ref, kd_ref, v16_ref, dl_ref, upd_ref, *, sub, tile):
    seq = q_ref.shape[0]
    n_sub = tile // sub
    t_i = lax.broadcasted_iota(jnp.int32, (tile, tile), 0)
    s_i = lax.broadcasted_iota(jnp.int32, (tile, tile), 1)
    same_sub = t_i // sub == s_i // sub
    tri_blk = (same_sub & (s_i <= t_i)).astype(BF16)
    e_i = lax.broadcasted_iota(jnp.int32, (C_W, C_W), 0) // HEAD_DIM
    d_i = lax.broadcasted_iota(jnp.int32, (C_W, C_W), 1) // HEAD_DIM
    same_head = e_i == d_i
    ones_bd = same_head.astype(BF16)
    row_in_sub = lax.broadcasted_iota(jnp.int32, (n_sub, sub, C_W), 1)
    split = lambda a: a.reshape(n_sub, sub, C_W)

    def tile_rows(t):
        return pl.ds(pl.multiple_of(t * tile, tile), tile)

    def decay_pass(t, _):
        rows = tile_rows(t)
        lf, q, k, v = lf_ref[rows, :], q_ref[rows, :], k_ref[rows, :], v_ref[rows, :]
        cum = _sel_dot(tri_blk, lf)
        cum3, q3, k3, v3 = split(cum), split(q), split(k), split(v)
        last3 = cum3[:, sub - 1:sub, :]
        qd_ref[rows, :] = (q * jnp.exp(cum)).astype(BF16)
        kd_ref[rows, :] = (k3 * jnp.exp(last3 - cum3)).reshape(tile, C_W).astype(BF16)
        v16_ref[rows, :] = v.astype(BF16)
        dl_ref[rows, :] = jnp.broadcast_to(jnp.exp(last3), (n_sub, sub, C_W)).reshape(tile, C_W)
        o3 = jnp.zeros((n_sub, sub, C_W), F32)
        for s in range(sub):
            dec = jnp.exp(jnp.where(row_in_sub >= s, cum3 - cum3[:, s:s + 1, :], NEG_BIG))
            m = (q3 * k3[:, s:s + 1, :] * dec).reshape(tile, C_W).astype(BF16)
            w = jnp.concatenate([_dot(m[:, :MXU_W], ones_bd[:MXU_W, :MXU_W]),
                                 _dot(m[:, MXU_W:], ones_bd[MXU_W:, MXU_W:])], axis=1)
            o3 = o3 + split(w) * v3[:, s:s + 1, :]
        o_ref[rows, :] = o3.reshape(tile, C_W)
        return 0

    lax.fori_loop(0, seq // tile, decay_pass, 0)
    spread = (lax.broadcasted_iota(jnp.int32, (HEAD_DIM, C_W), 0)
              == lax.broadcasted_iota(jnp.int32, (HEAD_DIM, C_W), 1) % HEAD_DIM).astype(BF16)
    st_ref[...] = jnp.where(same_head, _dot_sel(st0_ref[...], spread), 0.0)

    group = upd_ref.shape[0]

    def chunks(g, _):
        for j in range(group):
            rows = pl.ds(pl.multiple_of((g * group + j) * sub, sub), sub)
            upd_ref[j] = jnp.where(same_head, _dot_tn(v16_ref[rows, :], kd_ref[rows, :]), 0.0)
        for j in range(group):
            c = g * group + j
            rows = pl.ds(pl.multiple_of(c * sub, sub), sub)
            st = st_ref[...]
            o_ref[rows, :] = o_ref[rows, :] + _dot_nt(qd_ref[rows, :], st.astype(BF16))
            st_ref[...] = st * dl_ref[pl.ds(c * sub, 1), :] + upd_ref[j]
        return 0

    lax.fori_loop(0, seq // (sub * group), chunks, 0)

    gn = gn_ref[...]

    def norm_pass(t, _):
        rows = tile_rows(t)
        o = o_ref[rows, :]
        sq = o * o
        hi = sq.astype(BF16)
        lo = (sq - hi.astype(F32)).astype(BF16)
        ms = (_dot(hi, ones_bd) + _dot(lo, ones_bd)) * (1.0 / HEAD_DIM)
        o_ref[rows, :] = o * lax.rsqrt(ms + EPS) * gn * gs_ref[rows, :]
        return 0

    lax.fori_loop(0, seq // tile, norm_pass, 0)
    gather = (lax.broadcasted_iota(jnp.int32, (C_W, HEAD_DIM), 0) % HEAD_DIM
              == lax.broadcasted_iota(jnp.int32, (C_W, HEAD_DIM), 1)).astype(BF16)
    sto_ref[...] = _dot_sel(st_ref[...], gather)


def _hgrn(q, k, lf, v, gs, gn, st0, batch, seq):
    sub = math.gcd(seq, 16)
    tile = min(seq, 256)
    assert seq % tile == 0 and tile % sub == 0
    row = pl.BlockSpec((seq, C_W), lambda b: (b, 0))
    st_spec = pl.BlockSpec((None, C_W, HEAD_DIM), lambda b: (b, 0, 0))
    group = math.gcd(seq // sub, 8)
    est = (2 * 6 * seq * C_W * 4 + (8 + group) * C_W * C_W * 4 + seq * C_W * (3 * 2 + 4)
           + 24 * tile * C_W * 4)
    return pl.pallas_call(
        functools.partial(_hgrn_body, sub=sub, tile=tile),
        out_shape=[jax.ShapeDtypeStruct((batch * seq, C_W), F32),
                   jax.ShapeDtypeStruct((batch, C_W, HEAD_DIM), F32)],
        grid=(batch,),
        in_specs=[row, row, row, row, row, pl.BlockSpec((1, C_W), lambda b: (0, 0)), st_spec],
        out_specs=[row, st_spec],
        scratch_shapes=[pltpu.VMEM((C_W, C_W), F32),
                        pltpu.VMEM((seq, C_W), BF16), pltpu.VMEM((seq, C_W), BF16),
                        pltpu.VMEM((seq, C_W), BF16), pltpu.VMEM((seq, C_W), F32),
                        pltpu.VMEM((group, C_W, C_W), F32)],
        compiler_params=pltpu.CompilerParams(
            dimension_semantics=("arbitrary",), vmem_limit_bytes=_vmem_limit(est)),
        name="hgrn",
    )(q, k, lf, v, gs, gn, st0)


def _block_diag_q(q, heads, scale):
    t, w = q.shape
    rep = jnp.concatenate([q] * heads, axis=0)
    r = lax.broadcasted_iota(jnp.int32, (heads * t, w), 0) // t
    c = lax.broadcasted_iota(jnp.int32, (heads * t, w), 1) // HEAD_DIM
    return jnp.where(r == c, rep * scale, 0.0)


def _collapse_heads(o, heads, t):
    r = lax.broadcasted_iota(jnp.int32, o.shape, 0) // t
    c = lax.broadcasted_iota(jnp.int32, o.shape, 1) // HEAD_DIM
    o = jnp.where(r == c, o, 0.0)
    out = o[0:t, :]
    for h in range(1, heads):
        out = out + o[h * t:(h + 1) * t, :]
    return out


def _rows_per_head(x, heads, t):
    return jnp.concatenate([jnp.broadcast_to(x[h:h + 1, :], (t, x.shape[1])) for h in range(heads)], axis=0)


def _decode_body(pt_ref, q_ref, kn_ref, vn_ref, *rest, heads, moba, pp, n_steps, t_new):
    width = heads * HEAD_DIM
    if moba:
        k_refs, v_refs, rest = rest[:pp], rest[pp:2 * pp], rest[2 * pp:]
    else:
        lfn_ref, rest = rest[0], rest[1:]
        k_refs, v_refs, lf_refs, rest = rest[:pp], rest[pp:2 * pp], rest[2 * pp:3 * pp], rest[3 * pp:]
    o_ref, qbd_ref, aux_ref, m_ref, l_ref, part_ref, s_ref, p_ref = rest
    j = pl.program_id(1)
    page = k_refs[0].shape[-1]
    rows = heads * t_new
    per_blk = MOBA_BLOCK // page
    blk_per_step = pp // per_blk
    n_blk = n_steps * blk_per_step
    scale = HEAD_DIM ** -0.5
    lane = lax.broadcasted_iota(jnp.int32, (rows, LANES), 1)

    @pl.when(j == 0)
    def _():
        qbd_ref[...] = _block_diag_q(q_ref[...], heads, 1.0)
        aux_ref[...] = jnp.zeros(aux_ref.shape, F32)
        m_ref[...] = jnp.zeros(m_ref.shape, F32)
        l_ref[...] = jnp.zeros(l_ref.shape, F32)

    qb = (qbd_ref[...] * scale).astype(BF16)
    m_all, l_all = m_ref[...], l_ref[...]
    aux = aux_ref[...]
    if moba:
        lane_w = lax.broadcasted_iota(jnp.int32, (width, LANES), 1)
    else:
        upper = (lax.broadcasted_iota(jnp.int32, (page, page), 0)
                 <= lax.broadcasted_iota(jnp.int32, (page, page), 1)).astype(F32)
    for jj in range(pp):
        kt = k_refs[jj][...].reshape(width, page)
        s = _dot(qb, kt.astype(BF16))
        if moba:
            ksum = kt if jj % per_blk == 0 else ksum + kt
            if jj % per_blk == per_blk - 1:
                kmean_n = jnp.sum(ksum, axis=1, keepdims=True) * (1.0 / MOBA_BLOCK)
                aux = aux + jnp.where(lane_w == j * blk_per_step + jj // per_blk, kmean_n, 0.0)
        else:
            lf = lf_refs[jj][...]
            c_key = _dot(lf, upper, precision=HIGHEST) + aux
            aux = aux + jnp.sum(lf, axis=1, keepdims=True)
            s = s - _rows_per_head(c_key, heads, t_new)
        s_ref[jj] = s
    for g in range(blk_per_step):
        n = j * blk_per_step + g
        pages = range(g * per_blk, (g + 1) * per_blk)
        m_n = functools.reduce(jnp.maximum, [jnp.max(s_ref[jj], axis=1, keepdims=True) for jj in pages])
        l_n = 0.0
        for jj in pages:
            p = jnp.exp(s_ref[jj] - m_n)
            l_n = l_n + jnp.sum(p, axis=1, keepdims=True)
            p_ref[jj] = p.astype(BF16)
        m_all = jnp.where(lane == n, m_n, m_all)
        l_all = jnp.where(lane == n, l_n, l_all)
    for g in range(blk_per_step):
        part_ref[j * blk_per_step + g] = sum(
            _dot_nt(p_ref[jj], v_refs[jj][...].reshape(width, page).astype(BF16))
            for jj in range(g * per_blk, (g + 1) * per_blk))
    m_ref[...] = m_all
    l_ref[...] = l_all
    aux_ref[...] = aux

    @pl.when(j == n_steps - 1)
    def _():
        qi = lax.broadcasted_iota(jnp.int32, (rows, t_new), 0) % t_new
        kj = lax.broadcasted_iota(jnp.int32, (rows, t_new), 1)
        s_own = _dot_nt(qb, kn_ref[...].astype(BF16))
        if moba:
            gate = _dot(qbd_ref[...], aux, precision=HIGHEST)
            sel = _topk_mask(gate, lane < n_blk, MOBA_TOPK, axis=1) > 0.0
        else:
            r8 = lax.broadcasted_iota(jnp.int32, (t_new, t_new), 0)
            c8 = lax.broadcasted_iota(jnp.int32, (t_new, t_new), 1)
            c_new = _dot(lfn_ref[...], (r8 <= c8).astype(F32), precision=HIGHEST) + aux
            s_own = s_own - _rows_per_head(c_new, heads, t_new)
            sel = lane < n_blk
        s_own = jnp.where(kj <= qi, s_own, NEG_BIG)
        m_own = jnp.max(s_own, axis=1, keepdims=True)
        m_tot = jnp.maximum(m_own, jnp.max(jnp.where(sel, m_all, -jnp.inf), axis=1, keepdims=True))
        p_own = jnp.exp(s_own - m_tot)
        w = jnp.where(sel, jnp.exp(m_all - m_tot), 0.0)
        l_tot = jnp.sum(p_own, axis=1, keepdims=True) + jnp.sum(w * l_all, axis=1, keepdims=True)
        acc = _dot(p_own.astype(BF16), vn_ref[...].astype(BF16))
        for n in range(n_blk):
            acc = acc + w[:, n:n + 1] * part_ref[n]
        o_ref[...] = _collapse_heads(acc / l_tot, heads, t_new)


def _decode(page_table, q, k_new, v_new, cache_kt, cache_vt, layer, n_seq, t_new, heads, lf=None, pp=16):
    n_pages = page_table.shape[1]
    page = cache_kt.shape[-1]
    width = heads * HEAD_DIM
    per_blk = MOBA_BLOCK // page
    assert MOBA_BLOCK % page == 0 and n_pages % pp == 0 and pp % per_blk == 0
    n_steps = n_pages // pp
    n_blk = n_pages // per_blk
    assert n_blk <= LANES
    rows = heads * t_new
    new = pl.BlockSpec((t_new, width), lambda b, j, pt: (b, 0))

    def pages(block, index_fn):
        return [pl.BlockSpec(block, lambda b, j, pt, jj=jj: index_fn(pt[b, j * pp + jj])) for jj in range(pp)]

    kv_pages = lambda: pages((None, None, heads, HEAD_DIM, page), lambda p: (layer, p, 0, 0, 0))
    args, specs = [q, k_new, v_new], [new, new, new]
    if lf is not None:
        args.append(lf[0])
        specs.append(pl.BlockSpec((None, heads, t_new), lambda b, j, pt: (b, 0, 0)))
    args += [cache_kt] * pp + [cache_vt] * pp
    specs += kv_pages() + kv_pages()
    if lf is not None:
        args += [lf[1]] * pp
        specs += pages((None, None, heads, page), lambda p: (layer, p, 0, 0))
    aux_shape = (heads, 1) if lf is not None else (width, LANES)
    est = 2 * 2 * pp * page * (width + 8) * 4 + n_blk * rows * width * 4 + 4 * LANES * width * 4
    grid_spec = pltpu.PrefetchScalarGridSpec(
        num_scalar_prefetch=1,
        grid=(n_seq, n_steps),
        in_specs=specs,
        out_specs=new,
        scratch_shapes=[pltpu.VMEM((rows, width), F32), pltpu.VMEM(aux_shape, F32),
                        pltpu.VMEM((rows, LANES), F32), pltpu.VMEM((rows, LANES), F32),
                        pltpu.VMEM((n_blk, rows, width), F32),
                        pltpu.VMEM((pp, rows, page), F32), pltpu.VMEM((pp, rows, page), BF16)],
    )
    return pl.pallas_call(
        functools.partial(_decode_body, heads=heads, moba=lf is None, pp=pp, n_steps=n_steps, t_new=t_new),
        out_shape=jax.ShapeDtypeStruct((n_seq * t_new, width), F32),
        grid_spec=grid_spec,
        compiler_params=pltpu.CompilerParams(
            dimension_semantics=("arbitrary", "arbitrary"), vmem_limit_bytes=_vmem_limit(est)),
        name="moba_decode" if lf is None else "fox_decode",
    )(page_table, *args)


def _pad_w_in(w):
    sizes = (A_W,) * 3 + (B_W,) * 3 + (B_HEADS,) + (C_W,) * 4
    offs = np.cumsum((0,) + sizes)
    segs = [SEG_QA, SEG_KA, SEG_VA, SEG_QF, SEG_KF, SEG_VF, SEG_FF, SEG_QC, SEG_FC, SEG_IC, SEG_GC]
    order = np.argsort(segs)
    ends = sorted(segs)[1:] + [D_IN_PAD]
    wb = w.astype(BF16)
    parts = []
    for i, end in zip(order, ends):
        parts.append(wb[:, offs[i]:offs[i] + sizes[i]])
        gap = end - segs[i] - sizes[i]
        if gap:
            parts.append(jnp.zeros((w.shape[0], gap), BF16))
    return jnp.concatenate(parts, axis=1)


def _rope_tables(pos0, t, reps):
    half = HEAD_DIM // 2
    inv = ROPE_THETA ** (-jnp.arange(half, dtype=F32) * 2.0 / HEAD_DIM)
    ang = (pos0 + jnp.arange(t, dtype=F32))[:, None] * inv[None, :]
    cos, sin, zero = jnp.cos(ang), jnp.sin(ang), jnp.zeros((t, half), F32)
    heads_per_vreg = LANES // HEAD_DIM
    tabs = (jnp.concatenate([cos, cos] * heads_per_vreg, axis=1),
            jnp.concatenate([-sin, zero] * heads_per_vreg, axis=1),
            jnp.concatenate([zero, sin] * heads_per_vreg, axis=1))
    return tuple(jnp.tile(tb, (reps, 1)) for tb in tabs)


def _pad_lanes(a, width):
    return jnp.pad(a, ((0, 0), (0, width - a.shape[1])))


def _state_rows(s):
    b, h, dk, dv = s.shape
    return jnp.swapaxes(s, 2, 3).reshape(b, h * dv, dk)


def _rows_state(st, h):
    b, _, dk = st.shape
    return jnp.swapaxes(st.reshape(b, h, -1, dk), 2, 3)


def kernel(x_prompt, x_sample, cache_moba_k, cache_moba_v, cache_fox_k, cache_fox_v, cache_fox_logf, state_hgrn, page_table, norm_ffn1_g, ffn1_w_gate, ffn1_w_up, ffn1_w_down, norm_mix_g, w_in, fox_f_bias, hgrn_lb_logits, hgrn_norm_g, w_out, norm_ffn2_g, ffn2_w_gate, ffn2_w_up, ffn2_w_down, norm_final_g):
    depth = w_in.shape[0]
    batch, seq, d = x_prompt.shape
    n_seq, t_new, _ = x_sample.shape
    page = cache_moba_k.shape[2]
    past_len = page_table.shape[1] * page

    row = lambda a: a.reshape(1, -1)
    per_layer = lambda w: [w[l].astype(BF16) for l in range(depth)]
    w1g, w1u, w1d = per_layer(ffn1_w_gate), per_layer(ffn1_w_up), per_layer(ffn1_w_down)
    w2g, w2u, w2d = per_layer(ffn2_w_gate), per_layer(ffn2_w_up), per_layer(ffn2_w_down)
    wo = per_layer(w_out)
    w_in_pad = [_pad_w_in(w_in[l]) for l in range(depth)]
    fb = [_pad_lanes(row(fox_f_bias[l]), LANES) for l in range(depth)]
    lbl = _pad_lanes(hgrn_lb_logits.astype(F32), SEG_W)
    gn = jnp.tile(hgrn_norm_g, (1, C_HEADS))
    to_pages = lambda c: jnp.transpose(c, (0, 1, 3, 4, 2))
    ckt, cvt, fkt, fvt = (to_pages(c) for c in (cache_moba_k, cache_moba_v, cache_fox_k, cache_fox_v))
    flt = jnp.swapaxes(cache_fox_logf, 2, 3)

    def trunk(x, sample):
        if sample:
            nb, t = n_seq, t_new
            tabs = _rope_tables(float(past_len), t, nb)
        else:
            nb, t = batch, seq
            tabs = _rope_tables(0.0, t, 1)
        new = []
        for l in range(depth):
            if l == 0:
                x = _ffn(x, row(norm_ffn1_g[l]), w1g[l], w1u[l], w1d[l])
            args = (x, row(norm_mix_g[l]), w_in_pad[l], tabs, fb[l], lbl, l)
            heads_last = lambda a, h: a.reshape(nb, t, h, HEAD_DIM)
            if sample:
                qa, ka, va, qf, kf, vf, lf, qc, kc, lc, ic, gc = _inproj(*args)
                oa = _decode(page_table, qa, ka, va, ckt, cvt, l, nb, t, A_HEADS)
                lf_t = jnp.swapaxes(lf[:, :B_HEADS].reshape(nb, t, B_HEADS), 1, 2)
                of = _decode(page_table, qf, kf, vf, fkt, fvt, l, nb, t, B_HEADS, lf=(lf_t, flt))
                st0 = _state_rows(state_hgrn[l].astype(F32))
                rows_kv = (heads_last(ka, A_HEADS), heads_last(va, A_HEADS),
                           heads_last(kf, B_HEADS), heads_last(vf, B_HEADS))
            else:
                qa, ka, ka_t, va_t, qf, kf, kf_t, vf_t, lf, qc, kc, lc, ic, gc = _inproj(*args, seq_t=t)
                oa = _moba_prompt(qa, ka, va_t, nb, t)
                of = _fox_prompt(qf, kf, vf_t, lf, nb, t)
                st0 = jnp.zeros((nb, C_W, HEAD_DIM), F32)
                from_t = lambda a, h: jnp.transpose(a.reshape(nb, h, HEAD_DIM, t), (0, 3, 1, 2))
                rows_kv = (from_t(ka_t, A_HEADS), from_t(va_t, A_HEADS),
                           from_t(kf_t, B_HEADS), from_t(vf_t, B_HEADS))
            oc, st = _hgrn(qc, kc, lc, ic, gc, row(gn[l]), st0, nb, t)
            mix = (oa, of, oc, wo[l][:A_W], wo[l][A_W:A_W + B_W], wo[l][A_W + B_W:])
            last = l == depth - 1
            x = _ffn(x, row(norm_ffn2_g[l]), w2g[l], w2u[l], w2d[l], mix=mix,
                     g_final=row(norm_final_g) if last else None)
            if not last:
                x = _ffn(x, row(norm_ffn1_g[l + 1]), w1g[l + 1], w1u[l + 1], w1d[l + 1])
            new.append(rows_kv + (lf[:, :B_HEADS].reshape(nb, t, B_HEADS), _rows_state(st, C_HEADS)))
        return x.reshape(nb, t, d), [jnp.stack(z) for z in zip(*new)]

    y_p, (mk_p, mv_p, fk_p, fv_p, fl_p, hs_p) = trunk(x_prompt.reshape(batch * seq, d), False)
    y_s, (mk_s, mv_s, fk_s, fv_s, fl_s, hs_s) = trunk(x_sample.reshape(n_seq * t_new, d), True)
    return (y_p, y_s, mk_p, mv_p, fk_p, fv_p, fl_p, hs_p, mk_s, mv_s, fk_s, fv_s, fl_s, hs_s)
```

```python
import functools
import math

import numpy as np
import jax
import jax.numpy as jnp
from jax import lax
from jax.experimental import pallas as pl
from jax.experimental.pallas import tpu as pltpu

F32 = jnp.float32
BF16 = jnp.bfloat16
HIGHEST = lax.Precision.HIGHEST

HEAD_DIM = 64
A_HEADS = 6
B_HEADS = 5
C_HEADS = 5
A_W = A_HEADS * HEAD_DIM
B_W = B_HEADS * HEAD_DIM
C_W = C_HEADS * HEAD_DIM
D_FF_CHUNK = 256
MOBA_BLOCK = 256
MOBA_TOPK = 3
ROPE_THETA = 10000.0
EPS = 1e-6
NEG_BIG = -1e30

LANES = 128
SUBLANES = 8
MXU_W = 256
VMEM_PHYSICAL = 64 * 1024 * 1024
VMEM_CAP = VMEM_PHYSICAL - 8 * 1024 * 1024

SEG_W = 384
SEG_QA, SEG_KA, SEG_VA, SEG_QF, SEG_KF, SEG_VF, SEG_QC, SEG_FC, SEG_IC, SEG_GC = (
    i * SEG_W for i in range(10))
SEG_FF = 10 * SEG_W
D_IN_PAD = SEG_FF + LANES


def _vmem_limit(estimate_bytes):
    return int(min(max(2 * estimate_bytes, 32 * 1024 * 1024), VMEM_CAP))


def _dot(a, b, precision=None):
    return jnp.dot(a, b, preferred_element_type=F32, precision=precision)


def _dot_nt(a, b, precision=None):
    return lax.dot_general(a, b, (((1,), (1,)), ((), ())),
                           preferred_element_type=F32, precision=precision)


def _dot_tn(a, b, precision=None):
    return lax.dot_general(a, b, (((0,), (0,)), ((), ())),
                           preferred_element_type=F32, precision=precision)


def _rms(x, g):
    ms = jnp.mean(x * x, axis=-1, keepdims=True)
    return x * lax.rsqrt(ms + EPS) * g


def _sigmoid(x):
    return 1.0 / (1.0 + jnp.exp(-x))


def _silu(x):
    return x * _sigmoid(x)


def _log_sigmoid(x):
    return jnp.minimum(x, 0.0) - jnp.log1p(jnp.exp(-jnp.abs(x)))


def _bf16_pieces(x):
    x1 = x.astype(BF16)
    r1 = x - x1.astype(F32)
    x2 = r1.astype(BF16)
    return x1, x2, (r1 - x2.astype(F32)).astype(BF16)


def _dot_sel(x, sel):
    return sum(_dot(p, sel) for p in _bf16_pieces(x))


def _sel_dot(sel, x):
    return sum(_dot(sel, p) for p in _bf16_pieces(x))


def _ffn_body(*refs, has_mix, has_final, d_ff):
    it = iter(refs)
    x_ref = next(it)
    if has_mix:
        oa_ref, of_ref, oc_ref, wo_ref = (next(it) for _ in range(4))
    g_ref, wg_ref, wu_ref, wd_ref = (next(it) for _ in range(4))
    gf_ref = next(it) if has_final else None
    out_ref = next(it)

    x = x_ref[...]
    if has_mix:
        x = (x + _dot(oa_ref[...].astype(BF16), wo_ref[:A_W, :])
             + _dot(of_ref[...].astype(BF16), wo_ref[A_W:A_W + B_W, :])
             + _dot(oc_ref[...].astype(BF16), wo_ref[A_W + B_W:, :]))
    h = _rms(x, g_ref[...]).astype(BF16)
    acc = jnp.zeros(x.shape, F32)
    for c0 in range(0, d_ff, D_FF_CHUNK):
        gate = _dot(h, wg_ref[:, c0:c0 + D_FF_CHUNK])
        up = _dot(h, wu_ref[:, c0:c0 + D_FF_CHUNK])
        act = (_silu(gate) * up).astype(BF16)
        acc = acc + _dot(act, wd_ref[c0:c0 + D_FF_CHUNK, :])
    y = x + 0.5 * acc
    if has_final:
        y = _rms(y, gf_ref[...])
    out_ref[...] = y


def _layer_spec(a, layer):
    return pl.BlockSpec((None,) + a.shape[1:], lambda *_: (layer,) + (0,) * (a.ndim - 1),
                        pipeline_mode=pl.Buffered(1))


def _ffn(x, g, wg, wu, wd, layer, mix=None, g_final=None, tm=512):
    n, d = x.shape
    d_ff = wg.shape[2]
    tm = min(tm, n)
    assert n % tm == 0 and d_ff % D_FF_CHUNK == 0
    row = lambda w: pl.BlockSpec((tm, w), lambda i: (i, 0))
    full = lambda a: pl.BlockSpec(a.shape, lambda i: (0,) * a.ndim, pipeline_mode=pl.Buffered(1))
    args, specs = [x], [row(d)]
    if mix is not None:
        oa, of, oc, wo = mix
        args += [oa, of, oc, wo]
        specs += [row(oa.shape[1]), row(of.shape[1]), row(oc.shape[1]), _layer_spec(wo, layer)]
    args += [g, wg, wu, wd]
    specs += [full(g), _layer_spec(wg, layer), _layer_spec(wu, layer), _layer_spec(wd, layer)]
    if g_final is not None:
        args.append(g_final)
        specs.append(full(g_final))
    weights = 2 * 3 * d * d_ff + (2 * d * d if mix is not None else 0)
    tiles = 2 * 2 * tm * d * 4 * (2 if mix is not None else 1) + 6 * tm * d * 4
    return pl.pallas_call(
        functools.partial(_ffn_body, has_mix=mix is not None, has_final=g_final is not None, d_ff=d_ff),
        out_shape=jax.ShapeDtypeStruct((n, d), F32),
        grid=(n // tm,),
        in_specs=specs,
        out_specs=row(d),
        compiler_params=pltpu.CompilerParams(
            dimension_semantics=("arbitrary",), vmem_limit_bytes=_vmem_limit(weights + tiles)),
        name="ffn_mix" if mix is not None else "ffn",
    )(*args)


def _inproj_body(x_ref, g_ref, w_ref, cos_ref, slo_ref, shi_ref, fb_ref, lbl_ref, *out_refs, layer, transposed,
                 n_carried):
    out_refs = out_refs[n_carried:]
    if transposed:
        (qa_ref, ka_ref, kat_ref, vat_ref, qf_ref, kf_ref, kft_ref, vft_ref, lf_ref,
         qc_ref, kc_ref, lc_ref, ic_ref, gc_ref, stage_ref) = out_refs
    else:
        (qa_ref, ka_ref, va_ref, qf_ref, kf_ref, vf_ref, lf_ref,
         qc_ref, kc_ref, lc_ref, ic_ref, gc_ref) = out_refs
    h = _rms(x_ref[...], g_ref[...]).astype(BF16)

    def proj(c0, width=SEG_W):
        return _dot(h, w_ref[:, c0:c0 + width])

    cos, slo, shi = cos_ref[...], slo_ref[...], shi_ref[...]

    def rope(p):
        parts = []
        for c in range(SEG_W // LANES):
            pc = p[:, c * LANES:(c + 1) * LANES]
            parts.append(pc * cos + pltpu.roll(pc, LANES - HEAD_DIM // 2, 1) * slo
                         + pltpu.roll(pc, HEAD_DIM // 2, 1) * shi)
        return jnp.concatenate(parts, axis=1)

    qa_ref[...] = rope(proj(SEG_QA))
    ka = rope(proj(SEG_KA))
    ka_ref[...] = ka
    if transposed:
        def transposed_tile(p):
            stage_ref[...] = p
            return stage_ref[...].T

        kat_ref[...] = ka.T
        vat_ref[...] = transposed_tile(proj(SEG_VA))
        qf_ref[...] = proj(SEG_QF)
        kf_ref[...] = proj(SEG_KF)
        kft_ref[...] = kf_ref[...].T[:B_W, :]
        vft_ref[...] = transposed_tile(proj(SEG_VF))[:B_W, :]
    else:
        va_ref[...] = proj(SEG_VA)
        qf_ref[...] = proj(SEG_QF)[:, :B_W]
        kf_ref[...] = proj(SEG_KF)[:, :B_W]
        vf_ref[...] = proj(SEG_VF)[:, :B_W]
    lf_ref[...] = _log_sigmoid(proj(SEG_FF, LANES) + fb_ref[...])

    lbl = lbl_ref[...]
    e = jnp.exp(lbl - jnp.max(lbl, axis=0, keepdims=True))
    psm = e / jnp.sum(e, axis=0, keepdims=True)
    lb = jnp.zeros((1, SEG_W), F32)
    for j in range(1, layer + 1):
        lb = lb + psm[j:j + 1, :]
    fc = proj(SEG_FC)
    lc_ref[...] = jnp.log(lb + (1.0 - lb) * _sigmoid(fc))[:, :C_W]
    kc_ref[...] = ((1.0 - lb) * _sigmoid(-fc))[:, :C_W]
    qc_ref[...] = _silu(proj(SEG_QC))[:, :C_W]
    ic_ref[...] = proj(SEG_IC)[:, :C_W]
    gc_ref[...] = _silu(proj(SEG_GC))[:, :C_W]


def _inproj(x, g, w_pad, rope_tabs, fb, lbl, layer, seq_t=None, carried=None, tm=256):
    n, d = x.shape
    depth = w_pad.shape[0]
    tm = min(tm, n)
    assert n % tm == 0 and rope_tabs[0].shape[0] % tm == 0
    row = lambda w: pl.BlockSpec((tm, w), lambda i: (i, 0))
    full = lambda a: pl.BlockSpec(a.shape, lambda i: (0,) * a.ndim, pipeline_mode=pl.Buffered(1))
    n_tab = rope_tabs[0].shape[0] // tm
    tab = pl.BlockSpec((tm, LANES), lambda i: (i % n_tab, 0))
    args = [x, g, w_pad, *rope_tabs, fb, lbl]
    in_specs = [row(d), full(g), _layer_spec(w_pad, layer), tab, tab, tab, full(fb), full(lbl)]
    aliases = {}
    if seq_t is None:
        widths = [A_W, A_W, A_W, B_W, B_W, B_W, LANES, C_W, C_W, C_W, C_W, C_W]
        shapes = [(n, w) for w in widths]
        out_specs = [row(w) for w in widths]
    else:
        assert seq_t % tm == 0
        per_seq = seq_t // tm
        col = lambda w: pl.BlockSpec((None, None, w, tm), lambda i: (layer, i // per_seq, 0, i % per_seq))
        kinds = [(row, A_W), (row, A_W), (col, A_W), (col, A_W), (row, SEG_W), (row, SEG_W), (col, B_W),
                 (col, B_W), (row, LANES), (row, C_W), (row, C_W), (row, C_W), (row, C_W), (row, C_W)]
        widths = [w for _, w in kinds]
        shapes = [(n, w) if f is row else (depth, n // seq_t, w, seq_t) for f, w in kinds]
        out_specs = [f(w) for f, w in kinds]
        if carried is not None:
            stacked = [i for i, (f, _) in enumerate(kinds) if f is col]
            aliases = {len(args) + k: i for k, i in enumerate(stacked)}
            args += list(carried)
            in_specs += [pl.BlockSpec(memory_space=pl.ANY)] * len(carried)
    est = 2 * d * D_IN_PAD + 2 * tm * 4 * (d + sum(widths) + 3 * LANES) + 8 * tm * SEG_W * 4
    return pl.pallas_call(
        functools.partial(_inproj_body, layer=layer, transposed=seq_t is not None, n_carried=len(aliases)),
        out_shape=[jax.ShapeDtypeStruct(s, F32) for s in shapes],
        grid=(n // tm,),
        in_specs=in_specs,
        out_specs=out_specs,
        scratch_shapes=[pltpu.VMEM((tm, SEG_W), F32)] if seq_t is not None else [],
        input_output_aliases=aliases,
        compiler_params=pltpu.CompilerParams(
            dimension_semantics=("arbitrary",), vmem_limit_bytes=_vmem_limit(est)),
        name="inproj",
    )(*args)


def _topk_mask(gate, allowed, topk, axis):
    n = gate.shape[axis]
    pos = lax.broadcasted_iota(jnp.int32, gate.shape, axis).astype(F32)
    g = jnp.where(allowed, gate, NEG_BIG)
    sel = jnp.zeros(gate.shape, F32)
    for _ in range(topk):
        m = jnp.max(g, axis=axis, keepdims=True)
        idx = jnp.min(jnp.where(g == m, pos, float(n)), axis=axis, keepdims=True)
        hit = pos == idx
        sel = jnp.where(hit & allowed, 1.0, sel)
        g = jnp.where(hit, -jnp.inf, g)
    return sel


def _own_lanes(shape, h):
    return lax.broadcasted_iota(jnp.int32, shape, 1) // HEAD_DIM == h % 2


def _flash_step(heads, score_fn, value_fn, s_ref, p_ref, m_ref, l_ref, acc_ref):
    for h in range(heads):
        s_ref[h] = score_fn(h)
    alphas = []
    for h in range(heads):
        m_old = m_ref[h]
        m_new = jnp.maximum(m_old, jnp.max(s_ref[h], axis=0, keepdims=True))
        alpha = jnp.exp(m_old - m_new)
        p = jnp.exp(s_ref[h] - m_new)
        l_ref[h] = alpha * l_ref[h] + jnp.sum(p, axis=0, keepdims=True)
        m_ref[h] = m_new
        p_ref[h] = p.astype(BF16)
        alphas.append(alpha)
    for h in range(heads):
        acc_ref[h] = acc_ref[h] * alphas[h] + _dot(value_fn(h), p_ref[h])


def _flash_init(m_ref, l_ref, acc_ref):
    m_ref[...] = jnp.full(m_ref.shape, -jnp.inf, F32)
    l_ref[...] = jnp.zeros(l_ref.shape, F32)
    acc_ref[...] = jnp.zeros(acc_ref.shape, F32)


def _flash_finish(o_ref, l_ref, acc_ref, heads):
    blk = acc_ref.shape[2]
    parts = [acc_ref[h] / l_ref[h] for h in range(heads)]
    parts += [jnp.zeros((HEAD_DIM, blk), F32)] * (SEG_W // HEAD_DIM - heads)
    o_ref[...] = jnp.concatenate(parts, axis=0).T[:, :o_ref.shape[1]]


def _flash_scratch(heads, blk):
    return [pltpu.VMEM((heads, blk, blk), F32), pltpu.VMEM((heads, blk, blk), BF16),
            pltpu.VMEM((heads, 1, blk), F32), pltpu.VMEM((heads, 1, blk), F32),
            pltpu.VMEM((heads, HEAD_DIM, blk), F32)]


def _moba_prompt_body(q_ref, k_ref, vt_ref, o_ref, kmean_ref, k16_ref, vt16_ref, qm_ref, sel_ref,
                      s_ref, p_ref, m_ref, l_ref, acc_ref, *, blk):
    i = pl.program_id(1)
    nb = k_ref.shape[0] // blk
    scale = HEAD_DIM ** -0.5

    @pl.when(i == 0)
    def _():
        for n in range(nb):
            kb = k_ref[n * blk:(n + 1) * blk, :]
            kmean_ref[n:n + 1, :] = jnp.sum(kb, axis=0, keepdims=True) * (1.0 / blk)
            k16_ref[n * blk:(n + 1) * blk, :] = kb.astype(BF16)
            vt16_ref[n] = vt_ref[:, n * blk:(n + 1) * blk].astype(BF16)

    past = lax.broadcasted_iota(jnp.int32, (nb, blk), 0) < i
    for h in range(A_HEADS):
        c0 = (h // 2) * LANES
        qwin = q_ref[:, c0:c0 + LANES]
        kmean_h = jnp.where(_own_lanes((nb, LANES), h), kmean_ref[:, c0:c0 + LANES], 0.0)
        gate = _dot_nt(kmean_h, qwin, precision=HIGHEST)
        sel_ref[h] = _topk_mask(gate, past, MOBA_TOPK, axis=0)
        qm_ref[h] = jnp.where(_own_lanes((blk, LANES), h), qwin * scale, 0.0).astype(BF16)
    _flash_init(m_ref, l_ref, acc_ref)
    causal = (lax.broadcasted_iota(jnp.int32, (blk, blk), 0)
              <= lax.broadcasted_iota(jnp.int32, (blk, blk), 1))

    def step(n, mask_fn):
        rows = pl.ds(pl.multiple_of(n * blk, blk), blk)
        _flash_step(
            A_HEADS,
            lambda h: mask_fn(h, _dot_nt(k16_ref[rows, (h // 2) * LANES:(h // 2 + 1) * LANES], qm_ref[h])),
            lambda h: vt16_ref[n, h * HEAD_DIM:(h + 1) * HEAD_DIM, :],
            s_ref, p_ref, m_ref, l_ref, acc_ref)

    step(i, lambda h, st: jnp.where(causal, st, NEG_BIG))

    def body(n, carry):
        step(n, lambda h, st: jnp.where(sel_ref[h, pl.ds(n, 1), :] > 0.0, st, NEG_BIG))
        return carry

    lax.fori_loop(0, i, body, 0)
    _flash_finish(o_ref, l_ref, acc_ref, A_HEADS)


def _moba_prompt(q, k, vt, layer, batch, seq):
    blk = MOBA_BLOCK
    assert seq % blk == 0
    nq = seq // blk
    est = 2 * 2 * seq * A_W * 4 + 2 * seq * A_W * 2 + 4 * blk * A_W * 4 + 24 * blk * blk * 4
    return pl.pallas_call(
        functools.partial(_moba_prompt_body, blk=blk),
        out_shape=jax.ShapeDtypeStruct(q.shape, F32),
        grid=(batch, nq),
        in_specs=[pl.BlockSpec((blk, A_W), lambda b, i: (b * nq + i, 0)),
                  pl.BlockSpec((seq, A_W), lambda b, i: (b, 0)),
                  pl.BlockSpec((None, None, A_W, seq), lambda b, i: (layer, b, 0, 0))],
        out_specs=pl.BlockSpec((blk, A_W), lambda b, i: (b * nq + i, 0)),
        scratch_shapes=[pltpu.VMEM((nq, A_W), F32), pltpu.VMEM((seq, A_W), BF16),
                        pltpu.VMEM((nq, A_W, blk), BF16), pltpu.VMEM((A_HEADS, blk, LANES), BF16),
                        pltpu.VMEM((A_HEADS, nq, blk), F32)] + _flash_scratch(A_HEADS, blk),
        compiler_params=pltpu.CompilerParams(
            dimension_semantics=("arbitrary", "arbitrary"), vmem_limit_bytes=_vmem_limit(est)),
        name="moba_prompt",
    )(q, k, vt)


FOX_AUG = 8


def _fox_prompt_body(q_ref, k_ref, vt_ref, lf_ref, o_ref, bq_ref, kaug_ref, vt16_ref, qaug_ref,
                     s_ref, p_ref, m_ref, l_ref, acc_ref, *, blk):
    i = pl.program_id(1)
    nb = k_ref.shape[0] // blk
    scale = HEAD_DIM ** -0.5
    key_i = lax.broadcasted_iota(jnp.int32, (blk, blk), 0)
    qry_i = lax.broadcasted_iota(jnp.int32, (blk, blk), 1)

    @pl.when(i == 0)
    def _():
        tri = (qry_i <= key_i).astype(F32)
        head = lax.broadcasted_iota(jnp.int32, (LANES, LANES), 0)
        lane = lax.broadcasted_iota(jnp.int32, (LANES, LANES), 1)
        place = lambda off: ((lane == head * FOX_AUG + off) & (head < B_HEADS)).astype(BF16)
        lane1 = lax.broadcasted_iota(jnp.int32, (1, LANES), 1)
        used = lane1 < B_HEADS * FOX_AUG
        ones_q = (used & (lane1 % FOX_AUG < 3)).astype(F32)
        ones_k = (used & (lane1 % FOX_AUG >= 3) & (lane1 % FOX_AUG < 6)).astype(F32)
        carry = jnp.zeros((1, LANES), F32)
        for n in range(nb):
            rows = slice(n * blk, (n + 1) * blk)
            c = _dot(tri, lf_ref[rows, :], precision=HIGHEST) + carry
            carry = c[blk - 1:blk, :]
            pieces = _bf16_pieces(c)
            bq = ones_q + sum(_dot(pieces[j], place(3 + j)) for j in range(3))
            ak = ones_k - sum(_dot(pieces[j], place(j)) for j in range(3))
            bq_ref[rows, :] = bq.astype(BF16)
            for w in range(SEG_W // LANES):
                kaug_ref[w, rows, :LANES] = k_ref[rows, w * LANES:(w + 1) * LANES].astype(BF16)
                kaug_ref[w, rows, LANES:] = ak.astype(BF16)
            vt16_ref[n] = vt_ref[:, rows].astype(BF16)

    bq_rows = bq_ref[pl.ds(pl.multiple_of(i * blk, blk), blk), :]
    group = lax.broadcasted_iota(jnp.int32, (blk, LANES), 1) // FOX_AUG
    for h in range(B_HEADS):
        c0 = (h // 2) * LANES
        qaug_ref[h, :, :LANES] = jnp.where(_own_lanes((blk, LANES), h), q_ref[:, c0:c0 + LANES] * scale,
                                           0.0).astype(BF16)
        qaug_ref[h, :, LANES:] = jnp.where(group == h, bq_rows, jnp.zeros_like(bq_rows))
    _flash_init(m_ref, l_ref, acc_ref)
    causal = key_i <= qry_i

    def step(n, diag):
        rows = pl.ds(pl.multiple_of(n * blk, blk), blk)

        def scores(h):
            st = _dot_nt(kaug_ref[h // 2, rows, :], qaug_ref[h])
            return jnp.where(causal, st, NEG_BIG) if diag else st

        _flash_step(B_HEADS, scores, lambda h: vt16_ref[n, h * HEAD_DIM:(h + 1) * HEAD_DIM, :],
                    s_ref, p_ref, m_ref, l_ref, acc_ref)

    step(i, True)

    def body(n, carry):
        step(n, False)
        return carry

    lax.fori_loop(0, i, body, 0)
    _flash_finish(o_ref, l_ref, acc_ref, B_HEADS)


def _fox_prompt(q, k, vt, lf, layer, batch, seq, blk=256):
    assert seq % blk == 0
    nq = seq // blk
    est = (2 * seq * (SEG_W + B_W + LANES) * 4 + seq * (3 * 2 * LANES + LANES + B_W) * 2
           + 4 * blk * SEG_W * 4 + 24 * blk * blk * 4)
    return pl.pallas_call(
        functools.partial(_fox_prompt_body, blk=blk),
        out_shape=jax.ShapeDtypeStruct((batch * seq, B_W), F32),
        grid=(batch, nq),
        in_specs=[pl.BlockSpec((blk, SEG_W), lambda b, i: (b * nq + i, 0)),
                  pl.BlockSpec((seq, SEG_W), lambda b, i: (b, 0)),
                  pl.BlockSpec((None, None, B_W, seq), lambda b, i: (layer, b, 0, 0)),
                  pl.BlockSpec((seq, LANES), lambda b, i: (b, 0))],
        out_specs=pl.BlockSpec((blk, B_W), lambda b, i: (b * nq + i, 0)),
        scratch_shapes=[pltpu.VMEM((seq, LANES), BF16), pltpu.VMEM((SEG_W // LANES, seq, 2 * LANES), BF16),
                        pltpu.VMEM((nq, B_W, blk), BF16), pltpu.VMEM((B_HEADS, blk, 2 * LANES), BF16)]
        + _flash_scratch(B_HEADS, blk),
        compiler_params=pltpu.CompilerParams(
            dimension_semantics=("arbitrary", "arbitrary"), vmem_limit_bytes=_vmem_limit(est)),
        name="fox_prompt",
    )(q, k, vt, lf)


def _hgrn_body(q_ref, k_ref, lf_ref, v_ref, gs_ref, gn_ref, st0_ref, o_ref, sto_ref, st_ref,
               qd_ref, kd_ref, v16_ref, dl_ref, upd_ref, *, sub, tile):
    seq = q_ref.shape[0]
    n_sub = tile // sub
    t_i = lax.broadcasted_iota(jnp.int32, (tile, tile), 0)
    s_i = lax.broadcasted_iota(jnp.int32, (tile, tile), 1)
    same_sub = t_i // sub == s_i // sub
    tri_blk = (same_sub & (s_i <= t_i)).astype(BF16)
    e_i = lax.broadcasted_iota(jnp.int32, (C_W, C_W), 0) // HEAD_DIM
    d_i = lax.broadcasted_iota(jnp.int32, (C_W, C_W), 1) // HEAD_DIM
    same_head = e_i == d_i
    ones_bd = same_head.astype(BF16)
    row_in_sub = lax.broadcasted_iota(jnp.int32, (n_sub, sub, C_W), 1)
    split = lambda a: a.reshape(n_sub, sub, C_W)

    def tile_rows(t):
        return pl.ds(pl.multiple_of(t * tile, tile), tile)

    def decay_pass(t, _):
        rows = tile_rows(t)
        lf, q, k, v = lf_ref[rows, :], q_ref[rows, :], k_ref[rows, :], v_ref[rows, :]
        cum = _sel_dot(tri_blk, lf)
        cum3, q3, k3, v3 = split(cum), split(q), split(k), split(v)
        last3 = cum3[:, sub - 1:sub, :]
        qd_ref[rows, :] = (q * jnp.exp(cum)).astype(BF16)
        kd_ref[rows, :] = (k3 * jnp.exp(last3 - cum3)).reshape(tile, C_W).astype(BF16)
        v16_ref[rows, :] = v.astype(BF16)
        dl_ref[rows, :] = jnp.broadcast_to(jnp.exp(last3), (n_sub, sub, C_W)).reshape(tile, C_W)
        o3 = jnp.zeros((n_sub, sub, C_W), F32)
        for s in range(sub):
            dec = jnp.exp(jnp.where(row_in_sub >= s, cum3 - cum3[:, s:s + 1, :], NEG_BIG))
            m = (q3 * k3[:, s:s + 1, :] * dec).reshape(tile, C_W).astype(BF16)
            w = jnp.concatenate([_dot(m[:, :MXU_W], ones_bd[:MXU_W, :MXU_W]),
                                 _dot(m[:, MXU_W:], ones_bd[MXU_W:, MXU_W:])], axis=1)
            o3 = o3 + split(w) * v3[:, s:s + 1, :]
        o_ref[rows, :] = o3.reshape(tile, C_W)
        return 0

    lax.fori_loop(0, seq // tile, decay_pass, 0)
    spread = (lax.broadcasted_iota(jnp.int32, (HEAD_DIM, C_W), 0)
              == lax.broadcasted_iota(jnp.int32, (HEAD_DIM, C_W), 1) % HEAD_DIM).astype(BF16)
    st_ref[...] = jnp.where(same_head, _dot_sel(st0_ref[...], spread), 0.0)

    group = upd_ref.shape[0]

    def chunks(g, _):
        for j in range(group):
            rows = pl.ds(pl.multiple_of((g * group + j) * sub, sub), sub)
            upd_ref[j] = jnp.where(same_head, _dot_tn(v16_ref[rows, :], kd_ref[rows, :]), 0.0)
        for j in range(group):
            c = g * group + j
            rows = pl.ds(pl.multiple_of(c * sub, sub), sub)
            st = st_ref[...]
            o_ref[rows, :] = o_ref[rows, :] + _dot_nt(qd_ref[rows, :], st.astype(BF16))
            st_ref[...] = st * dl_ref[pl.ds(c * sub, 1), :] + upd_ref[j]
        return 0

    lax.fori_loop(0, seq // (sub * group), chunks, 0)

    gn = gn_ref[...]

    def norm_pass(t, _):
        rows = tile_rows(t)
        o = o_ref[rows, :]
        sq = o * o
        hi = sq.astype(BF16)
        lo = (sq - hi.astype(F32)).astype(BF16)
        ms = (_dot(hi, ones_bd) + _dot(lo, ones_bd)) * (1.0 / HEAD_DIM)
        o_ref[rows, :] = o * lax.rsqrt(ms + EPS) * gn * gs_ref[rows, :]
        return 0

    lax.fori_loop(0, seq // tile, norm_pass, 0)
    gather = (lax.broadcasted_iota(jnp.int32, (C_W, HEAD_DIM), 0) % HEAD_DIM
              == lax.broadcasted_iota(jnp.int32, (C_W, HEAD_DIM), 1)).astype(BF16)
    sto_ref[...] = _dot_sel(st_ref[...], gather)


def _hgrn(q, k, lf, v, gs, gn, st0, batch, seq):
    sub = math.gcd(seq, 16)
    tile = min(seq, 256)
    assert seq % tile == 0 and tile % sub == 0
    row = pl.BlockSpec((seq, C_W), lambda b: (b, 0))
    st_spec = pl.BlockSpec((None, C_W, HEAD_DIM), lambda b: (b, 0, 0))
    group = math.gcd(seq // sub, 8)
    est = (2 * 6 * seq * C_W * 4 + (8 + group) * C_W * C_W * 4 + seq * C_W * (3 * 2 + 4)
           + 24 * tile * C_W * 4)
    return pl.pallas_call(
        functools.partial(_hgrn_body, sub=sub, tile=tile),
        out_shape=[jax.ShapeDtypeStruct((batch * seq, C_W), F32),
                   jax.ShapeDtypeStruct((batch, C_W, HEAD_DIM), F32)],
        grid=(batch,),
        in_specs=[row, row, row, row, row, pl.BlockSpec((1, C_W), lambda b: (0, 0)), st_spec],
        out_specs=[row, st_spec],
        scratch_shapes=[pltpu.VMEM((C_W, C_W), F32),
                        pltpu.VMEM((seq, C_W), BF16), pltpu.VMEM((seq, C_W), BF16),
                        pltpu.VMEM((seq, C_W), BF16), pltpu.VMEM((seq, C_W), F32),
                        pltpu.VMEM((group, C_W, C_W), F32)],
        compiler_params=pltpu.CompilerParams(
            dimension_semantics=("arbitrary",), vmem_limit_bytes=_vmem_limit(est)),
        name="hgrn",
    )(q, k, lf, v, gs, gn, st0)


def _block_diag_q(q, heads, scale):
    t, w = q.shape
    rep = jnp.concatenate([q] * heads, axis=0)
    r = lax.broadcasted_iota(jnp.int32, (heads * t, w), 0) // t
    c = lax.broadcasted_iota(jnp.int32, (heads * t, w), 1) // HEAD_DIM
    return jnp.where(r == c, rep * scale, 0.0)


def _collapse_heads(o, heads, t):
    r = lax.broadcasted_iota(jnp.int32, o.shape, 0) // t
    c = lax.broadcasted_iota(jnp.int32, o.shape, 1) // HEAD_DIM
    o = jnp.where(r == c, o, 0.0)
    out = o[0:t, :]
    for h in range(1, heads):
        out = out + o[h * t:(h + 1) * t, :]
    return out


def _rows_per_head(x, heads, t):
    return jnp.concatenate([jnp.broadcast_to(x[h:h + 1, :], (t, x.shape[1])) for h in range(heads)], axis=0)


def _decode_body(pt_ref, q_ref, kn_ref, vn_ref, *rest, heads, moba, pp, n_steps, t_new):
    width = heads * HEAD_DIM
    if moba:
        k_refs, v_refs, rest = rest[:pp], rest[pp:2 * pp], rest[2 * pp:]
    else:
        lfn_ref, rest = rest[0], rest[1:]
        k_refs, v_refs, lf_refs, rest = rest[:pp], rest[pp:2 * pp], rest[2 * pp:3 * pp], rest[3 * pp:]
    o_ref, qbd_ref, aux_ref, m_ref, l_ref, part_ref, s_ref, p_ref = rest
    j = pl.program_id(1)
    page = k_refs[0].shape[-1]
    rows = heads * t_new
    per_blk = MOBA_BLOCK // page
    blk_per_step = pp // per_blk
    n_blk = n_steps * blk_per_step
    scale = HEAD_DIM ** -0.5
    lane = lax.broadcasted_iota(jnp.int32, (rows, LANES), 1)

    @pl.when(j == 0)
    def _():
        qbd_ref[...] = _block_diag_q(q_ref[...], heads, 1.0)
        aux_ref[...] = jnp.zeros(aux_ref.shape, F32)
        m_ref[...] = jnp.zeros(m_ref.shape, F32)
        l_ref[...] = jnp.zeros(l_ref.shape, F32)

    qb = (qbd_ref[...] * scale).astype(BF16)
    m_all, l_all = m_ref[...], l_ref[...]
    aux = aux_ref[...]
    if moba:
        lane_w = lax.broadcasted_iota(jnp.int32, (width, LANES), 1)
    else:
        upper = (lax.broadcasted_iota(jnp.int32, (page, page), 0)
                 <= lax.broadcasted_iota(jnp.int32, (page, page), 1)).astype(F32)
    for jj in range(pp):
        kt = k_refs[jj][...].reshape(width, page)
        s = _dot(qb, kt.astype(BF16))
        if moba:
            ksum = kt if jj % per_blk == 0 else ksum + kt
            if jj % per_blk == per_blk - 1:
                kmean_n = jnp.sum(ksum, axis=1, keepdims=True) * (1.0 / MOBA_BLOCK)
                aux = aux + jnp.where(lane_w == j * blk_per_step + jj // per_blk, kmean_n, 0.0)
        else:
            lf = lf_refs[jj][...]
            c_key = _dot(lf, upper, precision=HIGHEST) + aux
            aux = aux + jnp.sum(lf, axis=1, keepdims=True)
            s = s - _rows_per_head(c_key, heads, t_new)
        s_ref[jj] = s
    for g in range(blk_per_step):
        n = j * blk_per_step + g
        pages = range(g * per_blk, (g + 1) * per_blk)
        m_n = functools.reduce(jnp.maximum, [jnp.max(s_ref[jj], axis=1, keepdims=True) for jj in pages])
        l_n = 0.0
        for jj in pages:
            p = jnp.exp(s_ref[jj] - m_n)
            l_n = l_n + jnp.sum(p, axis=1, keepdims=True)
            p_ref[jj] = p.astype(BF16)
        m_all = jnp.where(lane == n, m_n, m_all)
        l_all = jnp.where(lane == n, l_n, l_all)
    for g in range(blk_per_step):
        part_ref[j * blk_per_step + g] = sum(
            _dot_nt(p_ref[jj], v_refs[jj][...].reshape(width, page).astype(BF16))
            for jj in range(g * per_blk, (g + 1) * per_blk))
    m_ref[...] = m_all
    l_ref[...] = l_all
    aux_ref[...] = aux

    @pl.when(j == n_steps - 1)
    def _():
        qi = lax.broadcasted_iota(jnp.int32, (rows, t_new), 0) % t_new
        kj = lax.broadcasted_iota(jnp.int32, (rows, t_new), 1)
        s_own = _dot_nt(qb, kn_ref[...].astype(BF16))
        if moba:
            gate = _dot(qbd_ref[...], aux, precision=HIGHEST)
            sel = _topk_mask(gate, lane < n_blk, MOBA_TOPK, axis=1) > 0.0
        else:
            r8 = lax.broadcasted_iota(jnp.int32, (t_new, t_new), 0)
            c8 = lax.broadcasted_iota(jnp.int32, (t_new, t_new), 1)
            c_new = _dot(lfn_ref[...], (r8 <= c8).astype(F32), precision=HIGHEST) + aux
            s_own = s_own - _rows_per_head(c_new, heads, t_new)
            sel = lane < n_blk
        s_own = jnp.where(kj <= qi, s_own, NEG_BIG)
        m_own = jnp.max(s_own, axis=1, keepdims=True)
        m_tot = jnp.maximum(m_own, jnp.max(jnp.where(sel, m_all, -jnp.inf), axis=1, keepdims=True))
        p_own = jnp.exp(s_own - m_tot)
        w = jnp.where(sel, jnp.exp(m_all - m_tot), 0.0)
        l_tot = jnp.sum(p_own, axis=1, keepdims=True) + jnp.sum(w * l_all, axis=1, keepdims=True)
        acc = _dot(p_own.astype(BF16), vn_ref[...].astype(BF16))
        for n in range(n_blk):
            acc = acc + w[:, n:n + 1] * part_ref[n]
        o_ref[...] = _collapse_heads(acc / l_tot, heads, t_new)


def _decode(page_table, q, k_new, v_new, cache_kt, cache_vt, layer, n_seq, t_new, heads, lf=None, pages_per_step=32):
    n_pages = page_table.shape[1]
    pp = min(pages_per_step, n_pages)
    page = cache_kt.shape[-1]
    width = heads * HEAD_DIM
    per_blk = MOBA_BLOCK // page
    assert MOBA_BLOCK % page == 0 and n_pages % pp == 0 and pp % per_blk == 0
    n_steps = n_pages // pp
    n_blk = n_pages // per_blk
    assert n_blk <= LANES
    rows = heads * t_new
    new = pl.BlockSpec((t_new, width), lambda b, j, pt: (b, 0))

    def pages(block, index_fn):
        return [pl.BlockSpec(block, lambda b, j, pt, jj=jj: index_fn(pt[b, j * pp + jj])) for jj in range(pp)]

    kv_pages = lambda: pages((None, None, heads, HEAD_DIM, page), lambda p: (layer, p, 0, 0, 0))
    args, specs = [q, k_new, v_new], [new, new, new]
    if lf is not None:
        args.append(lf[0])
        specs.append(pl.BlockSpec((None, heads, t_new), lambda b, j, pt: (b, 0, 0)))
    args += [cache_kt] * pp + [cache_vt] * pp
    specs += kv_pages() + kv_pages()
    if lf is not None:
        args += [lf[1]] * pp
        specs += pages((None, None, heads, page), lambda p: (layer, p, 0, 0))
    aux_shape = (heads, 1) if lf is not None else (width, LANES)
    est = 2 * 2 * pp * page * (width + 8) * 4 + n_blk * rows * width * 4 + 4 * LANES * width * 4
    grid_spec = pltpu.PrefetchScalarGridSpec(
        num_scalar_prefetch=1,
        grid=(n_seq, n_steps),
        in_specs=specs,
        out_specs=new,
        scratch_shapes=[pltpu.VMEM((rows, width), F32), pltpu.VMEM(aux_shape, F32),
                        pltpu.VMEM((rows, LANES), F32), pltpu.VMEM((rows, LANES), F32),
                        pltpu.VMEM((n_blk, rows, width), F32),
                        pltpu.VMEM((pp, rows, page), F32), pltpu.VMEM((pp, rows, page), BF16)],
    )
    return pl.pallas_call(
        functools.partial(_decode_body, heads=heads, moba=lf is None, pp=pp, n_steps=n_steps, t_new=t_new),
        out_shape=jax.ShapeDtypeStruct((n_seq * t_new, width), F32),
        grid_spec=grid_spec,
        compiler_params=pltpu.CompilerParams(
            dimension_semantics=("arbitrary", "arbitrary"), vmem_limit_bytes=_vmem_limit(est)),
        name="moba_decode" if lf is None else "fox_decode",
    )(page_table, *args)


def _pad_w_in(w):
    sizes = (A_W,) * 3 + (B_W,) * 3 + (B_HEADS,) + (C_W,) * 4
    offs = np.cumsum((0,) + sizes)
    segs = [SEG_QA, SEG_KA, SEG_VA, SEG_QF, SEG_KF, SEG_VF, SEG_FF, SEG_QC, SEG_FC, SEG_IC, SEG_GC]
    order = np.argsort(segs)
    ends = sorted(segs)[1:] + [D_IN_PAD]
    wb = w.astype(BF16)
    parts = []
    for i, end in zip(order, ends):
        parts.append(wb[:, :, offs[i]:offs[i] + sizes[i]])
        gap = end - segs[i] - sizes[i]
        if gap:
            parts.append(jnp.zeros(w.shape[:2] + (gap,), BF16))
    return jnp.concatenate(parts, axis=2)


def _rope_tables(pos0, t, reps):
    half = HEAD_DIM // 2
    inv = ROPE_THETA ** (-jnp.arange(half, dtype=F32) * 2.0 / HEAD_DIM)
    ang = (pos0 + jnp.arange(t, dtype=F32))[:, None] * inv[None, :]
    cos, sin, zero = jnp.cos(ang), jnp.sin(ang), jnp.zeros((t, half), F32)
    heads_per_vreg = LANES // HEAD_DIM
    tabs = (jnp.concatenate([cos, cos] * heads_per_vreg, axis=1),
            jnp.concatenate([-sin, zero] * heads_per_vreg, axis=1),
            jnp.concatenate([zero, sin] * heads_per_vreg, axis=1))
    return tuple(jnp.tile(tb, (reps, 1)) for tb in tabs)


def _pad_lanes(a, width):
    return jnp.pad(a, ((0, 0), (0, width - a.shape[1])))


def _state_rows(s):
    b, h, dk, dv = s.shape
    return jnp.swapaxes(s, 2, 3).reshape(b, h * dv, dk)


def _rows_state(st, h):
    b, _, dk = st.shape
    return jnp.swapaxes(st.reshape(b, h, -1, dk), 2, 3)


def kernel(x_prompt, x_sample, cache_moba_k, cache_moba_v, cache_fox_k, cache_fox_v, cache_fox_logf, state_hgrn, page_table, norm_ffn1_g, ffn1_w_gate, ffn1_w_up, ffn1_w_down, norm_mix_g, w_in, fox_f_bias, hgrn_lb_logits, hgrn_norm_g, w_out, norm_ffn2_g, ffn2_w_gate, ffn2_w_up, ffn2_w_down, norm_final_g):
    depth = w_in.shape[0]
    batch, seq, d = x_prompt.shape
    n_seq, t_new, _ = x_sample.shape
    page = cache_moba_k.shape[2]
    past_len = page_table.shape[1] * page

    row = lambda a: a.reshape(1, -1)
    bf = lambda w: w.astype(BF16)
    ffn1 = lambda l: (row(norm_ffn1_g[l]), w1g, w1u, w1d, l)
    w1g, w1u, w1d = bf(ffn1_w_gate), bf(ffn1_w_up), bf(ffn1_w_down)
    w2g, w2u, w2d = bf(ffn2_w_gate), bf(ffn2_w_up), bf(ffn2_w_down)
    wo = bf(w_out)
    w_in_pad = _pad_w_in(w_in)
    fb = [_pad_lanes(row(fox_f_bias[l]), LANES) for l in range(depth)]
    lbl = _pad_lanes(hgrn_lb_logits.astype(F32), SEG_W)
    gn = jnp.tile(hgrn_norm_g, (1, C_HEADS))
    to_pages = lambda c: jnp.transpose(c, (0, 1, 3, 4, 2))
    ckt, cvt, fkt, fvt = (to_pages(c) for c in (cache_moba_k, cache_moba_v, cache_fox_k, cache_fox_v))
    flt = jnp.swapaxes(cache_fox_logf, 2, 3)

    def trunk(x, sample):
        if sample:
            nb, t = n_seq, t_new
            tabs = _rope_tables(float(past_len), t, nb)
        else:
            nb, t = batch, seq
            tabs = _rope_tables(0.0, t, 1)
        new, kv_t = [], None
        for l in range(depth):
            if l == 0:
                x = _ffn(x, *ffn1(l))
            args = (x, row(norm_mix_g[l]), w_in_pad, tabs, fb[l], lbl, l)
            heads_last = lambda a, h: a.reshape(nb, t, h, HEAD_DIM)
            if sample:
                qa, ka, va, qf, kf, vf, lf, qc, kc, lc, ic, gc = _inproj(*args)
                oa = _decode(page_table, qa, ka, va, ckt, cvt, l, nb, t, A_HEADS)
                lf_t = jnp.swapaxes(lf[:, :B_HEADS].reshape(nb, t, B_HEADS), 1, 2)
                of = _decode(page_table, qf, kf, vf, fkt, fvt, l, nb, t, B_HEADS, lf=(lf_t, flt))
                st0 = _state_rows(state_hgrn[l].astype(F32))
                rows_kv = (heads_last(ka, A_HEADS), heads_last(va, A_HEADS),
                           heads_last(kf, B_HEADS), heads_last(vf, B_HEADS))
            else:
                qa, ka, ka_t, va_t, qf, kf, kf_t, vf_t, lf, qc, kc, lc, ic, gc = _inproj(
                    *args, seq_t=t, carried=kv_t)
                kv_t = (ka_t, va_t, kf_t, vf_t)
                oa = _moba_prompt(qa, ka, va_t, l, nb, t)
                of = _fox_prompt(qf, kf, vf_t, lf, l, nb, t)
                st0 = jnp.zeros((nb, C_W, HEAD_DIM), F32)
                rows_kv = ()
            oc, st = _hgrn(qc, kc, lc, ic, gc, row(gn[l]), st0, nb, t)
            last = l == depth - 1
            x = _ffn(x, row(norm_ffn2_g[l]), w2g, w2u, w2d, l, mix=(oa, of, oc, wo),
                     g_final=row(norm_final_g) if last else None)
            if not last:
                x = _ffn(x, *ffn1(l + 1))
            new.append(rows_kv + (lf[:, :B_HEADS].reshape(nb, t, B_HEADS), _rows_state(st, C_HEADS)))
        outs = [jnp.stack(z) for z in zip(*new)]
        if not sample:
            heads_of = (A_HEADS, A_HEADS, B_HEADS, B_HEADS)
            outs = [jnp.transpose(a.reshape(depth, nb, h, HEAD_DIM, t), (0, 1, 4, 2, 3))
                    for a, h in zip(kv_t, heads_of)] + outs
        return x.reshape(nb, t, d), outs

    y_p, (mk_p, mv_p, fk_p, fv_p, fl_p, hs_p) = trunk(x_prompt.reshape(batch * seq, d), False)
    y_s, (mk_s, mv_s, fk_s, fv_s, fl_s, hs_s) = trunk(x_sample.reshape(n_seq * t_new, d), True)
    return (y_p, y_s, mk_p, mv_p, fk_p, fv_p, fl_p, hs_p, mk_s, mv_s, fk_s, fv_s, fl_s, hs_s)
```

```python
import functools
import math

import numpy as np
import jax
import jax.numpy as jnp
from jax import lax
from jax.experimental import pallas as pl
from jax.experimental.pallas import tpu as pltpu

F32 = jnp.float32
BF16 = jnp.bfloat16
HIGHEST = lax.Precision.HIGHEST

HEAD_DIM = 64
A_HEADS = 6
B_HEADS = 5
C_HEADS = 5
A_W = A_HEADS * HEAD_DIM
B_W = B_HEADS * HEAD_DIM
C_W = C_HEADS * HEAD_DIM
D_FF_CHUNK = 256
MOBA_BLOCK = 256
MOBA_TOPK = 3
ROPE_THETA = 10000.0
EPS = 1e-6
NEG_BIG = -1e30

LANES = 128
SUBLANES = 8
MXU_W = 256
VMEM_PHYSICAL = 64 * 1024 * 1024
VMEM_CAP = VMEM_PHYSICAL - 8 * 1024 * 1024

SEG_W = 384
SEG_QA, SEG_KA, SEG_VA, SEG_QF, SEG_KF, SEG_VF, SEG_QC, SEG_FC, SEG_IC, SEG_GC = (
    i * SEG_W for i in range(10))
SEG_FF = 10 * SEG_W
D_IN_PAD = SEG_FF + LANES


def _vmem_limit(estimate_bytes):
    return int(min(max(2 * estimate_bytes, 32 * 1024 * 1024), VMEM_CAP))


def _dot(a, b, precision=None):
    return jnp.dot(a, b, preferred_element_type=F32, precision=precision)


def _dot_nt(a, b, precision=None):
    return lax.dot_general(a, b, (((1,), (1,)), ((), ())),
                           preferred_element_type=F32, precision=precision)


def _dot_tn(a, b, precision=None):
    return lax.dot_general(a, b, (((0,), (0,)), ((), ())),
                           preferred_element_type=F32, precision=precision)


def _rms(x, g):
    ms = jnp.mean(x * x, axis=-1, keepdims=True)
    return x * lax.rsqrt(ms + EPS) * g


def _sigmoid(x):
    return 1.0 / (1.0 + jnp.exp(-x))


def _silu(x):
    return x * _sigmoid(x)


def _log_sigmoid(x):
    return jnp.minimum(x, 0.0) - jnp.log1p(jnp.exp(-jnp.abs(x)))


def _bf16_pieces(x):
    x1 = x.astype(BF16)
    r1 = x - x1.astype(F32)
    x2 = r1.astype(BF16)
    return x1, x2, (r1 - x2.astype(F32)).astype(BF16)


def _dot_sel(x, sel):
    return sum(_dot(p, sel) for p in _bf16_pieces(x))


def _sel_dot(sel, x):
    return sum(_dot(sel, p) for p in _bf16_pieces(x))


def _ffn_body(*refs, has_mix, has_final, d_ff):
    it = iter(refs)
    x_ref = next(it)
    if has_mix:
        oa_ref, of_ref, oc_ref, wo_ref = (next(it) for _ in range(4))
    g_ref, wg_ref, wu_ref, wd_ref = (next(it) for _ in range(4))
    gf_ref = next(it) if has_final else None
    out_ref = next(it)

    x = x_ref[...]
    if has_mix:
        x = (x + _dot(oa_ref[...].astype(BF16), wo_ref[:A_W, :])
             + _dot(of_ref[...].astype(BF16), wo_ref[A_W:A_W + B_W, :])
             + _dot(oc_ref[...].astype(BF16), wo_ref[A_W + B_W:, :]))
    h = _rms(x, g_ref[...]).astype(BF16)
    acc = jnp.zeros(x.shape, F32)
    for c0 in range(0, d_ff, D_FF_CHUNK):
        gate = _dot(h, wg_ref[:, c0:c0 + D_FF_CHUNK])
        up = _dot(h, wu_ref[:, c0:c0 + D_FF_CHUNK])
        act = (_silu(gate) * up).astype(BF16)
        acc = acc + _dot(act, wd_ref[c0:c0 + D_FF_CHUNK, :])
    y = x + 0.5 * acc
    if has_final:
        y = _rms(y, gf_ref[...])
    out_ref[...] = y


def _layer_spec(a, layer):
    return pl.BlockSpec((None,) + a.shape[1:], lambda *_: (layer,) + (0,) * (a.ndim - 1),
                        pipeline_mode=pl.Buffered(1))


def _ffn(x, g, wg, wu, wd, layer, mix=None, g_final=None, tm=512):
    n, d = x.shape
    d_ff = wg.shape[2]
    tm = min(tm, n)
    assert n % tm == 0 and d_ff % D_FF_CHUNK == 0
    row = lambda w: pl.BlockSpec((tm, w), lambda i: (i, 0))
    full = lambda a: pl.BlockSpec(a.shape, lambda i: (0,) * a.ndim, pipeline_mode=pl.Buffered(1))
    args, specs = [x], [row(d)]
    if mix is not None:
        oa, of, oc, wo = mix
        args += [oa, of, oc, wo]
        specs += [row(oa.shape[1]), row(of.shape[1]), row(oc.shape[1]), _layer_spec(wo, layer)]
    args += [g, wg, wu, wd]
    specs += [full(g), _layer_spec(wg, layer), _layer_spec(wu, layer), _layer_spec(wd, layer)]
    if g_final is not None:
        args.append(g_final)
        specs.append(full(g_final))
    weights = 2 * 3 * d * d_ff + (2 * d * d if mix is not None else 0)
    tiles = 2 * 2 * tm * d * 4 * (2 if mix is not None else 1) + 6 * tm * d * 4
    return pl.pallas_call(
        functools.partial(_ffn_body, has_mix=mix is not None, has_final=g_final is not None, d_ff=d_ff),
        out_shape=jax.ShapeDtypeStruct((n, d), F32),
        grid=(n // tm,),
        in_specs=specs,
        out_specs=row(d),
        compiler_params=pltpu.CompilerParams(
            dimension_semantics=("arbitrary",), vmem_limit_bytes=_vmem_limit(weights + tiles)),
        name="ffn_mix" if mix is not None else "ffn",
    )(*args)


def _inproj_body(x_ref, g_ref, w_ref, cos_ref, slo_ref, shi_ref, fb_ref, lbl_ref, *out_refs, layer, transposed,
                 n_carried):
    out_refs = out_refs[n_carried:]
    if transposed:
        (qa_ref, ka_ref, kat_ref, vat_ref, qf_ref, kf_ref, kft_ref, vft_ref, lf_ref,
         qc_ref, kc_ref, lc_ref, ic_ref, gc_ref, stage_ref) = out_refs
    else:
        (qa_ref, ka_ref, va_ref, qf_ref, kf_ref, vf_ref, lf_ref,
         qc_ref, kc_ref, lc_ref, ic_ref, gc_ref) = out_refs
    h = _rms(x_ref[...], g_ref[...]).astype(BF16)

    def proj(c0, width=SEG_W):
        return _dot(h, w_ref[:, c0:c0 + width])

    cos, slo, shi = cos_ref[...], slo_ref[...], shi_ref[...]

    def rope(p):
        parts = []
        for c in range(SEG_W // LANES):
            pc = p[:, c * LANES:(c + 1) * LANES]
            parts.append(pc * cos + pltpu.roll(pc, LANES - HEAD_DIM // 2, 1) * slo
                         + pltpu.roll(pc, HEAD_DIM // 2, 1) * shi)
        return jnp.concatenate(parts, axis=1)

    qa_ref[...] = rope(proj(SEG_QA))
    ka = rope(proj(SEG_KA))
    ka_ref[...] = ka
    if transposed:
        def transposed_tile(p):
            stage_ref[...] = p
            return stage_ref[...].T

        kat_ref[...] = ka.T
        vat_ref[...] = transposed_tile(proj(SEG_VA))
        qf_ref[...] = proj(SEG_QF)
        kf_ref[...] = proj(SEG_KF)
        kft_ref[...] = kf_ref[...].T[:B_W, :]
        vft_ref[...] = transposed_tile(proj(SEG_VF))[:B_W, :]
    else:
        va_ref[...] = proj(SEG_VA)
        qf_ref[...] = proj(SEG_QF)[:, :B_W]
        kf_ref[...] = proj(SEG_KF)[:, :B_W]
        vf_ref[...] = proj(SEG_VF)[:, :B_W]
    lf_ref[...] = _log_sigmoid(proj(SEG_FF, LANES) + fb_ref[...])

    lbl = lbl_ref[...]
    e = jnp.exp(lbl - jnp.max(lbl, axis=0, keepdims=True))
    psm = e / jnp.sum(e, axis=0, keepdims=True)
    lb = jnp.zeros((1, SEG_W), F32)
    for j in range(1, layer + 1):
        lb = lb + psm[j:j + 1, :]
    fc = proj(SEG_FC)
    lc_ref[...] = jnp.log(lb + (1.0 - lb) * _sigmoid(fc))[:, :C_W]
    kc_ref[...] = ((1.0 - lb) * _sigmoid(-fc))[:, :C_W]
    qc_ref[...] = _silu(proj(SEG_QC))[:, :C_W]
    ic_ref[...] = proj(SEG_IC)[:, :C_W]
    gc_ref[...] = _silu(proj(SEG_GC))[:, :C_W]


def _inproj(x, g, w_pad, rope_tabs, fb, lbl, layer, seq_t=None, carried=None, tm=256):
    n, d = x.shape
    depth = w_pad.shape[0]
    tm = min(tm, n)
    assert n % tm == 0 and rope_tabs[0].shape[0] % tm == 0
    row = lambda w: pl.BlockSpec((tm, w), lambda i: (i, 0))
    full = lambda a: pl.BlockSpec(a.shape, lambda i: (0,) * a.ndim, pipeline_mode=pl.Buffered(1))
    n_tab = rope_tabs[0].shape[0] // tm
    tab = pl.BlockSpec((tm, LANES), lambda i: (i % n_tab, 0))
    args = [x, g, w_pad, *rope_tabs, fb, lbl]
    in_specs = [row(d), full(g), _layer_spec(w_pad, layer), tab, tab, tab, full(fb), full(lbl)]
    aliases = {}
    if seq_t is None:
        widths = [A_W, A_W, A_W, B_W, B_W, B_W, LANES, C_W, C_W, C_W, C_W, C_W]
        shapes = [(n, w) for w in widths]
        out_specs = [row(w) for w in widths]
    else:
        assert seq_t % tm == 0
        per_seq = seq_t // tm
        col = lambda w: pl.BlockSpec((None, None, w, tm), lambda i: (layer, i // per_seq, 0, i % per_seq))
        kinds = [(row, A_W), (row, A_W), (col, A_W), (col, A_W), (row, SEG_W), (row, SEG_W), (col, B_W),
                 (col, B_W), (row, LANES), (row, C_W), (row, C_W), (row, C_W), (row, C_W), (row, C_W)]
        widths = [w for _, w in kinds]
        shapes = [(n, w) if f is row else (depth, n // seq_t, w, seq_t) for f, w in kinds]
        out_specs = [f(w) for f, w in kinds]
        if carried is not None:
            stacked = [i for i, (f, _) in enumerate(kinds) if f is col]
            aliases = {len(args) + k: i for k, i in enumerate(stacked)}
            args += list(carried)
            in_specs += [pl.BlockSpec(memory_space=pl.ANY)] * len(carried)
    est = 2 * d * D_IN_PAD + 2 * tm * 4 * (d + sum(widths) + 3 * LANES) + 8 * tm * SEG_W * 4
    return pl.pallas_call(
        functools.partial(_inproj_body, layer=layer, transposed=seq_t is not None, n_carried=len(aliases)),
        out_shape=[jax.ShapeDtypeStruct(s, F32) for s in shapes],
        grid=(n // tm,),
        in_specs=in_specs,
        out_specs=out_specs,
        scratch_shapes=[pltpu.VMEM((tm, SEG_W), F32)] if seq_t is not None else [],
        input_output_aliases=aliases,
        compiler_params=pltpu.CompilerParams(
            dimension_semantics=("arbitrary",), vmem_limit_bytes=_vmem_limit(est)),
        name="inproj",
    )(*args)


def _topk_mask(gate, allowed, topk, axis):
    n = gate.shape[axis]
    pos = lax.broadcasted_iota(jnp.int32, gate.shape, axis).astype(F32)
    g = jnp.where(allowed, gate, NEG_BIG)
    sel = jnp.zeros(gate.shape, F32)
    for _ in range(topk):
        m = jnp.max(g, axis=axis, keepdims=True)
        idx = jnp.min(jnp.where(g == m, pos, float(n)), axis=axis, keepdims=True)
        hit = pos == idx
        sel = jnp.where(hit & allowed, 1.0, sel)
        g = jnp.where(hit, -jnp.inf, g)
    return sel


def _own_lanes(shape, h):
    return lax.broadcasted_iota(jnp.int32, shape, 1) // HEAD_DIM == h % 2


def _flash_step(heads, score_fn, value_fn, s_ref, p_ref, m_ref, l_ref, acc_ref):
    for h in range(heads):
        s_ref[h] = score_fn(h)
    alphas = []
    for h in range(heads):
        m_old = m_ref[h]
        m_new = jnp.maximum(m_old, jnp.max(s_ref[h], axis=0, keepdims=True))
        alpha = jnp.exp(m_old - m_new)
        p = jnp.exp(s_ref[h] - m_new)
        l_ref[h] = alpha * l_ref[h] + jnp.sum(p, axis=0, keepdims=True)
        m_ref[h] = m_new
        p_ref[h] = p.astype(BF16)
        alphas.append(alpha)
    for h in range(heads):
        acc_ref[h] = acc_ref[h] * alphas[h] + _dot(value_fn(h), p_ref[h])


def _flash_init(m_ref, l_ref, acc_ref):
    m_ref[...] = jnp.full(m_ref.shape, -jnp.inf, F32)
    l_ref[...] = jnp.zeros(l_ref.shape, F32)
    acc_ref[...] = jnp.zeros(acc_ref.shape, F32)


def _flash_finish(o_ref, l_ref, acc_ref, heads):
    blk = acc_ref.shape[2]
    parts = [acc_ref[h] / l_ref[h] for h in range(heads)]
    parts += [jnp.zeros((HEAD_DIM, blk), F32)] * (SEG_W // HEAD_DIM - heads)
    o_ref[...] = jnp.concatenate(parts, axis=0).T[:, :o_ref.shape[1]]


def _flash_scratch(heads, blk):
    return [pltpu.VMEM((heads, blk, blk), F32), pltpu.VMEM((heads, blk, blk), BF16),
            pltpu.VMEM((heads, 1, blk), F32), pltpu.VMEM((heads, 1, blk), F32),
            pltpu.VMEM((heads, HEAD_DIM, blk), F32)]


def _moba_prompt_body(q_ref, k_ref, vt_ref, o_ref, kmean_ref, k16_ref, vt16_ref, qm_ref, sel_ref,
                      s_ref, p_ref, m_ref, l_ref, acc_ref, *, blk):
    i = pl.program_id(1)
    nb = k_ref.shape[0] // blk
    scale = HEAD_DIM ** -0.5

    @pl.when(i == 0)
    def _():
        for n in range(nb):
            kb = k_ref[n * blk:(n + 1) * blk, :]
            kmean_ref[n:n + 1, :] = jnp.sum(kb, axis=0, keepdims=True) * (1.0 / blk)
            k16_ref[n * blk:(n + 1) * blk, :] = kb.astype(BF16)
            vt16_ref[n] = vt_ref[:, n * blk:(n + 1) * blk].astype(BF16)

    past = lax.broadcasted_iota(jnp.int32, (nb, blk), 0) < i
    for h in range(A_HEADS):
        c0 = (h // 2) * LANES
        qwin = q_ref[:, c0:c0 + LANES]
        kmean_h = jnp.where(_own_lanes((nb, LANES), h), kmean_ref[:, c0:c0 + LANES], 0.0)
        gate = _dot_nt(kmean_h, qwin, precision=HIGHEST)
        sel_ref[h] = _topk_mask(gate, past, MOBA_TOPK, axis=0)
        qm_ref[h] = jnp.where(_own_lanes((blk, LANES), h), qwin * scale, 0.0).astype(BF16)
    _flash_init(m_ref, l_ref, acc_ref)
    causal = (lax.broadcasted_iota(jnp.int32, (blk, blk), 0)
              <= lax.broadcasted_iota(jnp.int32, (blk, blk), 1))

    def step(n, mask_fn):
        rows = pl.ds(pl.multiple_of(n * blk, blk), blk)
        _flash_step(
            A_HEADS,
            lambda h: mask_fn(h, _dot_nt(k16_ref[rows, (h // 2) * LANES:(h // 2 + 1) * LANES], qm_ref[h])),
            lambda h: vt16_ref[n, h * HEAD_DIM:(h + 1) * HEAD_DIM, :],
            s_ref, p_ref, m_ref, l_ref, acc_ref)

    step(i, lambda h, st: jnp.where(causal, st, NEG_BIG))

    def body(n, carry):
        step(n, lambda h, st: jnp.where(sel_ref[h, pl.ds(n, 1), :] > 0.0, st, NEG_BIG))
        return carry

    lax.fori_loop(0, i, body, 0)
    _flash_finish(o_ref, l_ref, acc_ref, A_HEADS)


def _moba_prompt(q, k, vt, layer, batch, seq):
    blk = MOBA_BLOCK
    assert seq % blk == 0
    nq = seq // blk
    est = 2 * 2 * seq * A_W * 4 + 2 * seq * A_W * 2 + 4 * blk * A_W * 4 + 24 * blk * blk * 4
    return pl.pallas_call(
        functools.partial(_moba_prompt_body, blk=blk),
        out_shape=jax.ShapeDtypeStruct(q.shape, F32),
        grid=(batch, nq),
        in_specs=[pl.BlockSpec((blk, A_W), lambda b, i: (b * nq + i, 0)),
                  pl.BlockSpec((seq, A_W), lambda b, i: (b, 0)),
                  pl.BlockSpec((None, None, A_W, seq), lambda b, i: (layer, b, 0, 0))],
        out_specs=pl.BlockSpec((blk, A_W), lambda b, i: (b * nq + i, 0)),
        scratch_shapes=[pltpu.VMEM((nq, A_W), F32), pltpu.VMEM((seq, A_W), BF16),
                        pltpu.VMEM((nq, A_W, blk), BF16), pltpu.VMEM((A_HEADS, blk, LANES), BF16),
                        pltpu.VMEM((A_HEADS, nq, blk), F32)] + _flash_scratch(A_HEADS, blk),
        compiler_params=pltpu.CompilerParams(
            dimension_semantics=("arbitrary", "arbitrary"), vmem_limit_bytes=_vmem_limit(est)),
        name="moba_prompt",
    )(q, k, vt)


FOX_AUG = 8


def _fox_prompt_body(q_ref, k_ref, vt_ref, lf_ref, o_ref, bq_ref, kaug_ref, vt16_ref, qaug_ref,
                     s_ref, p_ref, m_ref, l_ref, acc_ref, *, blk):
    i = pl.program_id(1)
    nb = k_ref.shape[0] // blk
    scale = HEAD_DIM ** -0.5
    key_i = lax.broadcasted_iota(jnp.int32, (blk, blk), 0)
    qry_i = lax.broadcasted_iota(jnp.int32, (blk, blk), 1)

    @pl.when(i == 0)
    def _():
        tri = (qry_i <= key_i).astype(F32)
        head = lax.broadcasted_iota(jnp.int32, (LANES, LANES), 0)
        lane = lax.broadcasted_iota(jnp.int32, (LANES, LANES), 1)
        place = lambda off: ((lane == head * FOX_AUG + off) & (head < B_HEADS)).astype(BF16)
        lane1 = lax.broadcasted_iota(jnp.int32, (1, LANES), 1)
        used = lane1 < B_HEADS * FOX_AUG
        ones_q = (used & (lane1 % FOX_AUG < 3)).astype(F32)
        ones_k = (used & (lane1 % FOX_AUG >= 3) & (lane1 % FOX_AUG < 6)).astype(F32)
        carry = jnp.zeros((1, LANES), F32)
        for n in range(nb):
            rows = slice(n * blk, (n + 1) * blk)
            c = _dot(tri, lf_ref[rows, :], precision=HIGHEST) + carry
            carry = c[blk - 1:blk, :]
            pieces = _bf16_pieces(c)
            bq = ones_q + sum(_dot(pieces[j], place(3 + j)) for j in range(3))
            ak = ones_k - sum(_dot(pieces[j], place(j)) for j in range(3))
            bq_ref[rows, :] = bq.astype(BF16)
            for w in range(SEG_W // LANES):
                kaug_ref[w, rows, :LANES] = k_ref[rows, w * LANES:(w + 1) * LANES].astype(BF16)
                kaug_ref[w, rows, LANES:] = ak.astype(BF16)
            vt16_ref[n] = vt_ref[:, rows].astype(BF16)

    bq_rows = bq_ref[pl.ds(pl.multiple_of(i * blk, blk), blk), :]
    group = lax.broadcasted_iota(jnp.int32, (blk, LANES), 1) // FOX_AUG
    for h in range(B_HEADS):
        c0 = (h // 2) * LANES
        qaug_ref[h, :, :LANES] = jnp.where(_own_lanes((blk, LANES), h), q_ref[:, c0:c0 + LANES] * scale,
                                           0.0).astype(BF16)
        qaug_ref[h, :, LANES:] = jnp.where(group == h, bq_rows, jnp.zeros_like(bq_rows))
    _flash_init(m_ref, l_ref, acc_ref)
    causal = key_i <= qry_i

    def step(n, diag):
        rows = pl.ds(pl.multiple_of(n * blk, blk), blk)

        def scores(h):
            st = _dot_nt(kaug_ref[h // 2, rows, :], qaug_ref[h])
            return jnp.where(causal, st, NEG_BIG) if diag else st

        _flash_step(B_HEADS, scores, lambda h: vt16_ref[n, h * HEAD_DIM:(h + 1) * HEAD_DIM, :],
                    s_ref, p_ref, m_ref, l_ref, acc_ref)

    step(i, True)

    def body(n, carry):
        step(n, False)
        return carry

    lax.fori_loop(0, i, body, 0)
    _flash_finish(o_ref, l_ref, acc_ref, B_HEADS)


def _fox_prompt(q, k, vt, lf, layer, batch, seq, blk=256):
    assert seq % blk == 0
    nq = seq // blk
    est = (2 * seq * (SEG_W + B_W + LANES) * 4 + seq * (3 * 2 * LANES + LANES + B_W) * 2
           + 4 * blk * SEG_W * 4 + 24 * blk * blk * 4)
    return pl.pallas_call(
        functools.partial(_fox_prompt_body, blk=blk),
        out_shape=jax.ShapeDtypeStruct((batch * seq, B_W), F32),
        grid=(batch, nq),
        in_specs=[pl.BlockSpec((blk, SEG_W), lambda b, i: (b * nq + i, 0)),
                  pl.BlockSpec((seq, SEG_W), lambda b, i: (b, 0)),
                  pl.BlockSpec((None, None, B_W, seq), lambda b, i: (layer, b, 0, 0)),
                  pl.BlockSpec((seq, LANES), lambda b, i: (b, 0))],
        out_specs=pl.BlockSpec((blk, B_W), lambda b, i: (b * nq + i, 0)),
        scratch_shapes=[pltpu.VMEM((seq, LANES), BF16), pltpu.VMEM((SEG_W // LANES, seq, 2 * LANES), BF16),
                        pltpu.VMEM((nq, B_W, blk), BF16), pltpu.VMEM((B_HEADS, blk, 2 * LANES), BF16)]
        + _flash_scratch(B_HEADS, blk),
        compiler_params=pltpu.CompilerParams(
            dimension_semantics=("arbitrary", "arbitrary"), vmem_limit_bytes=_vmem_limit(est)),
        name="fox_prompt",
    )(q, k, vt, lf)


def _hgrn_body(q_ref, k_ref, lf_ref, v_ref, gs_ref, gn_ref, st0_ref, o_ref, sto_ref, st_ref,
               qd_ref, kd_ref, v16_ref, dl_ref, upd_ref, *, sub, tile):
    seq = q_ref.shape[0]
    n_sub = tile // sub
    t_i = lax.broadcasted_iota(jnp.int32, (tile, tile), 0)
    s_i = lax.broadcasted_iota(jnp.int32, (tile, tile), 1)
    same_sub = t_i // sub == s_i // sub
    tri_blk = (same_sub & (s_i <= t_i)).astype(BF16)
    e_i = lax.broadcasted_iota(jnp.int32, (C_W, C_W), 0) // HEAD_DIM
    d_i = lax.broadcasted_iota(jnp.int32, (C_W, C_W), 1) // HEAD_DIM
    same_head = e_i == d_i
    ones_bd = same_head.astype(BF16)
    row_in_sub = lax.broadcasted_iota(jnp.int32, (n_sub, sub, C_W), 1)
    split = lambda a: a.reshape(n_sub, sub, C_W)

    def tile_rows(t):
        return pl.ds(pl.multiple_of(t * tile, tile), tile)

    def decay_pass(t, _):
        rows = tile_rows(t)
        lf, q, k, v = lf_ref[rows, :], q_ref[rows, :], k_ref[rows, :], v_ref[rows, :]
        cum = _sel_dot(tri_blk, lf)
        cum3, q3, k3, v3 = split(cum), split(q), split(k), split(v)
        last3 = cum3[:, sub - 1:sub, :]
        qd_ref[rows, :] = (q * jnp.exp(cum)).astype(BF16)
        kd_ref[rows, :] = (k3 * jnp.exp(last3 - cum3)).reshape(tile, C_W).astype(BF16)
        v16_ref[rows, :] = v.astype(BF16)
        dl_ref[rows, :] = jnp.broadcast_to(jnp.exp(last3), (n_sub, sub, C_W)).reshape(tile, C_W)
        n_t = sub // SUBLANES
        tiles = lambda a: a.reshape(n_sub, n_t, SUBLANES, C_W)
        cum4, q4 = tiles(cum), tiles(q)
        row8 = lax.broadcasted_iota(jnp.int32, (n_sub, SUBLANES, C_W), 1)
        o_tiles = [jnp.zeros((n_sub, SUBLANES, C_W), F32) for _ in range(n_t)]
        for s in range(sub):
            for t in range(s // SUBLANES, n_t):
                dec = jnp.exp(jnp.where(row8 + t * SUBLANES >= s, cum4[:, t] - cum3[:, s:s + 1, :], NEG_BIG))
                m = (q4[:, t] * k3[:, s:s + 1, :] * dec).reshape(n_sub * SUBLANES, C_W).astype(BF16)
                w = jnp.concatenate([_dot(m[:, :MXU_W], ones_bd[:MXU_W, :MXU_W]),
                                     _dot(m[:, MXU_W:], ones_bd[MXU_W:, MXU_W:])], axis=1)
                o_tiles[t] = o_tiles[t] + w.reshape(n_sub, SUBLANES, C_W) * v3[:, s:s + 1, :]
        o_ref[rows, :] = jnp.stack(o_tiles, axis=1).reshape(tile, C_W)
        return 0

    lax.fori_loop(0, seq // tile, decay_pass, 0)
    spread = (lax.broadcasted_iota(jnp.int32, (HEAD_DIM, C_W), 0)
              == lax.broadcasted_iota(jnp.int32, (HEAD_DIM, C_W), 1) % HEAD_DIM).astype(BF16)
    st_ref[...] = jnp.where(same_head, _dot_sel(st0_ref[...], spread), 0.0)

    group = upd_ref.shape[0]

    def chunks(g, _):
        for j in range(group):
            rows = pl.ds(pl.multiple_of((g * group + j) * sub, sub), sub)
            upd_ref[j] = jnp.where(same_head, _dot_tn(v16_ref[rows, :], kd_ref[rows, :]), 0.0)
        for j in range(group):
            c = g * group + j
            rows = pl.ds(pl.multiple_of(c * sub, sub), sub)
            st = st_ref[...]
            o_ref[rows, :] = o_ref[rows, :] + _dot_nt(qd_ref[rows, :], st.astype(BF16))
            st_ref[...] = st * dl_ref[pl.ds(c * sub, 1), :] + upd_ref[j]
        return 0

    lax.fori_loop(0, seq // (sub * group), chunks, 0)

    gn = gn_ref[...]

    def norm_pass(t, _):
        rows = tile_rows(t)
        o = o_ref[rows, :]
        sq = o * o
        hi = sq.astype(BF16)
        lo = (sq - hi.astype(F32)).astype(BF16)
        ms = (_dot(hi, ones_bd) + _dot(lo, ones_bd)) * (1.0 / HEAD_DIM)
        o_ref[rows, :] = o * lax.rsqrt(ms + EPS) * gn * gs_ref[rows, :]
        return 0

    lax.fori_loop(0, seq // tile, norm_pass, 0)
    gather = (lax.broadcasted_iota(jnp.int32, (C_W, HEAD_DIM), 0) % HEAD_DIM
              == lax.broadcasted_iota(jnp.int32, (C_W, HEAD_DIM), 1)).astype(BF16)
    sto_ref[...] = _dot_sel(st_ref[...], gather)


def _hgrn(q, k, lf, v, gs, gn, st0, batch, seq):
    sub = math.gcd(seq, 16)
    tile = min(seq, 256)
    assert seq % tile == 0 and tile % sub == 0
    row = pl.BlockSpec((seq, C_W), lambda b: (b, 0))
    st_spec = pl.BlockSpec((None, C_W, HEAD_DIM), lambda b: (b, 0, 0))
    group = math.gcd(seq // sub, 8)
    est = (2 * 6 * seq * C_W * 4 + (8 + group) * C_W * C_W * 4 + seq * C_W * (3 * 2 + 4)
           + 24 * tile * C_W * 4)
    return pl.pallas_call(
        functools.partial(_hgrn_body, sub=sub, tile=tile),
        out_shape=[jax.ShapeDtypeStruct((batch * seq, C_W), F32),
                   jax.ShapeDtypeStruct((batch, C_W, HEAD_DIM), F32)],
        grid=(batch,),
        in_specs=[row, row, row, row, row, pl.BlockSpec((1, C_W), lambda b: (0, 0)), st_spec],
        out_specs=[row, st_spec],
        scratch_shapes=[pltpu.VMEM((C_W, C_W), F32),
                        pltpu.VMEM((seq, C_W), BF16), pltpu.VMEM((seq, C_W), BF16),
                        pltpu.VMEM((seq, C_W), BF16), pltpu.VMEM((seq, C_W), F32),
                        pltpu.VMEM((group, C_W, C_W), F32)],
        compiler_params=pltpu.CompilerParams(
            dimension_semantics=("arbitrary",), vmem_limit_bytes=_vmem_limit(est)),
        name="hgrn",
    )(q, k, lf, v, gs, gn, st0)


def _block_diag_q(q, heads, scale):
    t, w = q.shape
    rep = jnp.concatenate([q] * heads, axis=0)
    r = lax.broadcasted_iota(jnp.int32, (heads * t, w), 0) // t
    c = lax.broadcasted_iota(jnp.int32, (heads * t, w), 1) // HEAD_DIM
    return jnp.where(r == c, rep * scale, 0.0)


def _collapse_heads(o, heads, t):
    r = lax.broadcasted_iota(jnp.int32, o.shape, 0) // t
    c = lax.broadcasted_iota(jnp.int32, o.shape, 1) // HEAD_DIM
    o = jnp.where(r == c, o, 0.0)
    out = o[0:t, :]
    for h in range(1, heads):
        out = out + o[h * t:(h + 1) * t, :]
    return out


def _rows_per_head(x, heads, t):
    return jnp.concatenate([jnp.broadcast_to(x[h:h + 1, :], (t, x.shape[1])) for h in range(heads)], axis=0)


def _decode_body(pt_ref, q_ref, kn_ref, vn_ref, *rest, heads, moba, pp, n_steps, t_new):
    width = heads * HEAD_DIM
    if moba:
        k_refs, v_refs, rest = rest[:pp], rest[pp:2 * pp], rest[2 * pp:]
    else:
        lfn_ref, rest = rest[0], rest[1:]
        k_refs, v_refs, lf_refs, rest = rest[:pp], rest[pp:2 * pp], rest[2 * pp:3 * pp], rest[3 * pp:]
    o_ref, qbd_ref, aux_ref, m_ref, l_ref, part_ref, s_ref, p_ref = rest
    j = pl.program_id(1)
    page = k_refs[0].shape[-1]
    rows = heads * t_new
    per_blk = MOBA_BLOCK // page
    blk_per_step = pp // per_blk
    n_blk = n_steps * blk_per_step
    scale = HEAD_DIM ** -0.5
    lane = lax.broadcasted_iota(jnp.int32, (rows, LANES), 1)

    @pl.when(j == 0)
    def _():
        qbd_ref[...] = _block_diag_q(q_ref[...], heads, 1.0)
        aux_ref[...] = jnp.zeros(aux_ref.shape, F32)
        m_ref[...] = jnp.zeros(m_ref.shape, F32)
        l_ref[...] = jnp.zeros(l_ref.shape, F32)

    qb = (qbd_ref[...] * scale).astype(BF16)
    m_all, l_all = m_ref[...], l_ref[...]
    aux = aux_ref[...]
    if moba:
        lane_w = lax.broadcasted_iota(jnp.int32, (width, LANES), 1)
    else:
        upper = (lax.broadcasted_iota(jnp.int32, (page, page), 0)
                 <= lax.broadcasted_iota(jnp.int32, (page, page), 1)).astype(BF16)
        c_pages = _dot_sel(jnp.concatenate([r[...] for r in lf_refs], axis=0), upper)
    for jj in range(pp):
        kt = k_refs[jj][...].reshape(width, page)
        s = _dot(qb, kt.astype(BF16))
        if moba:
            ksum = kt if jj % per_blk == 0 else ksum + kt
            if jj % per_blk == per_blk - 1:
                kmean_n = jnp.sum(ksum, axis=1, keepdims=True) * (1.0 / MOBA_BLOCK)
                aux = aux + jnp.where(lane_w == j * blk_per_step + jj // per_blk, kmean_n, 0.0)
        else:
            c_page = c_pages[jj * SUBLANES:(jj + 1) * SUBLANES, :]
            s = s - _rows_per_head(c_page + aux, heads, t_new)
            aux = aux + c_page[:, page - 1:page]
        s_ref[jj] = s
    for g in range(blk_per_step):
        n = j * blk_per_step + g
        pages = range(g * per_blk, (g + 1) * per_blk)
        m_n = functools.reduce(jnp.maximum, [jnp.max(s_ref[jj], axis=1, keepdims=True) for jj in pages])
        l_n = 0.0
        for jj in pages:
            p = jnp.exp(s_ref[jj] - m_n)
            l_n = l_n + jnp.sum(p, axis=1, keepdims=True)
            p_ref[jj] = p.astype(BF16)
        m_all = jnp.where(lane == n, m_n, m_all)
        l_all = jnp.where(lane == n, l_n, l_all)
    for g in range(blk_per_step):
        part_ref[j * blk_per_step + g] = sum(
            _dot_nt(p_ref[jj], v_refs[jj][...].reshape(width, page).astype(BF16))
            for jj in range(g * per_blk, (g + 1) * per_blk))
    m_ref[...] = m_all
    l_ref[...] = l_all
    aux_ref[...] = aux

    @pl.when(j == n_steps - 1)
    def _():
        qi = lax.broadcasted_iota(jnp.int32, (rows, t_new), 0) % t_new
        kj = lax.broadcasted_iota(jnp.int32, (rows, t_new), 1)
        s_own = _dot_nt(qb, kn_ref[...].astype(BF16))
        if moba:
            gate = _dot(qbd_ref[...], aux, precision=HIGHEST)
            sel = _topk_mask(gate, lane < n_blk, MOBA_TOPK, axis=1) > 0.0
        else:
            r8 = lax.broadcasted_iota(jnp.int32, (t_new, t_new), 0)
            c8 = lax.broadcasted_iota(jnp.int32, (t_new, t_new), 1)
            c_new = (_dot(lfn_ref[...], (r8 <= c8).astype(F32), precision=HIGHEST)
                     + aux[:heads, :])
            s_own = s_own - _rows_per_head(c_new, heads, t_new)
            sel = lane < n_blk
        s_own = jnp.where(kj <= qi, s_own, NEG_BIG)
        m_own = jnp.max(s_own, axis=1, keepdims=True)
        m_tot = jnp.maximum(m_own, jnp.max(jnp.where(sel, m_all, -jnp.inf), axis=1, keepdims=True))
        p_own = jnp.exp(s_own - m_tot)
        w = jnp.where(sel, jnp.exp(m_all - m_tot), 0.0)
        l_tot = jnp.sum(p_own, axis=1, keepdims=True) + jnp.sum(w * l_all, axis=1, keepdims=True)
        acc = _dot(p_own.astype(BF16), vn_ref[...].astype(BF16))
        for n in range(n_blk):
            acc = acc + w[:, n:n + 1] * part_ref[n]
        o_ref[...] = _collapse_heads(acc / l_tot, heads, t_new)


def _decode(page_table, q, k_new, v_new, cache_kt, cache_vt, layer, n_seq, t_new, heads, lf=None, pages_per_step=32):
    n_pages = page_table.shape[1]
    pp = min(pages_per_step, n_pages)
    page = cache_kt.shape[-1]
    width = heads * HEAD_DIM
    per_blk = MOBA_BLOCK // page
    assert MOBA_BLOCK % page == 0 and n_pages % pp == 0 and pp % per_blk == 0
    n_steps = n_pages // pp
    n_blk = n_pages // per_blk
    assert n_blk <= LANES
    rows = heads * t_new
    new = pl.BlockSpec((t_new, width), lambda b, j, pt: (b, 0))

    def pages(block, index_fn):
        return [pl.BlockSpec(block, lambda b, j, pt, jj=jj: index_fn(pt[b, j * pp + jj])) for jj in range(pp)]

    kv_pages = lambda: pages((None, None, heads, HEAD_DIM, page), lambda p: (layer, p, 0, 0, 0))
    args, specs = [q, k_new, v_new], [new, new, new]
    if lf is not None:
        args.append(lf[0])
        specs.append(pl.BlockSpec((None, heads, t_new), lambda b, j, pt: (b, 0, 0)))
    args += [cache_kt] * pp + [cache_vt] * pp
    specs += kv_pages() + kv_pages()
    if lf is not None:
        args += [lf[1]] * pp
        specs += pages((None, None, SUBLANES, page), lambda p: (layer, p, 0, 0))
    aux_shape = (SUBLANES, 1) if lf is not None else (width, LANES)
    est = 2 * 2 * pp * page * (width + 8) * 4 + n_blk * rows * width * 4 + 4 * LANES * width * 4
    grid_spec = pltpu.PrefetchScalarGridSpec(
        num_scalar_prefetch=1,
        grid=(n_seq, n_steps),
        in_specs=specs,
        out_specs=new,
        scratch_shapes=[pltpu.VMEM((rows, width), F32), pltpu.VMEM(aux_shape, F32),
                        pltpu.VMEM((rows, LANES), F32), pltpu.VMEM((rows, LANES), F32),
                        pltpu.VMEM((n_blk, rows, width), F32),
                        pltpu.VMEM((pp, rows, page), F32), pltpu.VMEM((pp, rows, page), BF16)],
    )
    return pl.pallas_call(
        functools.partial(_decode_body, heads=heads, moba=lf is None, pp=pp, n_steps=n_steps, t_new=t_new),
        out_shape=jax.ShapeDtypeStruct((n_seq * t_new, width), F32),
        grid_spec=grid_spec,
        compiler_params=pltpu.CompilerParams(
            dimension_semantics=("arbitrary", "arbitrary"), vmem_limit_bytes=_vmem_limit(est)),
        name="moba_decode" if lf is None else "fox_decode",
    )(page_table, *args)


def _pad_w_in(w):
    sizes = (A_W,) * 3 + (B_W,) * 3 + (B_HEADS,) + (C_W,) * 4
    offs = np.cumsum((0,) + sizes)
    segs = [SEG_QA, SEG_KA, SEG_VA, SEG_QF, SEG_KF, SEG_VF, SEG_FF, SEG_QC, SEG_FC, SEG_IC, SEG_GC]
    order = np.argsort(segs)
    ends = sorted(segs)[1:] + [D_IN_PAD]
    wb = w.astype(BF16)
    parts = []
    for i, end in zip(order, ends):
        parts.append(wb[:, :, offs[i]:offs[i] + sizes[i]])
        gap = end - segs[i] - sizes[i]
        if gap:
            parts.append(jnp.zeros(w.shape[:2] + (gap,), BF16))
    return jnp.concatenate(parts, axis=2)


def _rope_tables(pos0, t, reps):
    half = HEAD_DIM // 2
    inv = ROPE_THETA ** (-jnp.arange(half, dtype=F32) * 2.0 / HEAD_DIM)
    ang = (pos0 + jnp.arange(t, dtype=F32))[:, None] * inv[None, :]
    cos, sin, zero = jnp.cos(ang), jnp.sin(ang), jnp.zeros((t, half), F32)
    heads_per_vreg = LANES // HEAD_DIM
    tabs = (jnp.concatenate([cos, cos] * heads_per_vreg, axis=1),
            jnp.concatenate([-sin, zero] * heads_per_vreg, axis=1),
            jnp.concatenate([zero, sin] * heads_per_vreg, axis=1))
    return tuple(jnp.tile(tb, (reps, 1)) for tb in tabs)


def _pad_lanes(a, width):
    return jnp.pad(a, ((0, 0), (0, width - a.shape[1])))


def _state_rows(s):
    b, h, dk, dv = s.shape
    return jnp.swapaxes(s, 2, 3).reshape(b, h * dv, dk)


def _rows_state(st, h):
    b, _, dk = st.shape
    return jnp.swapaxes(st.reshape(b, h, -1, dk), 2, 3)


def kernel(x_prompt, x_sample, cache_moba_k, cache_moba_v, cache_fox_k, cache_fox_v, cache_fox_logf, state_hgrn, page_table, norm_ffn1_g, ffn1_w_gate, ffn1_w_up, ffn1_w_down, norm_mix_g, w_in, fox_f_bias, hgrn_lb_logits, hgrn_norm_g, w_out, norm_ffn2_g, ffn2_w_gate, ffn2_w_up, ffn2_w_down, norm_final_g):
    depth = w_in.shape[0]
    batch, seq, d = x_prompt.shape
    n_seq, t_new, _ = x_sample.shape
    page = cache_moba_k.shape[2]
    past_len = page_table.shape[1] * page

    row = lambda a: a.reshape(1, -1)
    bf = lambda w: w.astype(BF16)
    ffn1 = lambda l: (row(norm_ffn1_g[l]), w1g, w1u, w1d, l)
    w1g, w1u, w1d = bf(ffn1_w_gate), bf(ffn1_w_up), bf(ffn1_w_down)
    w2g, w2u, w2d = bf(ffn2_w_gate), bf(ffn2_w_up), bf(ffn2_w_down)
    wo = bf(w_out)
    w_in_pad = _pad_w_in(w_in)
    fb = [_pad_lanes(row(fox_f_bias[l]), LANES) for l in range(depth)]
    lbl = _pad_lanes(hgrn_lb_logits.astype(F32), SEG_W)
    gn = jnp.tile(hgrn_norm_g, (1, C_HEADS))
    to_pages = lambda c: jnp.transpose(c, (0, 1, 3, 4, 2))
    ckt, cvt, fkt, fvt = (to_pages(c) for c in (cache_moba_k, cache_moba_v, cache_fox_k, cache_fox_v))
    flt = jnp.pad(jnp.swapaxes(cache_fox_logf, 2, 3), ((0, 0), (0, 0), (0, SUBLANES - B_HEADS), (0, 0)))

    def trunk(x, sample):
        if sample:
            nb, t = n_seq, t_new
            tabs = _rope_tables(float(past_len), t, nb)
        else:
            nb, t = batch, seq
            tabs = _rope_tables(0.0, t, 1)
        new, kv_t = [], None
        for l in range(depth):
            if l == 0:
                x = _ffn(x, *ffn1(l))
            args = (x, row(norm_mix_g[l]), w_in_pad, tabs, fb[l], lbl, l)
            heads_last = lambda a, h: a.reshape(nb, t, h, HEAD_DIM)
            if sample:
                qa, ka, va, qf, kf, vf, lf, qc, kc, lc, ic, gc = _inproj(*args)
                oa = _decode(page_table, qa, ka, va, ckt, cvt, l, nb, t, A_HEADS)
                lf_t = jnp.swapaxes(lf[:, :B_HEADS].reshape(nb, t, B_HEADS), 1, 2)
                of = _decode(page_table, qf, kf, vf, fkt, fvt, l, nb, t, B_HEADS, lf=(lf_t, flt))
                st0 = _state_rows(state_hgrn[l].astype(F32))
                rows_kv = (heads_last(ka, A_HEADS), heads_last(va, A_HEADS),
                           heads_last(kf, B_HEADS), heads_last(vf, B_HEADS))
            else:
                qa, ka, ka_t, va_t, qf, kf, kf_t, vf_t, lf, qc, kc, lc, ic, gc = _inproj(
                    *args, seq_t=t, carried=kv_t)
                kv_t = (ka_t, va_t, kf_t, vf_t)
                oa = _moba_prompt(qa, ka, va_t, l, nb, t)
                of = _fox_prompt(qf, kf, vf_t, lf, l, nb, t)
                st0 = jnp.zeros((nb, C_W, HEAD_DIM), F32)
                rows_kv = ()
            oc, st = _hgrn(qc, kc, lc, ic, gc, row(gn[l]), st0, nb, t)
            last = l == depth - 1
            x = _ffn(x, row(norm_ffn2_g[l]), w2g, w2u, w2d, l, mix=(oa, of, oc, wo),
                     g_final=row(norm_final_g) if last else None)
            if not last:
                x = _ffn(x, *ffn1(l + 1))
            new.append(rows_kv + (lf[:, :B_HEADS].reshape(nb, t, B_HEADS), _rows_state(st, C_HEADS)))
        outs = [jnp.stack(z) for z in zip(*new)]
        if not sample:
            heads_of = (A_HEADS, A_HEADS, B_HEADS, B_HEADS)
            outs = [jnp.transpose(a.reshape(depth, nb, h, HEAD_DIM, t), (0, 1, 4, 2, 3))
                    for a, h in zip(kv_t, heads_of)] + outs
        return x.reshape(nb, t, d), outs

    y_p, (mk_p, mv_p, fk_p, fv_p, fl_p, hs_p) = trunk(x_prompt.reshape(batch * seq, d), False)
    y_s, (mk_s, mv_s, fk_s, fv_s, fl_s, hs_s) = trunk(x_sample.reshape(n_seq * t_new, d), True)
    return (y_p, y_s, mk_p, mv_p, fk_p, fv_p, fl_p, hs_p, mk_s, mv_s, fk_s, fv_s, fl_s, hs_s)
```

```python
import functools
import math

import numpy as np
import jax
import jax.numpy as jnp
from jax import lax
from jax.experimental import pallas as pl
from jax.experimental.pallas import tpu as pltpu

F32 = jnp.float32
BF16 = jnp.bfloat16
HIGHEST = lax.Precision.HIGHEST

HEAD_DIM = 64
A_HEADS = 6
B_HEADS = 5
C_HEADS = 5
A_W = A_HEADS * HEAD_DIM
B_W = B_HEADS * HEAD_DIM
C_W = C_HEADS * HEAD_DIM
D_FF_CHUNK = 256
MOBA_BLOCK = 256
MOBA_TOPK = 3
ROPE_THETA = 10000.0
EPS = 1e-6
NEG_BIG = -1e30

LANES = 128
SUBLANES = 8
MXU_W = 256
VMEM_PHYSICAL = 64 * 1024 * 1024
VMEM_CAP = VMEM_PHYSICAL - 8 * 1024 * 1024

SEG_W = 384
SEG_QA, SEG_KA, SEG_VA, SEG_QF, SEG_KF, SEG_VF, SEG_QC, SEG_FC, SEG_IC, SEG_GC = (
    i * SEG_W for i in range(10))
SEG_FF = 10 * SEG_W
D_IN_PAD = SEG_FF + LANES


def _vmem_limit(estimate_bytes):
    return int(min(max(2 * estimate_bytes, 32 * 1024 * 1024), VMEM_CAP))


def _dot(a, b, precision=None):
    return jnp.dot(a, b, preferred_element_type=F32, precision=precision)


def _dot_nt(a, b, precision=None):
    return lax.dot_general(a, b, (((1,), (1,)), ((), ())),
                           preferred_element_type=F32, precision=precision)


def _dot_tn(a, b, precision=None):
    return lax.dot_general(a, b, (((0,), (0,)), ((), ())),
                           preferred_element_type=F32, precision=precision)


def _rms(x, g):
    ms = jnp.mean(x * x, axis=-1, keepdims=True)
    return x * lax.rsqrt(ms + EPS) * g


def _sigmoid(x):
    return 1.0 / (1.0 + jnp.exp(-x))


def _silu(x):
    return x * _sigmoid(x)


def _log_sigmoid(x):
    return jnp.minimum(x, 0.0) - jnp.log1p(jnp.exp(-jnp.abs(x)))


def _bf16_pieces(x):
    x1 = x.astype(BF16)
    r1 = x - x1.astype(F32)
    x2 = r1.astype(BF16)
    return x1, x2, (r1 - x2.astype(F32)).astype(BF16)


def _dot_sel(x, sel):
    return sum(_dot(p, sel) for p in _bf16_pieces(x))


def _sel_dot(sel, x):
    return sum(_dot(sel, p) for p in _bf16_pieces(x))


def _ffn_body(*refs, has_mix, has_final, d_ff):
    it = iter(refs)
    x_ref = next(it)
    if has_mix:
        oa_ref, of_ref, oc_ref, wo_ref = (next(it) for _ in range(4))
    g_ref, wg_ref, wu_ref, wd_ref = (next(it) for _ in range(4))
    gf_ref = next(it) if has_final else None
    out_ref = next(it)

    x = x_ref[...]
    if has_mix:
        x = (x + _dot(oa_ref[...].astype(BF16), wo_ref[:A_W, :])
             + _dot(of_ref[...].astype(BF16), wo_ref[A_W:A_W + B_W, :])
             + _dot(oc_ref[...].astype(BF16), wo_ref[A_W + B_W:, :]))
    h = _rms(x, g_ref[...]).astype(BF16)
    acc = jnp.zeros(x.shape, F32)
    for c0 in range(0, d_ff, D_FF_CHUNK):
        gate = _dot(h, wg_ref[:, c0:c0 + D_FF_CHUNK])
        up = _dot(h, wu_ref[:, c0:c0 + D_FF_CHUNK])
        act = (_silu(gate) * up).astype(BF16)
        acc = acc + _dot(act, wd_ref[c0:c0 + D_FF_CHUNK, :])
    y = x + 0.5 * acc
    if has_final:
        y = _rms(y, gf_ref[...])
    out_ref[...] = y


def _layer_spec(a, layer):
    return pl.BlockSpec((None,) + a.shape[1:], lambda *_: (layer,) + (0,) * (a.ndim - 1),
                        pipeline_mode=pl.Buffered(1))


def _ffn(x, g, wg, wu, wd, layer, mix=None, g_final=None, tm=512):
    n, d = x.shape
    d_ff = wg.shape[2]
    tm = min(tm, n)
    assert n % tm == 0 and d_ff % D_FF_CHUNK == 0
    row = lambda w: pl.BlockSpec((tm, w), lambda i: (i, 0))
    full = lambda a: pl.BlockSpec(a.shape, lambda i: (0,) * a.ndim, pipeline_mode=pl.Buffered(1))
    args, specs = [x], [row(d)]
    if mix is not None:
        oa, of, oc, wo = mix
        args += [oa, of, oc, wo]
        specs += [row(oa.shape[1]), row(of.shape[1]), row(oc.shape[1]), _layer_spec(wo, layer)]
    args += [g, wg, wu, wd]
    specs += [full(g), _layer_spec(wg, layer), _layer_spec(wu, layer), _layer_spec(wd, layer)]
    if g_final is not None:
        args.append(g_final)
        specs.append(full(g_final))
    weights = 2 * 3 * d * d_ff + (2 * d * d if mix is not None else 0)
    tiles = 2 * 2 * tm * d * 4 * (2 if mix is not None else 1) + 6 * tm * d * 4
    return pl.pallas_call(
        functools.partial(_ffn_body, has_mix=mix is not None, has_final=g_final is not None, d_ff=d_ff),
        out_shape=jax.ShapeDtypeStruct((n, d), F32),
        grid=(n // tm,),
        in_specs=specs,
        out_specs=row(d),
        compiler_params=pltpu.CompilerParams(
            dimension_semantics=("arbitrary",), vmem_limit_bytes=_vmem_limit(weights + tiles)),
        name="ffn_mix" if mix is not None else "ffn",
    )(*args)


def _inproj_body(x_ref, g_ref, w_ref, cos_ref, slo_ref, shi_ref, fb_ref, lbl_ref, *out_refs, layer, transposed,
                 n_carried):
    out_refs = out_refs[n_carried:]
    if transposed:
        (qa_ref, ka_ref, kat_ref, vat_ref, qf_ref, kf_ref, kft_ref, vft_ref, lf_ref,
         qc_ref, kc_ref, lc_ref, ic_ref, gc_ref, stage_ref) = out_refs
    else:
        (qa_ref, ka_ref, va_ref, qf_ref, kf_ref, vf_ref, lf_ref,
         qc_ref, kc_ref, lc_ref, ic_ref, gc_ref) = out_refs
    h = _rms(x_ref[...], g_ref[...]).astype(BF16)

    def proj(c0, width=SEG_W):
        return _dot(h, w_ref[:, c0:c0 + width])

    cos, slo, shi = cos_ref[...], slo_ref[...], shi_ref[...]

    def rope(p):
        parts = []
        for c in range(SEG_W // LANES):
            pc = p[:, c * LANES:(c + 1) * LANES]
            parts.append(pc * cos + pltpu.roll(pc, LANES - HEAD_DIM // 2, 1) * slo
                         + pltpu.roll(pc, HEAD_DIM // 2, 1) * shi)
        return jnp.concatenate(parts, axis=1)

    qa_ref[...] = rope(proj(SEG_QA))
    ka = rope(proj(SEG_KA))
    ka_ref[...] = ka
    if transposed:
        def transposed_tile(p):
            stage_ref[...] = p
            return stage_ref[...].T

        kat_ref[...] = ka.T
        vat_ref[...] = transposed_tile(proj(SEG_VA))
        qf_ref[...] = proj(SEG_QF)
        kf_ref[...] = proj(SEG_KF)
        kft_ref[...] = kf_ref[...].T[:B_W, :]
        vft_ref[...] = transposed_tile(proj(SEG_VF))[:B_W, :]
    else:
        va_ref[...] = proj(SEG_VA)
        qf_ref[...] = proj(SEG_QF)[:, :B_W]
        kf_ref[...] = proj(SEG_KF)[:, :B_W]
        vf_ref[...] = proj(SEG_VF)[:, :B_W]
    lf_ref[...] = _log_sigmoid(proj(SEG_FF, LANES) + fb_ref[...])

    lbl = lbl_ref[...]
    e = jnp.exp(lbl - jnp.max(lbl, axis=0, keepdims=True))
    psm = e / jnp.sum(e, axis=0, keepdims=True)
    lb = jnp.zeros((1, SEG_W), F32)
    for j in range(1, layer + 1):
        lb = lb + psm[j:j + 1, :]
    fc = proj(SEG_FC)
    lc_ref[...] = jnp.log(lb + (1.0 - lb) * _sigmoid(fc))[:, :C_W]
    kc_ref[...] = ((1.0 - lb) * _sigmoid(-fc))[:, :C_W]
    qc_ref[...] = _silu(proj(SEG_QC))[:, :C_W]
    ic_ref[...] = proj(SEG_IC)[:, :C_W]
    gc_ref[...] = _silu(proj(SEG_GC))[:, :C_W]


def _inproj(x, g, w_pad, rope_tabs, fb, lbl, layer, seq_t=None, carried=None, tm=256):
    n, d = x.shape
    depth = w_pad.shape[0]
    tm = min(tm, n)
    assert n % tm == 0 and rope_tabs[0].shape[0] % tm == 0
    row = lambda w: pl.BlockSpec((tm, w), lambda i: (i, 0))
    full = lambda a: pl.BlockSpec(a.shape, lambda i: (0,) * a.ndim, pipeline_mode=pl.Buffered(1))
    n_tab = rope_tabs[0].shape[0] // tm
    tab = pl.BlockSpec((tm, LANES), lambda i: (i % n_tab, 0))
    args = [x, g, w_pad, *rope_tabs, fb, lbl]
    in_specs = [row(d), full(g), _layer_spec(w_pad, layer), tab, tab, tab, full(fb), full(lbl)]
    aliases = {}
    if seq_t is None:
        widths = [A_W, A_W, A_W, B_W, B_W, B_W, LANES, C_W, C_W, C_W, C_W, C_W]
        shapes = [(n, w) for w in widths]
        out_specs = [row(w) for w in widths]
    else:
        assert seq_t % tm == 0
        per_seq = seq_t // tm
        col = lambda w: pl.BlockSpec((None, None, w, tm), lambda i: (layer, i // per_seq, 0, i % per_seq))
        kinds = [(row, A_W), (row, A_W), (col, A_W), (col, A_W), (row, SEG_W), (row, SEG_W), (col, B_W),
                 (col, B_W), (row, LANES), (row, C_W), (row, C_W), (row, C_W), (row, C_W), (row, C_W)]
        widths = [w for _, w in kinds]
        shapes = [(n, w) if f is row else (depth, n // seq_t, w, seq_t) for f, w in kinds]
        out_specs = [f(w) for f, w in kinds]
        if carried is not None:
            stacked = [i for i, (f, _) in enumerate(kinds) if f is col]
            aliases = {len(args) + k: i for k, i in enumerate(stacked)}
            args += list(carried)
            in_specs += [pl.BlockSpec(memory_space=pl.ANY)] * len(carried)
    est = 2 * d * D_IN_PAD + 2 * tm * 4 * (d + sum(widths) + 3 * LANES) + 8 * tm * SEG_W * 4
    return pl.pallas_call(
        functools.partial(_inproj_body, layer=layer, transposed=seq_t is not None, n_carried=len(aliases)),
        out_shape=[jax.ShapeDtypeStruct(s, F32) for s in shapes],
        grid=(n // tm,),
        in_specs=in_specs,
        out_specs=out_specs,
        scratch_shapes=[pltpu.VMEM((tm, SEG_W), F32)] if seq_t is not None else [],
        input_output_aliases=aliases,
        compiler_params=pltpu.CompilerParams(
            dimension_semantics=("arbitrary",), vmem_limit_bytes=_vmem_limit(est)),
        name="inproj",
    )(*args)


def _topk_mask(gate, allowed, topk, axis):
    n = gate.shape[axis]
    pos = lax.broadcasted_iota(jnp.int32, gate.shape, axis).astype(F32)
    g = jnp.where(allowed, gate, NEG_BIG)
    sel = jnp.zeros(gate.shape, F32)
    for _ in range(topk):
        m = jnp.max(g, axis=axis, keepdims=True)
        idx = jnp.min(jnp.where(g == m, pos, float(n)), axis=axis, keepdims=True)
        hit = pos == idx
        sel = jnp.where(hit & allowed, 1.0, sel)
        g = jnp.where(hit, -jnp.inf, g)
    return sel


def _own_lanes(shape, h):
    return lax.broadcasted_iota(jnp.int32, shape, 1) // HEAD_DIM == h % 2


def _flash_step(heads, score_fn, value_fn, s_ref, p_ref, m_ref, l_ref, acc_ref):
    for h in range(heads):
        s_ref[h] = score_fn(h)
    alphas = []
    for h in range(heads):
        m_old = m_ref[h]
        m_new = jnp.maximum(m_old, jnp.max(s_ref[h], axis=0, keepdims=True))
        alpha = jnp.exp(m_old - m_new)
        p = jnp.exp(s_ref[h] - m_new)
        l_ref[h] = alpha * l_ref[h] + jnp.sum(p, axis=0, keepdims=True)
        m_ref[h] = m_new
        p_ref[h] = p.astype(BF16)
        alphas.append(alpha)
    for h in range(heads):
        acc_ref[h] = acc_ref[h] * alphas[h] + _dot(value_fn(h), p_ref[h])


def _flash_init(m_ref, l_ref, acc_ref):
    m_ref[...] = jnp.full(m_ref.shape, -jnp.inf, F32)
    l_ref[...] = jnp.zeros(l_ref.shape, F32)
    acc_ref[...] = jnp.zeros(acc_ref.shape, F32)


def _flash_finish(o_ref, l_ref, acc_ref, heads):
    blk = acc_ref.shape[2]
    parts = [acc_ref[h] / l_ref[h] for h in range(heads)]
    parts += [jnp.zeros((HEAD_DIM, blk), F32)] * (SEG_W // HEAD_DIM - heads)
    o_ref[...] = jnp.concatenate(parts, axis=0).T[:, :o_ref.shape[1]]


def _flash_scratch(heads, blk):
    return [pltpu.VMEM((heads, blk, blk), F32), pltpu.VMEM((heads, blk, blk), BF16),
            pltpu.VMEM((heads, 1, blk), F32), pltpu.VMEM((heads, 1, blk), F32),
            pltpu.VMEM((heads, HEAD_DIM, blk), F32)]


def _moba_prompt_body(q_ref, k_ref, vt_ref, o_ref, kmean_ref, k16_ref, vt16_ref, qm_ref, sel_ref,
                      s_ref, p_ref, m_ref, l_ref, acc_ref, *, blk):
    i = pl.program_id(1)
    nb = k_ref.shape[0] // blk
    scale = HEAD_DIM ** -0.5

    @pl.when(i == 0)
    def _():
        for n in range(nb):
            kb = k_ref[n * blk:(n + 1) * blk, :]
            kmean_ref[n:n + 1, :] = jnp.sum(kb, axis=0, keepdims=True) * (1.0 / blk)
            k16_ref[n * blk:(n + 1) * blk, :] = kb.astype(BF16)
            vt16_ref[n] = vt_ref[:, n * blk:(n + 1) * blk].astype(BF16)

    past = lax.broadcasted_iota(jnp.int32, (nb, blk), 0) < i
    for h in range(A_HEADS):
        c0 = (h // 2) * LANES
        qwin = q_ref[:, c0:c0 + LANES]
        kmean_h = jnp.where(_own_lanes((nb, LANES), h), kmean_ref[:, c0:c0 + LANES], 0.0)
        gate = _dot_nt(kmean_h, qwin, precision=HIGHEST)
        sel_ref[h] = _topk_mask(gate, past, MOBA_TOPK, axis=0)
        qm_ref[h] = jnp.where(_own_lanes((blk, LANES), h), qwin * scale, 0.0).astype(BF16)
    _flash_init(m_ref, l_ref, acc_ref)
    causal = (lax.broadcasted_iota(jnp.int32, (blk, blk), 0)
              <= lax.broadcasted_iota(jnp.int32, (blk, blk), 1))

    def step(n, mask_fn):
        rows = pl.ds(pl.multiple_of(n * blk, blk), blk)
        _flash_step(
            A_HEADS,
            lambda h: mask_fn(h, _dot_nt(k16_ref[rows, (h // 2) * LANES:(h // 2 + 1) * LANES], qm_ref[h])),
            lambda h: vt16_ref[n, h * HEAD_DIM:(h + 1) * HEAD_DIM, :],
            s_ref, p_ref, m_ref, l_ref, acc_ref)

    step(i, lambda h, st: jnp.where(causal, st, NEG_BIG))

    def body(n, carry):
        step(n, lambda h, st: jnp.where(sel_ref[h, pl.ds(n, 1), :] > 0.0, st, NEG_BIG))
        return carry

    lax.fori_loop(0, i, body, 0)
    _flash_finish(o_ref, l_ref, acc_ref, A_HEADS)


def _moba_prompt(q, k, vt, layer, batch, seq):
    blk = MOBA_BLOCK
    assert seq % blk == 0
    nq = seq // blk
    est = 2 * 2 * seq * A_W * 4 + 2 * seq * A_W * 2 + 4 * blk * A_W * 4 + 24 * blk * blk * 4
    return pl.pallas_call(
        functools.partial(_moba_prompt_body, blk=blk),
        out_shape=jax.ShapeDtypeStruct(q.shape, F32),
        grid=(batch, nq),
        in_specs=[pl.BlockSpec((blk, A_W), lambda b, i: (b * nq + i, 0)),
                  pl.BlockSpec((seq, A_W), lambda b, i: (b, 0)),
                  pl.BlockSpec((None, None, A_W, seq), lambda b, i: (layer, b, 0, 0))],
        out_specs=pl.BlockSpec((blk, A_W), lambda b, i: (b * nq + i, 0)),
        scratch_shapes=[pltpu.VMEM((nq, A_W), F32), pltpu.VMEM((seq, A_W), BF16),
                        pltpu.VMEM((nq, A_W, blk), BF16), pltpu.VMEM((A_HEADS, blk, LANES), BF16),
                        pltpu.VMEM((A_HEADS, nq, blk), F32)] + _flash_scratch(A_HEADS, blk),
        compiler_params=pltpu.CompilerParams(
            dimension_semantics=("arbitrary", "arbitrary"), vmem_limit_bytes=_vmem_limit(est)),
        name="moba_prompt",
    )(q, k, vt)


FOX_AUG = 8


EXP_ZERO = 104.0
NORM_SLACK = 1.02


def _fox_prompt_body(q_ref, k_ref, vt_ref, lf_ref, o_ref, bq_ref, kaug_ref, vt16_ref, qaug_ref, c_ref, kn_ref,
                     s_ref, p_ref, m_ref, l_ref, acc_ref, *, blk):
    i = pl.program_id(1)
    nb = k_ref.shape[0] // blk
    scale = HEAD_DIM ** -0.5
    key_i = lax.broadcasted_iota(jnp.int32, (blk, blk), 0)
    qry_i = lax.broadcasted_iota(jnp.int32, (blk, blk), 1)
    lane1 = lax.broadcasted_iota(jnp.int32, (1, LANES), 1)
    head_of_col = (lax.broadcasted_iota(jnp.int32, (SEG_W, LANES), 0) // HEAD_DIM
                   == lax.broadcasted_iota(jnp.int32, (SEG_W, LANES), 1)).astype(BF16)

    def max_norm(x):
        return jnp.sqrt(jnp.max(_dot_sel(x * x, head_of_col), axis=0, keepdims=True))

    @pl.when(i == 0)
    def _():
        tri = (qry_i <= key_i).astype(F32)
        head = lax.broadcasted_iota(jnp.int32, (LANES, LANES), 0)
        lane = lax.broadcasted_iota(jnp.int32, (LANES, LANES), 1)
        place = lambda off: ((lane == head * FOX_AUG + off) & (head < B_HEADS)).astype(BF16)
        used = lane1 < B_HEADS * FOX_AUG
        ones_q = (used & (lane1 % FOX_AUG < 3)).astype(F32)
        ones_k = (used & (lane1 % FOX_AUG >= 3) & (lane1 % FOX_AUG < 6)).astype(F32)
        carry = jnp.zeros((1, LANES), F32)
        for n in range(nb):
            rows = slice(n * blk, (n + 1) * blk)
            c = _dot(tri, lf_ref[rows, :], precision=HIGHEST) + carry
            carry = c[blk - 1:blk, :]
            c_ref[rows, :] = c
            kn_ref[n:n + 1, :] = max_norm(k_ref[rows, :])
            pieces = _bf16_pieces(c)
            bq = ones_q + sum(_dot(pieces[j], place(3 + j)) for j in range(3))
            ak = ones_k - sum(_dot(pieces[j], place(j)) for j in range(3))
            bq_ref[rows, :] = bq.astype(BF16)
            for w in range(SEG_W // LANES):
                kaug_ref[w, rows, :LANES] = k_ref[rows, w * LANES:(w + 1) * LANES].astype(BF16)
                kaug_ref[w, rows, LANES:] = ak.astype(BF16)
            vt16_ref[n] = vt_ref[:, rows].astype(BF16)

    bq_rows = bq_ref[pl.ds(pl.multiple_of(i * blk, blk), blk), :]
    group = lax.broadcasted_iota(jnp.int32, (blk, LANES), 1) // FOX_AUG
    for h in range(B_HEADS):
        c0 = (h // 2) * LANES
        qaug_ref[h, :, :LANES] = jnp.where(_own_lanes((blk, LANES), h), q_ref[:, c0:c0 + LANES] * scale,
                                           0.0).astype(BF16)
        qaug_ref[h, :, LANES:] = jnp.where(group == h, bq_rows, jnp.zeros_like(bq_rows))
    _flash_init(m_ref, l_ref, acc_ref)
    causal = key_i <= qry_i

    def step(n, diag):
        rows = pl.ds(pl.multiple_of(n * blk, blk), blk)

        def scores(h):
            st = _dot_nt(kaug_ref[h // 2, rows, :], qaug_ref[h])
            return jnp.where(causal, st, NEG_BIG) if diag else st

        _flash_step(B_HEADS, scores, lambda h: vt16_ref[n, h * HEAD_DIM:(h + 1) * HEAD_DIM, :],
                    s_ref, p_ref, m_ref, l_ref, acc_ref)

    step(i, True)

    m_low = jnp.zeros((1, LANES), F32)
    for h in range(B_HEADS):
        m_low = jnp.where(lane1 == h, jnp.min(m_ref[h], axis=1, keepdims=True), m_low)
    q_norm = max_norm(q_ref[...] * scale)
    c_first = c_ref[pl.ds(pl.multiple_of(i * blk, blk), 1), :]

    def body(n, carry):
        c_last = c_ref[pl.ds(n * blk + blk - 1, 1), :]
        reach = NORM_SLACK * q_norm * kn_ref[pl.ds(n, 1), :] + (c_first - c_last) - m_low
        live = jnp.max(jnp.where(lane1 < B_HEADS, reach, -jnp.inf)) > -EXP_ZERO

        @pl.when(live)
        def _():
            step(n, False)

        return carry

    lax.fori_loop(0, i, body, 0)
    _flash_finish(o_ref, l_ref, acc_ref, B_HEADS)


def _fox_prompt(q, k, vt, lf, layer, batch, seq, blk=256):
    assert seq % blk == 0
    nq = seq // blk
    est = (2 * seq * (SEG_W + B_W + LANES) * 4 + seq * (3 * 2 * LANES + LANES + B_W) * 2
           + 4 * blk * SEG_W * 4 + 24 * blk * blk * 4)
    return pl.pallas_call(
        functools.partial(_fox_prompt_body, blk=blk),
        out_shape=jax.ShapeDtypeStruct((batch * seq, B_W), F32),
        grid=(batch, nq),
        in_specs=[pl.BlockSpec((blk, SEG_W), lambda b, i: (b * nq + i, 0)),
                  pl.BlockSpec((seq, SEG_W), lambda b, i: (b, 0)),
                  pl.BlockSpec((None, None, B_W, seq), lambda b, i: (layer, b, 0, 0)),
                  pl.BlockSpec((seq, LANES), lambda b, i: (b, 0))],
        out_specs=pl.BlockSpec((blk, B_W), lambda b, i: (b * nq + i, 0)),
        scratch_shapes=[pltpu.VMEM((seq, LANES), BF16), pltpu.VMEM((SEG_W // LANES, seq, 2 * LANES), BF16),
                        pltpu.VMEM((nq, B_W, blk), BF16), pltpu.VMEM((B_HEADS, blk, 2 * LANES), BF16),
                        pltpu.VMEM((seq, LANES), F32), pltpu.VMEM((nq, LANES), F32)]
        + _flash_scratch(B_HEADS, blk),
        compiler_params=pltpu.CompilerParams(
            dimension_semantics=("arbitrary", "arbitrary"), vmem_limit_bytes=_vmem_limit(est)),
        name="fox_prompt",
    )(q, k, vt, lf)


def _hgrn_body(q_ref, k_ref, lf_ref, v_ref, gs_ref, gn_ref, st0_ref, o_ref, sto_ref, st_ref,
               qd_ref, kd_ref, v16_ref, dl_ref, upd_ref, *, sub, tile):
    seq = q_ref.shape[0]
    n_sub = tile // sub
    t_i = lax.broadcasted_iota(jnp.int32, (tile, tile), 0)
    s_i = lax.broadcasted_iota(jnp.int32, (tile, tile), 1)
    same_sub = t_i // sub == s_i // sub
    tri_blk = (same_sub & (s_i <= t_i)).astype(BF16)
    e_i = lax.broadcasted_iota(jnp.int32, (C_W, C_W), 0) // HEAD_DIM
    d_i = lax.broadcasted_iota(jnp.int32, (C_W, C_W), 1) // HEAD_DIM
    same_head = e_i == d_i
    ones_bd = same_head.astype(BF16)
    row_in_sub = lax.broadcasted_iota(jnp.int32, (n_sub, sub, C_W), 1)
    split = lambda a: a.reshape(n_sub, sub, C_W)

    def tile_rows(t):
        return pl.ds(pl.multiple_of(t * tile, tile), tile)

    def decay_pass(t, _):
        rows = tile_rows(t)
        lf, q, k, v = lf_ref[rows, :], q_ref[rows, :], k_ref[rows, :], v_ref[rows, :]
        cum = _sel_dot(tri_blk, lf)
        cum3, q3, k3, v3 = split(cum), split(q), split(k), split(v)
        last3 = cum3[:, sub - 1:sub, :]
        qd_ref[rows, :] = (q * jnp.exp(cum)).astype(BF16)
        kd_ref[rows, :] = (k3 * jnp.exp(last3 - cum3)).reshape(tile, C_W).astype(BF16)
        v16_ref[rows, :] = v.astype(BF16)
        dl_ref[rows, :] = jnp.broadcast_to(jnp.exp(last3), (n_sub, sub, C_W)).reshape(tile, C_W)
        n_t = sub // SUBLANES
        tiles = lambda a: a.reshape(n_sub, n_t, SUBLANES, C_W)
        cum4, q4 = tiles(cum), tiles(q)
        row8 = lax.broadcasted_iota(jnp.int32, (n_sub, SUBLANES, C_W), 1)
        o_tiles = [jnp.zeros((n_sub, SUBLANES, C_W), F32) for _ in range(n_t)]
        for s in range(sub):
            for t in range(s // SUBLANES, n_t):
                dec = jnp.exp(jnp.where(row8 + t * SUBLANES >= s, cum4[:, t] - cum3[:, s:s + 1, :], NEG_BIG))
                m = (q4[:, t] * k3[:, s:s + 1, :] * dec).reshape(n_sub * SUBLANES, C_W).astype(BF16)
                w = jnp.concatenate([_dot(m[:, :MXU_W], ones_bd[:MXU_W, :MXU_W]),
                                     _dot(m[:, MXU_W:], ones_bd[MXU_W:, MXU_W:])], axis=1)
                o_tiles[t] = o_tiles[t] + w.reshape(n_sub, SUBLANES, C_W) * v3[:, s:s + 1, :]
        o_ref[rows, :] = jnp.stack(o_tiles, axis=1).reshape(tile, C_W)
        return 0

    lax.fori_loop(0, seq // tile, decay_pass, 0)
    spread = (lax.broadcasted_iota(jnp.int32, (HEAD_DIM, C_W), 0)
              == lax.broadcasted_iota(jnp.int32, (HEAD_DIM, C_W), 1) % HEAD_DIM).astype(BF16)
    st_ref[...] = jnp.where(same_head, _dot_sel(st0_ref[...], spread), 0.0)

    group = upd_ref.shape[0]

    def chunks(g, _):
        for j in range(group):
            rows = pl.ds(pl.multiple_of((g * group + j) * sub, sub), sub)
            upd_ref[j] = jnp.where(same_head, _dot_tn(v16_ref[rows, :], kd_ref[rows, :]), 0.0)
        for j in range(group):
            c = g * group + j
            rows = pl.ds(pl.multiple_of(c * sub, sub), sub)
            st = st_ref[...]
            o_ref[rows, :] = o_ref[rows, :] + _dot_nt(qd_ref[rows, :], st.astype(BF16))
            st_ref[...] = st * dl_ref[pl.ds(c * sub, 1), :] + upd_ref[j]
        return 0

    lax.fori_loop(0, seq // (sub * group), chunks, 0)

    gn = gn_ref[...]

    def norm_pass(t, _):
        rows = tile_rows(t)
        o = o_ref[rows, :]
        sq = o * o
        hi = sq.astype(BF16)
        lo = (sq - hi.astype(F32)).astype(BF16)
        ms = (_dot(hi, ones_bd) + _dot(lo, ones_bd)) * (1.0 / HEAD_DIM)
        o_ref[rows, :] = o * lax.rsqrt(ms + EPS) * gn * gs_ref[rows, :]
        return 0

    lax.fori_loop(0, seq // tile, norm_pass, 0)
    gather = (lax.broadcasted_iota(jnp.int32, (C_W, HEAD_DIM), 0) % HEAD_DIM
              == lax.broadcasted_iota(jnp.int32, (C_W, HEAD_DIM), 1)).astype(BF16)
    sto_ref[...] = _dot_sel(st_ref[...], gather)


def _hgrn(q, k, lf, v, gs, gn, st0, batch, seq):
    sub = math.gcd(seq, 16)
    tile = min(seq, 256)
    assert seq % tile == 0 and tile % sub == 0
    row = pl.BlockSpec((seq, C_W), lambda b: (b, 0))
    st_spec = pl.BlockSpec((None, C_W, HEAD_DIM), lambda b: (b, 0, 0))
    group = math.gcd(seq // sub, 8)
    est = (2 * 6 * seq * C_W * 4 + (8 + group) * C_W * C_W * 4 + seq * C_W * (3 * 2 + 4)
           + 24 * tile * C_W * 4)
    return pl.pallas_call(
        functools.partial(_hgrn_body, sub=sub, tile=tile),
        out_shape=[jax.ShapeDtypeStruct((batch * seq, C_W), F32),
                   jax.ShapeDtypeStruct((batch, C_W, HEAD_DIM), F32)],
        grid=(batch,),
        in_specs=[row, row, row, row, row, pl.BlockSpec((1, C_W), lambda b: (0, 0)), st_spec],
        out_specs=[row, st_spec],
        scratch_shapes=[pltpu.VMEM((C_W, C_W), F32),
                        pltpu.VMEM((seq, C_W), BF16), pltpu.VMEM((seq, C_W), BF16),
                        pltpu.VMEM((seq, C_W), BF16), pltpu.VMEM((seq, C_W), F32),
                        pltpu.VMEM((group, C_W, C_W), F32)],
        compiler_params=pltpu.CompilerParams(
            dimension_semantics=("arbitrary",), vmem_limit_bytes=_vmem_limit(est)),
        name="hgrn",
    )(q, k, lf, v, gs, gn, st0)


def _block_diag_q(q, heads, scale):
    t, w = q.shape
    rep = jnp.concatenate([q] * heads, axis=0)
    r = lax.broadcasted_iota(jnp.int32, (heads * t, w), 0) // t
    c = lax.broadcasted_iota(jnp.int32, (heads * t, w), 1) // HEAD_DIM
    return jnp.where(r == c, rep * scale, 0.0)


def _collapse_heads(o, heads, t):
    r = lax.broadcasted_iota(jnp.int32, o.shape, 0) // t
    c = lax.broadcasted_iota(jnp.int32, o.shape, 1) // HEAD_DIM
    o = jnp.where(r == c, o, 0.0)
    out = o[0:t, :]
    for h in range(1, heads):
        out = out + o[h * t:(h + 1) * t, :]
    return out


def _rows_per_head(x, heads, t):
    return jnp.concatenate([jnp.broadcast_to(x[h:h + 1, :], (t, x.shape[1])) for h in range(heads)], axis=0)


def _decode_body(pt_ref, q_ref, kn_ref, vn_ref, *rest, heads, moba, pp, n_steps, t_new):
    width = heads * HEAD_DIM
    if moba:
        k_refs, v_refs, rest = rest[:pp], rest[pp:2 * pp], rest[2 * pp:]
    else:
        lfn_ref, rest = rest[0], rest[1:]
        k_refs, v_refs, lf_refs, rest = rest[:pp], rest[pp:2 * pp], rest[2 * pp:3 * pp], rest[3 * pp:]
    o_ref, qbd_ref, aux_ref, m_ref, l_ref, part_ref, s_ref, p_ref = rest
    j = pl.program_id(1)
    page = k_refs[0].shape[-1]
    rows = heads * t_new
    per_blk = MOBA_BLOCK // page
    blk_per_step = pp // per_blk
    n_blk = n_steps * blk_per_step
    scale = HEAD_DIM ** -0.5
    lane = lax.broadcasted_iota(jnp.int32, (rows, LANES), 1)

    @pl.when(j == 0)
    def _():
        qbd_ref[...] = _block_diag_q(q_ref[...], heads, 1.0)
        aux_ref[...] = jnp.zeros(aux_ref.shape, F32)
        m_ref[...] = jnp.zeros(m_ref.shape, F32)
        l_ref[...] = jnp.zeros(l_ref.shape, F32)

    qb = (qbd_ref[...] * scale).astype(BF16)
    m_all, l_all = m_ref[...], l_ref[...]
    aux = aux_ref[...]
    if moba:
        lane_w = lax.broadcasted_iota(jnp.int32, (width, LANES), 1)
    else:
        upper = (lax.broadcasted_iota(jnp.int32, (page, page), 0)
                 <= lax.broadcasted_iota(jnp.int32, (page, page), 1)).astype(BF16)
        c_pages = _dot_sel(jnp.concatenate([r[...] for r in lf_refs], axis=0), upper)
    for jj in range(pp):
        kt = k_refs[jj][...].reshape(width, page)
        s = _dot(qb, kt.astype(BF16))
        if moba:
            ksum = kt if jj % per_blk == 0 else ksum + kt
            if jj % per_blk == per_blk - 1:
                kmean_n = jnp.sum(ksum, axis=1, keepdims=True) * (1.0 / MOBA_BLOCK)
                aux = aux + jnp.where(lane_w == j * blk_per_step + jj // per_blk, kmean_n, 0.0)
        else:
            c_page = c_pages[jj * SUBLANES:(jj + 1) * SUBLANES, :]
            s = s - _rows_per_head(c_page + aux, heads, t_new)
            aux = aux + c_page[:, page - 1:page]
        s_ref[jj] = s
    for g in range(blk_per_step):
        n = j * blk_per_step + g
        pages = range(g * per_blk, (g + 1) * per_blk)
        m_n = functools.reduce(jnp.maximum, [jnp.max(s_ref[jj], axis=1, keepdims=True) for jj in pages])
        l_n = 0.0
        for jj in pages:
            p = jnp.exp(s_ref[jj] - m_n)
            l_n = l_n + jnp.sum(p, axis=1, keepdims=True)
            p_ref[jj] = p.astype(BF16)
        m_all = jnp.where(lane == n, m_n, m_all)
        l_all = jnp.where(lane == n, l_n, l_all)
    for g in range(blk_per_step):
        part_ref[j * blk_per_step + g] = sum(
            _dot_nt(p_ref[jj], v_refs[jj][...].reshape(width, page).astype(BF16))
            for jj in range(g * per_blk, (g + 1) * per_blk))
    m_ref[...] = m_all
    l_ref[...] = l_all
    aux_ref[...] = aux

    @pl.when(j == n_steps - 1)
    def _():
        qi = lax.broadcasted_iota(jnp.int32, (rows, t_new), 0) % t_new
        kj = lax.broadcasted_iota(jnp.int32, (rows, t_new), 1)
        s_own = _dot_nt(qb, kn_ref[...].astype(BF16))
        if moba:
            gate = _dot(qbd_ref[...], aux, precision=HIGHEST)
            sel = _topk_mask(gate, lane < n_blk, MOBA_TOPK, axis=1) > 0.0
        else:
            r8 = lax.broadcasted_iota(jnp.int32, (t_new, t_new), 0)
            c8 = lax.broadcasted_iota(jnp.int32, (t_new, t_new), 1)
            c_new = (_dot(lfn_ref[...], (r8 <= c8).astype(F32), precision=HIGHEST)
                     + aux[:heads, :])
            s_own = s_own - _rows_per_head(c_new, heads, t_new)
            sel = lane < n_blk
        s_own = jnp.where(kj <= qi, s_own, NEG_BIG)
        m_own = jnp.max(s_own, axis=1, keepdims=True)
        m_tot = jnp.maximum(m_own, jnp.max(jnp.where(sel, m_all, -jnp.inf), axis=1, keepdims=True))
        p_own = jnp.exp(s_own - m_tot)
        w = jnp.where(sel, jnp.exp(m_all - m_tot), 0.0)
        l_tot = jnp.sum(p_own, axis=1, keepdims=True) + jnp.sum(w * l_all, axis=1, keepdims=True)
        acc = _dot(p_own.astype(BF16), vn_ref[...].astype(BF16))
        for n in range(n_blk):
            acc = acc + w[:, n:n + 1] * part_ref[n]
        o_ref[...] = _collapse_heads(acc / l_tot, heads, t_new)


def _decode(page_table, q, k_new, v_new, cache_kt, cache_vt, layer, n_seq, t_new, heads, lf=None, pages_per_step=32):
    n_pages = page_table.shape[1]
    pp = min(pages_per_step, n_pages)
    page = cache_kt.shape[-1]
    width = heads * HEAD_DIM
    per_blk = MOBA_BLOCK // page
    assert MOBA_BLOCK % page == 0 and n_pages % pp == 0 and pp % per_blk == 0
    n_steps = n_pages // pp
    n_blk = n_pages // per_blk
    assert n_blk <= LANES
    rows = heads * t_new
    new = pl.BlockSpec((t_new, width), lambda b, j, pt: (b, 0))

    def pages(block, index_fn):
        return [pl.BlockSpec(block, lambda b, j, pt, jj=jj: index_fn(pt[b, j * pp + jj])) for jj in range(pp)]

    kv_pages = lambda: pages((None, None, heads, HEAD_DIM, page), lambda p: (layer, p, 0, 0, 0))
    args, specs = [q, k_new, v_new], [new, new, new]
    if lf is not None:
        args.append(lf[0])
        specs.append(pl.BlockSpec((None, heads, t_new), lambda b, j, pt: (b, 0, 0)))
    args += [cache_kt] * pp + [cache_vt] * pp
    specs += kv_pages() + kv_pages()
    if lf is not None:
        args += [lf[1]] * pp
        specs += pages((None, None, SUBLANES, page), lambda p: (layer, p, 0, 0))
    aux_shape = (SUBLANES, 1) if lf is not None else (width, LANES)
    est = 2 * 2 * pp * page * (width + 8) * 4 + n_blk * rows * width * 4 + 4 * LANES * width * 4
    grid_spec = pltpu.PrefetchScalarGridSpec(
        num_scalar_prefetch=1,
        grid=(n_seq, n_steps),
        in_specs=specs,
        out_specs=new,
        scratch_shapes=[pltpu.VMEM((rows, width), F32), pltpu.VMEM(aux_shape, F32),
                        pltpu.VMEM((rows, LANES), F32), pltpu.VMEM((rows, LANES), F32),
                        pltpu.VMEM((n_blk, rows, width), F32),
                        pltpu.VMEM((pp, rows, page), F32), pltpu.VMEM((pp, rows, page), BF16)],
    )
    return pl.pallas_call(
        functools.partial(_decode_body, heads=heads, moba=lf is None, pp=pp, n_steps=n_steps, t_new=t_new),
        out_shape=jax.ShapeDtypeStruct((n_seq * t_new, width), F32),
        grid_spec=grid_spec,
        compiler_params=pltpu.CompilerParams(
            dimension_semantics=("arbitrary", "arbitrary"), vmem_limit_bytes=_vmem_limit(est)),
        name="moba_decode" if lf is None else "fox_decode",
    )(page_table, *args)


def _pad_w_in(w):
    sizes = (A_W,) * 3 + (B_W,) * 3 + (B_HEADS,) + (C_W,) * 4
    offs = np.cumsum((0,) + sizes)
    segs = [SEG_QA, SEG_KA, SEG_VA, SEG_QF, SEG_KF, SEG_VF, SEG_FF, SEG_QC, SEG_FC, SEG_IC, SEG_GC]
    order = np.argsort(segs)
    ends = sorted(segs)[1:] + [D_IN_PAD]
    wb = w.astype(BF16)
    parts = []
    for i, end in zip(order, ends):
        parts.append(wb[:, :, offs[i]:offs[i] + sizes[i]])
        gap = end - segs[i] - sizes[i]
        if gap:
            parts.append(jnp.zeros(w.shape[:2] + (gap,), BF16))
    return jnp.concatenate(parts, axis=2)


def _rope_tables(pos0, t, reps):
    half = HEAD_DIM // 2
    inv = ROPE_THETA ** (-jnp.arange(half, dtype=F32) * 2.0 / HEAD_DIM)
    ang = (pos0 + jnp.arange(t, dtype=F32))[:, None] * inv[None, :]
    cos, sin, zero = jnp.cos(ang), jnp.sin(ang), jnp.zeros((t, half), F32)
    heads_per_vreg = LANES // HEAD_DIM
    tabs = (jnp.concatenate([cos, cos] * heads_per_vreg, axis=1),
            jnp.concatenate([-sin, zero] * heads_per_vreg, axis=1),
            jnp.concatenate([zero, sin] * heads_per_vreg, axis=1))
    return tuple(jnp.tile(tb, (reps, 1)) for tb in tabs)


def _pad_lanes(a, width):
    return jnp.pad(a, ((0, 0), (0, width - a.shape[1])))


def _state_rows(s):
    b, h, dk, dv = s.shape
    return jnp.swapaxes(s, 2, 3).reshape(b, h * dv, dk)


def _rows_state(st, h):
    b, _, dk = st.shape
    return jnp.swapaxes(st.reshape(b, h, -1, dk), 2, 3)


def kernel(x_prompt, x_sample, cache_moba_k, cache_moba_v, cache_fox_k, cache_fox_v, cache_fox_logf, state_hgrn, page_table, norm_ffn1_g, ffn1_w_gate, ffn1_w_up, ffn1_w_down, norm_mix_g, w_in, fox_f_bias, hgrn_lb_logits, hgrn_norm_g, w_out, norm_ffn2_g, ffn2_w_gate, ffn2_w_up, ffn2_w_down, norm_final_g):
    depth = w_in.shape[0]
    batch, seq, d = x_prompt.shape
    n_seq, t_new, _ = x_sample.shape
    page = cache_moba_k.shape[2]
    past_len = page_table.shape[1] * page

    row = lambda a: a.reshape(1, -1)
    bf = lambda w: w.astype(BF16)
    ffn1 = lambda l: (row(norm_ffn1_g[l]), w1g, w1u, w1d, l)
    w1g, w1u, w1d = bf(ffn1_w_gate), bf(ffn1_w_up), bf(ffn1_w_down)
    w2g, w2u, w2d = bf(ffn2_w_gate), bf(ffn2_w_up), bf(ffn2_w_down)
    wo = bf(w_out)
    w_in_pad = _pad_w_in(w_in)
    fb = [_pad_lanes(row(fox_f_bias[l]), LANES) for l in range(depth)]
    lbl = _pad_lanes(hgrn_lb_logits.astype(F32), SEG_W)
    gn = jnp.tile(hgrn_norm_g, (1, C_HEADS))
    to_pages = lambda c: jnp.transpose(c, (0, 1, 3, 4, 2))
    ckt, cvt, fkt, fvt = (to_pages(c) for c in (cache_moba_k, cache_moba_v, cache_fox_k, cache_fox_v))
    flt = jnp.pad(jnp.swapaxes(cache_fox_logf, 2, 3), ((0, 0), (0, 0), (0, SUBLANES - B_HEADS), (0, 0)))

    def trunk(x, sample):
        if sample:
            nb, t = n_seq, t_new
            tabs = _rope_tables(float(past_len), t, nb)
        else:
            nb, t = batch, seq
            tabs = _rope_tables(0.0, t, 1)
        new, kv_t = [], None
        for l in range(depth):
            if l == 0:
                x = _ffn(x, *ffn1(l))
            args = (x, row(norm_mix_g[l]), w_in_pad, tabs, fb[l], lbl, l)
            heads_last = lambda a, h: a.reshape(nb, t, h, HEAD_DIM)
            if sample:
                qa, ka, va, qf, kf, vf, lf, qc, kc, lc, ic, gc = _inproj(*args)
                oa = _decode(page_table, qa, ka, va, ckt, cvt, l, nb, t, A_HEADS)
                lf_t = jnp.swapaxes(lf[:, :B_HEADS].reshape(nb, t, B_HEADS), 1, 2)
                of = _decode(page_table, qf, kf, vf, fkt, fvt, l, nb, t, B_HEADS, lf=(lf_t, flt))
                st0 = _state_rows(state_hgrn[l].astype(F32))
                rows_kv = (heads_last(ka, A_HEADS), heads_last(va, A_HEADS),
                           heads_last(kf, B_HEADS), heads_last(vf, B_HEADS))
            else:
                qa, ka, ka_t, va_t, qf, kf, kf_t, vf_t, lf, qc, kc, lc, ic, gc = _inproj(
                    *args, seq_t=t, carried=kv_t)
                kv_t = (ka_t, va_t, kf_t, vf_t)
                oa = _moba_prompt(qa, ka, va_t, l, nb, t)
                of = _fox_prompt(qf, kf, vf_t, lf, l, nb, t)
                st0 = jnp.zeros((nb, C_W, HEAD_DIM), F32)
                rows_kv = ()
            oc, st = _hgrn(qc, kc, lc, ic, gc, row(gn[l]), st0, nb, t)
            last = l == depth - 1
            x = _ffn(x, row(norm_ffn2_g[l]), w2g, w2u, w2d, l, mix=(oa, of, oc, wo),
                     g_final=row(norm_final_g) if last else None)
            if not last:
                x = _ffn(x, *ffn1(l + 1))
            new.append(rows_kv + (lf[:, :B_HEADS].reshape(nb, t, B_HEADS), _rows_state(st, C_HEADS)))
        outs = [jnp.stack(z) for z in zip(*new)]
        if not sample:
            heads_of = (A_HEADS, A_HEADS, B_HEADS, B_HEADS)
            outs = [jnp.transpose(a.reshape(depth, nb, h, HEAD_DIM, t), (0, 1, 4, 2, 3))
                    for a, h in zip(kv_t, heads_of)] + outs
        return x.reshape(nb, t, d), outs

    y_p, (mk_p, mv_p, fk_p, fv_p, fl_p, hs_p) = trunk(x_prompt.reshape(batch * seq, d), False)
    y_s, (mk_s, mv_s, fk_s, fv_s, fl_s, hs_s) = trunk(x_sample.reshape(n_seq * t_new, d), True)
    return (y_p, y_s, mk_p, mv_p, fk_p, fv_p, fl_p, hs_p, mk_s, mv_s, fk_s, fv_s, fl_s, hs_s)
```

```python
import functools
import math

import numpy as np
import jax
import jax.numpy as jnp
from jax import lax
from jax.experimental import pallas as pl
from jax.experimental.pallas import tpu as pltpu

F32 = jnp.float32
BF16 = jnp.bfloat16
HIGHEST = lax.Precision.HIGHEST

HEAD_DIM = 64
A_HEADS = 6
B_HEADS = 5
C_HEADS = 5
A_W = A_HEADS * HEAD_DIM
B_W = B_HEADS * HEAD_DIM
C_W = C_HEADS * HEAD_DIM
D_FF_CHUNK = 256
MOBA_BLOCK = 256
MOBA_TOPK = 3
ROPE_THETA = 10000.0
EPS = 1e-6
NEG_BIG = -1e30

LANES = 128
SUBLANES = 8
MXU_W = 256
VMEM_PHYSICAL = 64 * 1024 * 1024
VMEM_CAP = VMEM_PHYSICAL - 8 * 1024 * 1024

SEG_W = 384
SEG_QA, SEG_KA, SEG_VA, SEG_QF, SEG_KF, SEG_VF, SEG_QC, SEG_FC, SEG_IC, SEG_GC = (
    i * SEG_W for i in range(10))
SEG_FF = 10 * SEG_W
D_IN_PAD = SEG_FF + LANES


def _vmem_limit(estimate_bytes):
    return int(min(max(2 * estimate_bytes, 32 * 1024 * 1024), VMEM_CAP))


def _dot(a, b, precision=None):
    return jnp.dot(a, b, preferred_element_type=F32, precision=precision)


def _dot_nt(a, b, precision=None):
    return lax.dot_general(a, b, (((1,), (1,)), ((), ())),
                           preferred_element_type=F32, precision=precision)


def _dot_tn(a, b, precision=None):
    return lax.dot_general(a, b, (((0,), (0,)), ((), ())),
                           preferred_element_type=F32, precision=precision)


def _rms(x, g):
    ms = jnp.mean(x * x, axis=-1, keepdims=True)
    return x * lax.rsqrt(ms + EPS) * g


def _sigmoid(x):
    return 1.0 / (1.0 + jnp.exp(-x))


def _silu(x):
    return x * _sigmoid(x)


def _log_sigmoid(x):
    return jnp.minimum(x, 0.0) - jnp.log1p(jnp.exp(-jnp.abs(x)))


def _bf16_pieces(x):
    x1 = x.astype(BF16)
    r1 = x - x1.astype(F32)
    x2 = r1.astype(BF16)
    return x1, x2, (r1 - x2.astype(F32)).astype(BF16)


def _dot_sel(x, sel):
    return sum(_dot(p, sel) for p in _bf16_pieces(x))


def _sel_dot(sel, x):
    return sum(_dot(sel, p) for p in _bf16_pieces(x))


def _ffn_body(*refs, has_mix, has_final, d_ff):
    it = iter(refs)
    x_ref = next(it)
    if has_mix:
        oa_ref, of_ref, oc_ref, wo_ref = (next(it) for _ in range(4))
    g_ref, wg_ref, wu_ref, wd_ref = (next(it) for _ in range(4))
    gf_ref = next(it) if has_final else None
    out_ref = next(it)

    x = x_ref[...]
    if has_mix:
        x = (x + _dot(oa_ref[...].astype(BF16), wo_ref[:A_W, :])
             + _dot(of_ref[...].astype(BF16), wo_ref[A_W:A_W + B_W, :])
             + _dot(oc_ref[...].astype(BF16), wo_ref[A_W + B_W:, :]))
    h = _rms(x, g_ref[...]).astype(BF16)
    acc = jnp.zeros(x.shape, F32)
    for c0 in range(0, d_ff, D_FF_CHUNK):
        gate = _dot(h, wg_ref[:, c0:c0 + D_FF_CHUNK])
        up = _dot(h, wu_ref[:, c0:c0 + D_FF_CHUNK])
        act = (_silu(gate) * up).astype(BF16)
        acc = acc + _dot(act, wd_ref[c0:c0 + D_FF_CHUNK, :])
    y = x + 0.5 * acc
    if has_final:
        y = _rms(y, gf_ref[...])
    out_ref[...] = y


def _layer_spec(a, layer):
    return pl.BlockSpec((None,) + a.shape[1:], lambda *_: (layer,) + (0,) * (a.ndim - 1),
                        pipeline_mode=pl.Buffered(1))


def _ffn(x, g, wg, wu, wd, layer, mix=None, g_final=None, tm=512):
    n, d = x.shape
    d_ff = wg.shape[2]
    tm = min(tm, n)
    assert n % tm == 0 and d_ff % D_FF_CHUNK == 0
    row = lambda w: pl.BlockSpec((tm, w), lambda i: (i, 0))
    full = lambda a: pl.BlockSpec(a.shape, lambda i: (0,) * a.ndim, pipeline_mode=pl.Buffered(1))
    args, specs = [x], [row(d)]
    if mix is not None:
        oa, of, oc, wo = mix
        args += [oa, of, oc, wo]
        specs += [row(oa.shape[1]), row(of.shape[1]), row(oc.shape[1]), _layer_spec(wo, layer)]
    args += [g, wg, wu, wd]
    specs += [full(g), _layer_spec(wg, layer), _layer_spec(wu, layer), _layer_spec(wd, layer)]
    if g_final is not None:
        args.append(g_final)
        specs.append(full(g_final))
    weights = 2 * 3 * d * d_ff + (2 * d * d if mix is not None else 0)
    tiles = 2 * 2 * tm * d * 4 * (2 if mix is not None else 1) + 6 * tm * d * 4
    return pl.pallas_call(
        functools.partial(_ffn_body, has_mix=mix is not None, has_final=g_final is not None, d_ff=d_ff),
        out_shape=jax.ShapeDtypeStruct((n, d), F32),
        grid=(n // tm,),
        in_specs=specs,
        out_specs=row(d),
        compiler_params=pltpu.CompilerParams(
            dimension_semantics=("arbitrary",), vmem_limit_bytes=_vmem_limit(weights + tiles)),
        name="ffn_mix" if mix is not None else "ffn",
    )(*args)


def _inproj_body(x_ref, g_ref, w_ref, cos_ref, slo_ref, shi_ref, fb_ref, lbl_ref, *out_refs, layer, transposed,
                 n_carried):
    out_refs = out_refs[n_carried:]
    if transposed:
        (qa_ref, ka_ref, kat_ref, vat_ref, qf_ref, kf_ref, kft_ref, vft_ref, lf_ref,
         qc_ref, kc_ref, lc_ref, ic_ref, gc_ref, stage_ref) = out_refs
    else:
        (qa_ref, ka_ref, va_ref, qf_ref, kf_ref, vf_ref, lf_ref,
         qc_ref, kc_ref, lc_ref, ic_ref, gc_ref) = out_refs
    h = _rms(x_ref[...], g_ref[...]).astype(BF16)

    def proj(c0, width=SEG_W):
        return _dot(h, w_ref[:, c0:c0 + width])

    cos, slo, shi = cos_ref[...], slo_ref[...], shi_ref[...]

    def rope(p):
        parts = []
        for c in range(SEG_W // LANES):
            pc = p[:, c * LANES:(c + 1) * LANES]
            parts.append(pc * cos + pltpu.roll(pc, LANES - HEAD_DIM // 2, 1) * slo
                         + pltpu.roll(pc, HEAD_DIM // 2, 1) * shi)
        return jnp.concatenate(parts, axis=1)

    qa_ref[...] = rope(proj(SEG_QA))
    ka = rope(proj(SEG_KA))
    ka_ref[...] = ka
    if transposed:
        def transposed_tile(p):
            stage_ref[...] = p
            return stage_ref[...].T

        kat_ref[...] = ka.T
        vat_ref[...] = transposed_tile(proj(SEG_VA))
        qf_ref[...] = proj(SEG_QF)
        kf_ref[...] = proj(SEG_KF)
        kft_ref[...] = kf_ref[...].T[:B_W, :]
        vft_ref[...] = transposed_tile(proj(SEG_VF))[:B_W, :]
    else:
        va_ref[...] = proj(SEG_VA)
        qf_ref[...] = proj(SEG_QF)[:, :B_W]
        kf_ref[...] = proj(SEG_KF)[:, :B_W]
        vf_ref[...] = proj(SEG_VF)[:, :B_W]
    lf_ref[...] = _log_sigmoid(proj(SEG_FF, LANES) + fb_ref[...])

    lbl = lbl_ref[...]
    e = jnp.exp(lbl - jnp.max(lbl, axis=0, keepdims=True))
    psm = e / jnp.sum(e, axis=0, keepdims=True)
    lb = jnp.zeros((1, SEG_W), F32)
    for j in range(1, layer + 1):
        lb = lb + psm[j:j + 1, :]
    fc = proj(SEG_FC)
    lc_ref[...] = jnp.log(lb + (1.0 - lb) * _sigmoid(fc))[:, :C_W]
    kc_ref[...] = ((1.0 - lb) * _sigmoid(-fc))[:, :C_W]
    qc_ref[...] = _silu(proj(SEG_QC))[:, :C_W]
    ic_ref[...] = proj(SEG_IC)[:, :C_W]
    gc_ref[...] = _silu(proj(SEG_GC))[:, :C_W]


def _inproj(x, g, w_pad, rope_tabs, fb, lbl, layer, seq_t=None, carried=None, tm=512):
    n, d = x.shape
    depth = w_pad.shape[0]
    tm = min(tm, n)
    assert n % tm == 0 and rope_tabs[0].shape[0] % tm == 0
    row = lambda w: pl.BlockSpec((tm, w), lambda i: (i, 0))
    full = lambda a: pl.BlockSpec(a.shape, lambda i: (0,) * a.ndim, pipeline_mode=pl.Buffered(1))
    n_tab = rope_tabs[0].shape[0] // tm
    tab = pl.BlockSpec((tm, LANES), lambda i: (i % n_tab, 0))
    args = [x, g, w_pad, *rope_tabs, fb, lbl]
    in_specs = [row(d), full(g), _layer_spec(w_pad, layer), tab, tab, tab, full(fb), full(lbl)]
    aliases = {}
    if seq_t is None:
        widths = [A_W, A_W, A_W, B_W, B_W, B_W, LANES, C_W, C_W, C_W, C_W, C_W]
        shapes = [(n, w) for w in widths]
        out_specs = [row(w) for w in widths]
    else:
        assert seq_t % tm == 0
        per_seq = seq_t // tm
        col = lambda w: pl.BlockSpec((None, None, w, tm), lambda i: (layer, i // per_seq, 0, i % per_seq))
        kinds = [(row, A_W), (row, A_W), (col, A_W), (col, A_W), (row, SEG_W), (row, SEG_W), (col, B_W),
                 (col, B_W), (row, LANES), (row, C_W), (row, C_W), (row, C_W), (row, C_W), (row, C_W)]
        widths = [w for _, w in kinds]
        shapes = [(n, w) if f is row else (depth, n // seq_t, w, seq_t) for f, w in kinds]
        out_specs = [f(w) for f, w in kinds]
        if carried is not None:
            stacked = [i for i, (f, _) in enumerate(kinds) if f is col]
            aliases = {len(args) + k: i for k, i in enumerate(stacked)}
            args += list(carried)
            in_specs += [pl.BlockSpec(memory_space=pl.ANY)] * len(carried)
    est = 2 * d * D_IN_PAD + 2 * tm * 4 * (d + sum(widths) + 3 * LANES) + 8 * tm * SEG_W * 4
    return pl.pallas_call(
        functools.partial(_inproj_body, layer=layer, transposed=seq_t is not None, n_carried=len(aliases)),
        out_shape=[jax.ShapeDtypeStruct(s, F32) for s in shapes],
        grid=(n // tm,),
        in_specs=in_specs,
        out_specs=out_specs,
        scratch_shapes=[pltpu.VMEM((tm, SEG_W), F32)] if seq_t is not None else [],
        input_output_aliases=aliases,
        compiler_params=pltpu.CompilerParams(
            dimension_semantics=("arbitrary",), vmem_limit_bytes=_vmem_limit(est)),
        name="inproj",
    )(*args)


def _topk_mask(gate, allowed, topk, axis):
    n = gate.shape[axis]
    pos = lax.broadcasted_iota(jnp.int32, gate.shape, axis).astype(F32)
    g = jnp.where(allowed, gate, NEG_BIG)
    sel = jnp.zeros(gate.shape, F32)
    for _ in range(topk):
        m = jnp.max(g, axis=axis, keepdims=True)
        idx = jnp.min(jnp.where(g == m, pos, float(n)), axis=axis, keepdims=True)
        hit = pos == idx
        sel = jnp.where(hit & allowed, 1.0, sel)
        g = jnp.where(hit, -jnp.inf, g)
    return sel


def _own_lanes(shape, h):
    return lax.broadcasted_iota(jnp.int32, shape, 1) // HEAD_DIM == h % 2


def _flash_step(heads, score_fn, value_fn, s_ref, p_ref, m_ref, l_ref, acc_ref):
    for h in range(heads):
        s_ref[h] = score_fn(h)
    alphas = []
    for h in range(heads):
        m_old = m_ref[h]
        m_new = jnp.maximum(m_old, jnp.max(s_ref[h], axis=0, keepdims=True))
        alpha = jnp.exp(m_old - m_new)
        p = jnp.exp(s_ref[h] - m_new)
        l_ref[h] = alpha * l_ref[h] + jnp.sum(p, axis=0, keepdims=True)
        m_ref[h] = m_new
        p_ref[h] = p.astype(BF16)
        alphas.append(alpha)
    for h in range(heads):
        acc_ref[h] = acc_ref[h] * alphas[h] + _dot(value_fn(h), p_ref[h])


def _flash_init(m_ref, l_ref, acc_ref):
    m_ref[...] = jnp.full(m_ref.shape, -jnp.inf, F32)
    l_ref[...] = jnp.zeros(l_ref.shape, F32)
    acc_ref[...] = jnp.zeros(acc_ref.shape, F32)


def _flash_finish(o_ref, l_ref, acc_ref, heads):
    blk = acc_ref.shape[2]
    parts = [acc_ref[h] / l_ref[h] for h in range(heads)]
    parts += [jnp.zeros((HEAD_DIM, blk), F32)] * (SEG_W // HEAD_DIM - heads)
    o_ref[...] = jnp.concatenate(parts, axis=0).T[:, :o_ref.shape[1]]


def _flash_scratch(heads, blk):
    return [pltpu.VMEM((heads, blk, blk), F32), pltpu.VMEM((heads, blk, blk), BF16),
            pltpu.VMEM((heads, 1, blk), F32), pltpu.VMEM((heads, 1, blk), F32),
            pltpu.VMEM((heads, HEAD_DIM, blk), F32)]


def _moba_prompt_body(q_ref, k_ref, vt_ref, o_ref, kmean_ref, k16_ref, vt16_ref, qm_ref, sel_ref,
                      s_ref, p_ref, m_ref, l_ref, acc_ref, *, blk):
    i = pl.program_id(1)
    nb = k_ref.shape[0] // blk
    scale = HEAD_DIM ** -0.5

    @pl.when(i == 0)
    def _():
        for n in range(nb):
            kb = k_ref[n * blk:(n + 1) * blk, :]
            kmean_ref[n:n + 1, :] = jnp.sum(kb, axis=0, keepdims=True) * (1.0 / blk)
            k16_ref[n * blk:(n + 1) * blk, :] = kb.astype(BF16)
            vt16_ref[n] = vt_ref[:, n * blk:(n + 1) * blk].astype(BF16)

    past = lax.broadcasted_iota(jnp.int32, (nb, blk), 0) < i
    for h in range(A_HEADS):
        c0 = (h // 2) * LANES
        qwin = q_ref[:, c0:c0 + LANES]
        kmean_h = jnp.where(_own_lanes((nb, LANES), h), kmean_ref[:, c0:c0 + LANES], 0.0)
        gate = _dot_nt(kmean_h, qwin, precision=HIGHEST)
        sel_ref[h] = _topk_mask(gate, past, MOBA_TOPK, axis=0)
        qm_ref[h] = jnp.where(_own_lanes((blk, LANES), h), qwin * scale, 0.0).astype(BF16)
    _flash_init(m_ref, l_ref, acc_ref)
    causal = (lax.broadcasted_iota(jnp.int32, (blk, blk), 0)
              <= lax.broadcasted_iota(jnp.int32, (blk, blk), 1))

    def step(n, mask_fn):
        rows = pl.ds(pl.multiple_of(n * blk, blk), blk)
        _flash_step(
            A_HEADS,
            lambda h: mask_fn(h, _dot_nt(k16_ref[rows, (h // 2) * LANES:(h // 2 + 1) * LANES], qm_ref[h])),
            lambda h: vt16_ref[n, h * HEAD_DIM:(h + 1) * HEAD_DIM, :],
            s_ref, p_ref, m_ref, l_ref, acc_ref)

    step(i, lambda h, st: jnp.where(causal, st, NEG_BIG))

    def body(n, carry):
        step(n, lambda h, st: jnp.where(sel_ref[h, pl.ds(n, 1), :] > 0.0, st, NEG_BIG))
        return carry

    lax.fori_loop(0, i, body, 0)
    _flash_finish(o_ref, l_ref, acc_ref, A_HEADS)


def _moba_prompt(q, k, vt, layer, batch, seq):
    blk = MOBA_BLOCK
    assert seq % blk == 0
    nq = seq // blk
    est = 2 * 2 * seq * A_W * 4 + 2 * seq * A_W * 2 + 4 * blk * A_W * 4 + 24 * blk * blk * 4
    return pl.pallas_call(
        functools.partial(_moba_prompt_body, blk=blk),
        out_shape=jax.ShapeDtypeStruct(q.shape, F32),
        grid=(batch, nq),
        in_specs=[pl.BlockSpec((blk, A_W), lambda b, i: (b * nq + i, 0)),
                  pl.BlockSpec((seq, A_W), lambda b, i: (b, 0)),
                  pl.BlockSpec((None, None, A_W, seq), lambda b, i: (layer, b, 0, 0))],
        out_specs=pl.BlockSpec((blk, A_W), lambda b, i: (b * nq + i, 0)),
        scratch_shapes=[pltpu.VMEM((nq, A_W), F32), pltpu.VMEM((seq, A_W), BF16),
                        pltpu.VMEM((nq, A_W, blk), BF16), pltpu.VMEM((A_HEADS, blk, LANES), BF16),
                        pltpu.VMEM((A_HEADS, nq, blk), F32)] + _flash_scratch(A_HEADS, blk),
        compiler_params=pltpu.CompilerParams(
            dimension_semantics=("arbitrary", "arbitrary"), vmem_limit_bytes=_vmem_limit(est)),
        name="moba_prompt",
    )(q, k, vt)


FOX_AUG = 8


EXP_ZERO = 104.0
NORM_SLACK = 1.02


def _fox_prompt_body(q_ref, k_ref, vt_ref, lf_ref, o_ref, bq_ref, kaug_ref, vt16_ref, qaug_ref, c_ref, kn_ref,
                     s_ref, p_ref, m_ref, l_ref, acc_ref, *, blk):
    i = pl.program_id(1)
    nb = k_ref.shape[0] // blk
    scale = HEAD_DIM ** -0.5
    key_i = lax.broadcasted_iota(jnp.int32, (blk, blk), 0)
    qry_i = lax.broadcasted_iota(jnp.int32, (blk, blk), 1)
    lane1 = lax.broadcasted_iota(jnp.int32, (1, LANES), 1)
    head_of_col = (lax.broadcasted_iota(jnp.int32, (SEG_W, LANES), 0) // HEAD_DIM
                   == lax.broadcasted_iota(jnp.int32, (SEG_W, LANES), 1)).astype(BF16)

    def max_norm(x):
        return jnp.sqrt(jnp.max(_dot_sel(x * x, head_of_col), axis=0, keepdims=True))

    @pl.when(i == 0)
    def _():
        tri = (qry_i <= key_i).astype(F32)
        head = lax.broadcasted_iota(jnp.int32, (LANES, LANES), 0)
        lane = lax.broadcasted_iota(jnp.int32, (LANES, LANES), 1)
        place = lambda off: ((lane == head * FOX_AUG + off) & (head < B_HEADS)).astype(BF16)
        used = lane1 < B_HEADS * FOX_AUG
        ones_q = (used & (lane1 % FOX_AUG < 3)).astype(F32)
        ones_k = (used & (lane1 % FOX_AUG >= 3) & (lane1 % FOX_AUG < 6)).astype(F32)
        carry = jnp.zeros((1, LANES), F32)
        for n in range(nb):
            rows = slice(n * blk, (n + 1) * blk)
            c = _dot(tri, lf_ref[rows, :], precision=HIGHEST) + carry
            carry = c[blk - 1:blk, :]
            c_ref[rows, :] = c
            kn_ref[n:n + 1, :] = max_norm(k_ref[rows, :])
            pieces = _bf16_pieces(c)
            bq = ones_q + sum(_dot(pieces[j], place(3 + j)) for j in range(3))
            ak = ones_k - sum(_dot(pieces[j], place(j)) for j in range(3))
            bq_ref[rows, :] = bq.astype(BF16)
            for w in range(SEG_W // LANES):
                kaug_ref[w, rows, :LANES] = k_ref[rows, w * LANES:(w + 1) * LANES].astype(BF16)
                kaug_ref[w, rows, LANES:] = ak.astype(BF16)
            vt16_ref[n] = vt_ref[:, rows].astype(BF16)

    bq_rows = bq_ref[pl.ds(pl.multiple_of(i * blk, blk), blk), :]
    group = lax.broadcasted_iota(jnp.int32, (blk, LANES), 1) // FOX_AUG
    for h in range(B_HEADS):
        c0 = (h // 2) * LANES
        qaug_ref[h, :, :LANES] = jnp.where(_own_lanes((blk, LANES), h), q_ref[:, c0:c0 + LANES] * scale,
                                           0.0).astype(BF16)
        qaug_ref[h, :, LANES:] = jnp.where(group == h, bq_rows, jnp.zeros_like(bq_rows))
    _flash_init(m_ref, l_ref, acc_ref)
    causal = key_i <= qry_i

    def step(n, diag):
        rows = pl.ds(pl.multiple_of(n * blk, blk), blk)

        def scores(h):
            st = _dot_nt(kaug_ref[h // 2, rows, :], qaug_ref[h])
            return jnp.where(causal, st, NEG_BIG) if diag else st

        _flash_step(B_HEADS, scores, lambda h: vt16_ref[n, h * HEAD_DIM:(h + 1) * HEAD_DIM, :],
                    s_ref, p_ref, m_ref, l_ref, acc_ref)

    step(i, True)

    m_low = jnp.zeros((1, LANES), F32)
    for h in range(B_HEADS):
        m_low = jnp.where(lane1 == h, jnp.min(m_ref[h], axis=1, keepdims=True), m_low)
    q_norm = max_norm(q_ref[...] * scale)
    c_first = c_ref[pl.ds(pl.multiple_of(i * blk, blk), 1), :]

    def body(n, carry):
        c_last = c_ref[pl.ds(n * blk + blk - 1, 1), :]
        reach = NORM_SLACK * q_norm * kn_ref[pl.ds(n, 1), :] + (c_first - c_last) - m_low
        live = jnp.max(jnp.where(lane1 < B_HEADS, reach, -jnp.inf)) > -EXP_ZERO

        @pl.when(live)
        def _():
            step(n, False)

        return carry

    lax.fori_loop(0, i, body, 0)
    _flash_finish(o_ref, l_ref, acc_ref, B_HEADS)


def _fox_prompt(q, k, vt, lf, layer, batch, seq, blk=256):
    assert seq % blk == 0
    nq = seq // blk
    est = (2 * seq * (SEG_W + B_W + LANES) * 4 + seq * (3 * 2 * LANES + LANES + B_W) * 2
           + 4 * blk * SEG_W * 4 + 24 * blk * blk * 4)
    return pl.pallas_call(
        functools.partial(_fox_prompt_body, blk=blk),
        out_shape=jax.ShapeDtypeStruct((batch * seq, B_W), F32),
        grid=(batch, nq),
        in_specs=[pl.BlockSpec((blk, SEG_W), lambda b, i: (b * nq + i, 0)),
                  pl.BlockSpec((seq, SEG_W), lambda b, i: (b, 0)),
                  pl.BlockSpec((None, None, B_W, seq), lambda b, i: (layer, b, 0, 0)),
                  pl.BlockSpec((seq, LANES), lambda b, i: (b, 0))],
        out_specs=pl.BlockSpec((blk, B_W), lambda b, i: (b * nq + i, 0)),
        scratch_shapes=[pltpu.VMEM((seq, LANES), BF16), pltpu.VMEM((SEG_W // LANES, seq, 2 * LANES), BF16),
                        pltpu.VMEM((nq, B_W, blk), BF16), pltpu.VMEM((B_HEADS, blk, 2 * LANES), BF16),
                        pltpu.VMEM((seq, LANES), F32), pltpu.VMEM((nq, LANES), F32)]
        + _flash_scratch(B_HEADS, blk),
        compiler_params=pltpu.CompilerParams(
            dimension_semantics=("arbitrary", "arbitrary"), vmem_limit_bytes=_vmem_limit(est)),
        name="fox_prompt",
    )(q, k, vt, lf)


def _hgrn_body(q_ref, k_ref, lf_ref, v_ref, gs_ref, gn_ref, st0_ref, o_ref, sto_ref, st_ref,
               qd_ref, kd_ref, v16_ref, dl_ref, upd_ref, *, sub, tile):
    seq = q_ref.shape[0]
    n_sub = tile // sub
    t_i = lax.broadcasted_iota(jnp.int32, (tile, tile), 0)
    s_i = lax.broadcasted_iota(jnp.int32, (tile, tile), 1)
    same_sub = t_i // sub == s_i // sub
    tri_blk = (same_sub & (s_i <= t_i)).astype(BF16)
    e_i = lax.broadcasted_iota(jnp.int32, (C_W, C_W), 0) // HEAD_DIM
    d_i = lax.broadcasted_iota(jnp.int32, (C_W, C_W), 1) // HEAD_DIM
    same_head = e_i == d_i
    ones_bd = same_head.astype(BF16)
    row_in_sub = lax.broadcasted_iota(jnp.int32, (n_sub, sub, C_W), 1)
    split = lambda a: a.reshape(n_sub, sub, C_W)

    def tile_rows(t):
        return pl.ds(pl.multiple_of(t * tile, tile), tile)

    def decay_pass(t, _):
        rows = tile_rows(t)
        lf, q, k, v = lf_ref[rows, :], q_ref[rows, :], k_ref[rows, :], v_ref[rows, :]
        cum = _sel_dot(tri_blk, lf)
        cum3, q3, k3, v3 = split(cum), split(q), split(k), split(v)
        last3 = cum3[:, sub - 1:sub, :]
        qd_ref[rows, :] = (q * jnp.exp(cum)).astype(BF16)
        kd_ref[rows, :] = (k3 * jnp.exp(last3 - cum3)).reshape(tile, C_W).astype(BF16)
        v16_ref[rows, :] = v.astype(BF16)
        dl_ref[rows, :] = jnp.broadcast_to(jnp.exp(last3), (n_sub, sub, C_W)).reshape(tile, C_W)
        n_t = sub // SUBLANES
        tiles = lambda a: a.reshape(n_sub, n_t, SUBLANES, C_W)
        cum4, q4 = tiles(cum), tiles(q)
        row8 = lax.broadcasted_iota(jnp.int32, (n_sub, SUBLANES, C_W), 1)
        o_tiles = [jnp.zeros((n_sub, SUBLANES, C_W), F32) for _ in range(n_t)]
        for s in range(sub):
            for t in range(s // SUBLANES, n_t):
                dec = jnp.exp(jnp.where(row8 + t * SUBLANES >= s, cum4[:, t] - cum3[:, s:s + 1, :], NEG_BIG))
                m = (q4[:, t] * k3[:, s:s + 1, :] * dec).reshape(n_sub * SUBLANES, C_W).astype(BF16)
                w = jnp.concatenate([_dot(m[:, :MXU_W], ones_bd[:MXU_W, :MXU_W]),
                                     _dot(m[:, MXU_W:], ones_bd[MXU_W:, MXU_W:])], axis=1)
                o_tiles[t] = o_tiles[t] + w.reshape(n_sub, SUBLANES, C_W) * v3[:, s:s + 1, :]
        o_ref[rows, :] = jnp.stack(o_tiles, axis=1).reshape(tile, C_W)
        return 0

    lax.fori_loop(0, seq // tile, decay_pass, 0)
    spread = (lax.broadcasted_iota(jnp.int32, (HEAD_DIM, C_W), 0)
              == lax.broadcasted_iota(jnp.int32, (HEAD_DIM, C_W), 1) % HEAD_DIM).astype(BF16)
    st_ref[...] = jnp.where(same_head, _dot_sel(st0_ref[...], spread), 0.0)

    group = upd_ref.shape[0]

    def chunks(g, _):
        for j in range(group):
            rows = pl.ds(pl.multiple_of((g * group + j) * sub, sub), sub)
            upd_ref[j] = jnp.where(same_head, _dot_tn(v16_ref[rows, :], kd_ref[rows, :]), 0.0)
        for j in range(group):
            c = g * group + j
            rows = pl.ds(pl.multiple_of(c * sub, sub), sub)
            st = st_ref[...]
            o_ref[rows, :] = o_ref[rows, :] + _dot_nt(qd_ref[rows, :], st.astype(BF16))
            st_ref[...] = st * dl_ref[pl.ds(c * sub, 1), :] + upd_ref[j]
        return 0

    lax.fori_loop(0, seq // (sub * group), chunks, 0)

    gn = gn_ref[...]

    def norm_pass(t, _):
        rows = tile_rows(t)
        o = o_ref[rows, :]
        sq = o * o
        hi = sq.astype(BF16)
        lo = (sq - hi.astype(F32)).astype(BF16)
        ms = (_dot(hi, ones_bd) + _dot(lo, ones_bd)) * (1.0 / HEAD_DIM)
        o_ref[rows, :] = o * lax.rsqrt(ms + EPS) * gn * gs_ref[rows, :]
        return 0

    lax.fori_loop(0, seq // tile, norm_pass, 0)
    gather = (lax.broadcasted_iota(jnp.int32, (C_W, HEAD_DIM), 0) % HEAD_DIM
              == lax.broadcasted_iota(jnp.int32, (C_W, HEAD_DIM), 1)).astype(BF16)
    sto_ref[...] = _dot_sel(st_ref[...], gather)


def _hgrn(q, k, lf, v, gs, gn, st0, batch, seq):
    sub = math.gcd(seq, 16)
    tile = min(seq, 256)
    assert seq % tile == 0 and tile % sub == 0
    row = pl.BlockSpec((seq, C_W), lambda b: (b, 0))
    st_spec = pl.BlockSpec((None, C_W, HEAD_DIM), lambda b: (b, 0, 0))
    group = math.gcd(seq // sub, 8)
    est = (2 * 6 * seq * C_W * 4 + (8 + group) * C_W * C_W * 4 + seq * C_W * (3 * 2 + 4)
           + 24 * tile * C_W * 4)
    return pl.pallas_call(
        functools.partial(_hgrn_body, sub=sub, tile=tile),
        out_shape=[jax.ShapeDtypeStruct((batch * seq, C_W), F32),
                   jax.ShapeDtypeStruct((batch, C_W, HEAD_DIM), F32)],
        grid=(batch,),
        in_specs=[row, row, row, row, row, pl.BlockSpec((1, C_W), lambda b: (0, 0)), st_spec],
        out_specs=[row, st_spec],
        scratch_shapes=[pltpu.VMEM((C_W, C_W), F32),
                        pltpu.VMEM((seq, C_W), BF16), pltpu.VMEM((seq, C_W), BF16),
                        pltpu.VMEM((seq, C_W), BF16), pltpu.VMEM((seq, C_W), F32),
                        pltpu.VMEM((group, C_W, C_W), F32)],
        compiler_params=pltpu.CompilerParams(
            dimension_semantics=("arbitrary",), vmem_limit_bytes=_vmem_limit(est)),
        name="hgrn",
    )(q, k, lf, v, gs, gn, st0)


def _block_diag_q(q, heads, scale):
    t, w = q.shape
    rep = jnp.concatenate([q] * heads, axis=0)
    r = lax.broadcasted_iota(jnp.int32, (heads * t, w), 0) // t
    c = lax.broadcasted_iota(jnp.int32, (heads * t, w), 1) // HEAD_DIM
    return jnp.where(r == c, rep * scale, 0.0)


def _collapse_heads(o, heads, t):
    r = lax.broadcasted_iota(jnp.int32, o.shape, 0) // t
    c = lax.broadcasted_iota(jnp.int32, o.shape, 1) // HEAD_DIM
    o = jnp.where(r == c, o, 0.0)
    out = o[0:t, :]
    for h in range(1, heads):
        out = out + o[h * t:(h + 1) * t, :]
    return out


def _rows_per_head(x, heads, t):
    return jnp.concatenate([jnp.broadcast_to(x[h:h + 1, :], (t, x.shape[1])) for h in range(heads)], axis=0)


def _decode_body(pt_ref, q_ref, kn_ref, vn_ref, *rest, heads, moba, pp, n_steps, t_new):
    width = heads * HEAD_DIM
    if moba:
        k_refs, v_refs, rest = rest[:pp], rest[pp:2 * pp], rest[2 * pp:]
    else:
        lfn_ref, rest = rest[0], rest[1:]
        k_refs, v_refs, lf_pool_ref, rest = rest[:pp], rest[pp:2 * pp], rest[2 * pp], rest[2 * pp + 1:]
        lf_ref, rest = rest[-1], rest[:-1]
    o_ref, qbd_ref, aux_ref, m_ref, l_ref, part_ref, s_ref, p_ref = rest
    j = pl.program_id(1)
    page = k_refs[0].shape[-1]
    rows = heads * t_new
    per_blk = MOBA_BLOCK // page
    blk_per_step = pp // per_blk
    n_blk = n_steps * blk_per_step
    scale = HEAD_DIM ** -0.5
    lane = lax.broadcasted_iota(jnp.int32, (rows, LANES), 1)

    @pl.when(j == 0)
    def _():
        qbd_ref[...] = _block_diag_q(q_ref[...], heads, 1.0)
        aux_ref[...] = jnp.zeros(aux_ref.shape, F32)
        m_ref[...] = jnp.zeros(m_ref.shape, F32)
        l_ref[...] = jnp.zeros(l_ref.shape, F32)
        if not moba:
            lf_ref[...] = jnp.zeros(lf_ref.shape, F32)

    qb = (qbd_ref[...] * scale).astype(BF16)
    m_all, l_all = m_ref[...], l_ref[...]
    aux = aux_ref[...]
    if moba:
        lane_w = lax.broadcasted_iota(jnp.int32, (width, LANES), 1)
    else:
        seq_i = pl.program_id(0)
        for jj in range(pp):
            page_id = pt_ref[seq_i, j * pp + jj]
            for h in range(heads):
                lf_ref[jj * SUBLANES + h:jj * SUBLANES + h + 1, :] = lf_pool_ref[h, pl.ds(page_id, 1), :]
        upper = (lax.broadcasted_iota(jnp.int32, (page, page), 0)
                 <= lax.broadcasted_iota(jnp.int32, (page, page), 1)).astype(BF16)
        c_pages = _dot_sel(lf_ref[...], upper)
    for jj in range(pp):
        kt = k_refs[jj][...].reshape(width, page)
        s = _dot(qb, kt.astype(BF16))
        if moba:
            ksum = kt if jj % per_blk == 0 else ksum + kt
            if jj % per_blk == per_blk - 1:
                kmean_n = jnp.sum(ksum, axis=1, keepdims=True) * (1.0 / MOBA_BLOCK)
                aux = aux + jnp.where(lane_w == j * blk_per_step + jj // per_blk, kmean_n, 0.0)
        else:
            c_page = c_pages[jj * SUBLANES:(jj + 1) * SUBLANES, :]
            s = s - _rows_per_head(c_page + aux, heads, t_new)
            aux = aux + c_page[:, page - 1:page]
        s_ref[jj] = s
    for g in range(blk_per_step):
        n = j * blk_per_step + g
        pages = range(g * per_blk, (g + 1) * per_blk)
        m_n = functools.reduce(jnp.maximum, [jnp.max(s_ref[jj], axis=1, keepdims=True) for jj in pages])
        l_n = 0.0
        for jj in pages:
            p = jnp.exp(s_ref[jj] - m_n)
            l_n = l_n + jnp.sum(p, axis=1, keepdims=True)
            p_ref[jj] = p.astype(BF16)
        m_all = jnp.where(lane == n, m_n, m_all)
        l_all = jnp.where(lane == n, l_n, l_all)
    for g in range(blk_per_step):
        part_ref[j * blk_per_step + g] = sum(
            _dot_nt(p_ref[jj], v_refs[jj][...].reshape(width, page).astype(BF16))
            for jj in range(g * per_blk, (g + 1) * per_blk))
    m_ref[...] = m_all
    l_ref[...] = l_all
    aux_ref[...] = aux

    @pl.when(j == n_steps - 1)
    def _():
        qi = lax.broadcasted_iota(jnp.int32, (rows, t_new), 0) % t_new
        kj = lax.broadcasted_iota(jnp.int32, (rows, t_new), 1)
        s_own = _dot_nt(qb, kn_ref[...].astype(BF16))
        if moba:
            gate = _dot(qbd_ref[...], aux, precision=HIGHEST)
            sel = _topk_mask(gate, lane < n_blk, MOBA_TOPK, axis=1) > 0.0
        else:
            r8 = lax.broadcasted_iota(jnp.int32, (t_new, t_new), 0)
            c8 = lax.broadcasted_iota(jnp.int32, (t_new, t_new), 1)
            c_new = (_dot(lfn_ref[...], (r8 <= c8).astype(F32), precision=HIGHEST)
                     + aux[:heads, :])
            s_own = s_own - _rows_per_head(c_new, heads, t_new)
            sel = lane < n_blk
        s_own = jnp.where(kj <= qi, s_own, NEG_BIG)
        m_own = jnp.max(s_own, axis=1, keepdims=True)
        m_tot = jnp.maximum(m_own, jnp.max(jnp.where(sel, m_all, -jnp.inf), axis=1, keepdims=True))
        p_own = jnp.exp(s_own - m_tot)
        w = jnp.where(sel, jnp.exp(m_all - m_tot), 0.0)
        l_tot = jnp.sum(p_own, axis=1, keepdims=True) + jnp.sum(w * l_all, axis=1, keepdims=True)
        acc = _dot(p_own.astype(BF16), vn_ref[...].astype(BF16))
        for n in range(n_blk):
            acc = acc + w[:, n:n + 1] * part_ref[n]
        o_ref[...] = _collapse_heads(acc / l_tot, heads, t_new)


def _decode(page_table, q, k_new, v_new, cache_kt, cache_vt, layer, n_seq, t_new, heads, lf=None, pages_per_step=32):
    n_pages = page_table.shape[1]
    pp = min(pages_per_step, n_pages)
    page = cache_kt.shape[-1]
    width = heads * HEAD_DIM
    per_blk = MOBA_BLOCK // page
    assert MOBA_BLOCK % page == 0 and n_pages % pp == 0 and pp % per_blk == 0
    n_steps = n_pages // pp
    n_blk = n_pages // per_blk
    assert n_blk <= LANES
    rows = heads * t_new
    new = pl.BlockSpec((t_new, width), lambda b, j, pt: (b, 0))

    def pages(block, index_fn):
        return [pl.BlockSpec(block, lambda b, j, pt, jj=jj: index_fn(pt[b, j * pp + jj])) for jj in range(pp)]

    kv_pages = lambda: pages((None, None, heads, HEAD_DIM, page), lambda p: (layer, p, 0, 0, 0))
    args, specs = [q, k_new, v_new], [new, new, new]
    if lf is not None:
        args.append(lf[0])
        specs.append(pl.BlockSpec((None, heads, t_new), lambda b, j, pt: (b, 0, 0)))
    args += [cache_kt] * pp + [cache_vt] * pp
    specs += kv_pages() + kv_pages()
    scratch_tail = []
    est = 2 * 2 * pp * page * (width + 8) * 4 + n_blk * rows * width * 4 + 4 * LANES * width * 4
    if lf is not None:
        args.append(lf[1])
        specs.append(pl.BlockSpec((None,) + lf[1].shape[1:], lambda b, j, pt: (layer, 0, 0, 0),
                                  pipeline_mode=pl.Buffered(1)))
        scratch_tail = [pltpu.VMEM((pp * SUBLANES, page), F32)]
        est += lf[1][0].size * 4
    aux_shape = (SUBLANES, 1) if lf is not None else (width, LANES)
    grid_spec = pltpu.PrefetchScalarGridSpec(
        num_scalar_prefetch=1,
        grid=(n_seq, n_steps),
        in_specs=specs,
        out_specs=new,
        scratch_shapes=[pltpu.VMEM((rows, width), F32), pltpu.VMEM(aux_shape, F32),
                        pltpu.VMEM((rows, LANES), F32), pltpu.VMEM((rows, LANES), F32),
                        pltpu.VMEM((n_blk, rows, width), F32),
                        pltpu.VMEM((pp, rows, page), F32), pltpu.VMEM((pp, rows, page), BF16)] + scratch_tail,
    )
    return pl.pallas_call(
        functools.partial(_decode_body, heads=heads, moba=lf is None, pp=pp, n_steps=n_steps, t_new=t_new),
        out_shape=jax.ShapeDtypeStruct((n_seq * t_new, width), F32),
        grid_spec=grid_spec,
        compiler_params=pltpu.CompilerParams(
            dimension_semantics=("arbitrary", "arbitrary"), vmem_limit_bytes=_vmem_limit(est)),
        name="moba_decode" if lf is None else "fox_decode",
    )(page_table, *args)


def _pad_w_in(w):
    sizes = (A_W,) * 3 + (B_W,) * 3 + (B_HEADS,) + (C_W,) * 4
    offs = np.cumsum((0,) + sizes)
    segs = [SEG_QA, SEG_KA, SEG_VA, SEG_QF, SEG_KF, SEG_VF, SEG_FF, SEG_QC, SEG_FC, SEG_IC, SEG_GC]
    order = np.argsort(segs)
    ends = sorted(segs)[1:] + [D_IN_PAD]
    wb = w.astype(BF16)
    parts = []
    for i, end in zip(order, ends):
        parts.append(wb[:, :, offs[i]:offs[i] + sizes[i]])
        gap = end - segs[i] - sizes[i]
        if gap:
            parts.append(jnp.zeros(w.shape[:2] + (gap,), BF16))
    return jnp.concatenate(parts, axis=2)


def _rope_tables(pos0, t, reps):
    half = HEAD_DIM // 2
    inv = ROPE_THETA ** (-jnp.arange(half, dtype=F32) * 2.0 / HEAD_DIM)
    ang = (pos0 + jnp.arange(t, dtype=F32))[:, None] * inv[None, :]
    cos, sin, zero = jnp.cos(ang), jnp.sin(ang), jnp.zeros((t, half), F32)
    heads_per_vreg = LANES // HEAD_DIM
    tabs = (jnp.concatenate([cos, cos] * heads_per_vreg, axis=1),
            jnp.concatenate([-sin, zero] * heads_per_vreg, axis=1),
            jnp.concatenate([zero, sin] * heads_per_vreg, axis=1))
    return tuple(jnp.tile(tb, (reps, 1)) for tb in tabs)


def _pad_lanes(a, width):
    return jnp.pad(a, ((0, 0), (0, width - a.shape[1])))


def _state_rows(s):
    b, h, dk, dv = s.shape
    return jnp.swapaxes(s, 2, 3).reshape(b, h * dv, dk)


def _rows_state(st, h):
    b, _, dk = st.shape
    return jnp.swapaxes(st.reshape(b, h, -1, dk), 2, 3)


def kernel(x_prompt, x_sample, cache_moba_k, cache_moba_v, cache_fox_k, cache_fox_v, cache_fox_logf, state_hgrn, page_table, norm_ffn1_g, ffn1_w_gate, ffn1_w_up, ffn1_w_down, norm_mix_g, w_in, fox_f_bias, hgrn_lb_logits, hgrn_norm_g, w_out, norm_ffn2_g, ffn2_w_gate, ffn2_w_up, ffn2_w_down, norm_final_g):
    depth = w_in.shape[0]
    batch, seq, d = x_prompt.shape
    n_seq, t_new, _ = x_sample.shape
    page = cache_moba_k.shape[2]
    past_len = page_table.shape[1] * page

    row = lambda a: a.reshape(1, -1)
    bf = lambda w: w.astype(BF16)
    ffn1 = lambda l: (row(norm_ffn1_g[l]), w1g, w1u, w1d, l)
    w1g, w1u, w1d = bf(ffn1_w_gate), bf(ffn1_w_up), bf(ffn1_w_down)
    w2g, w2u, w2d = bf(ffn2_w_gate), bf(ffn2_w_up), bf(ffn2_w_down)
    wo = bf(w_out)
    w_in_pad = _pad_w_in(w_in)
    fb = [_pad_lanes(row(fox_f_bias[l]), LANES) for l in range(depth)]
    lbl = _pad_lanes(hgrn_lb_logits.astype(F32), SEG_W)
    gn = jnp.tile(hgrn_norm_g, (1, C_HEADS))
    to_pages = lambda c: jnp.transpose(c, (0, 1, 3, 4, 2))
    ckt, cvt, fkt, fvt = (to_pages(c) for c in (cache_moba_k, cache_moba_v, cache_fox_k, cache_fox_v))
    flt = jnp.transpose(cache_fox_logf, (0, 3, 1, 2))

    def trunk(x, sample):
        if sample:
            nb, t = n_seq, t_new
            tabs = _rope_tables(float(past_len), t, nb)
        else:
            nb, t = batch, seq
            tabs = _rope_tables(0.0, t, 1)
        new, kv_t = [], None
        for l in range(depth):
            if l == 0:
                x = _ffn(x, *ffn1(l))
            args = (x, row(norm_mix_g[l]), w_in_pad, tabs, fb[l], lbl, l)
            heads_last = lambda a, h: a.reshape(nb, t, h, HEAD_DIM)
            if sample:
                qa, ka, va, qf, kf, vf, lf, qc, kc, lc, ic, gc = _inproj(*args)
                oa = _decode(page_table, qa, ka, va, ckt, cvt, l, nb, t, A_HEADS)
                lf_t = jnp.swapaxes(lf[:, :B_HEADS].reshape(nb, t, B_HEADS), 1, 2)
                of = _decode(page_table, qf, kf, vf, fkt, fvt, l, nb, t, B_HEADS, lf=(lf_t, flt))
                st0 = _state_rows(state_hgrn[l].astype(F32))
                rows_kv = (heads_last(ka, A_HEADS), heads_last(va, A_HEADS),
                           heads_last(kf, B_HEADS), heads_last(vf, B_HEADS))
            else:
                qa, ka, ka_t, va_t, qf, kf, kf_t, vf_t, lf, qc, kc, lc, ic, gc = _inproj(
                    *args, seq_t=t, carried=kv_t)
                kv_t = (ka_t, va_t, kf_t, vf_t)
                oa = _moba_prompt(qa, ka, va_t, l, nb, t)
                of = _fox_prompt(qf, kf, vf_t, lf, l, nb, t)
                st0 = jnp.zeros((nb, C_W, HEAD_DIM), F32)
                rows_kv = ()
            oc, st = _hgrn(qc, kc, lc, ic, gc, row(gn[l]), st0, nb, t)
            last = l == depth - 1
            x = _ffn(x, row(norm_ffn2_g[l]), w2g, w2u, w2d, l, mix=(oa, of, oc, wo),
                     g_final=row(norm_final_g) if last else None)
            if not last:
                x = _ffn(x, *ffn1(l + 1))
            new.append(rows_kv + (lf[:, :B_HEADS].reshape(nb, t, B_HEADS), _rows_state(st, C_HEADS)))
        outs = [jnp.stack(z) for z in zip(*new)]
        if not sample:
            heads_of = (A_HEADS, A_HEADS, B_HEADS, B_HEADS)
            outs = [jnp.transpose(a.reshape(depth, nb, h, HEAD_DIM, t), (0, 1, 4, 2, 3))
                    for a, h in zip(kv_t, heads_of)] + outs
        return x.reshape(nb, t, d), outs

    y_p, (mk_p, mv_p, fk_p, fv_p, fl_p, hs_p) = trunk(x_prompt.reshape(batch * seq, d), False)
    y_s, (mk_s, mv_s, fk_s, fv_s, fl_s, hs_s) = trunk(x_sample.reshape(n_seq * t_new, d), True)
    return (y_p, y_s, mk_p, mv_p, fk_p, fv_p, fl_p, hs_p, mk_s, mv_s, fk_s, fv_s, fl_s, hs_s)
```

```python
import functools
import math

import numpy as np
import jax
import jax.numpy as jnp
from jax import lax
from jax.experimental import pallas as pl
from jax.experimental.pallas import tpu as pltpu

F32 = jnp.float32
BF16 = jnp.bfloat16
HIGHEST = lax.Precision.HIGHEST

HEAD_DIM = 64
A_HEADS = 6
B_HEADS = 5
C_HEADS = 5
A_W = A_HEADS * HEAD_DIM
B_W = B_HEADS * HEAD_DIM
C_W = C_HEADS * HEAD_DIM
D_FF_CHUNK = 256
MOBA_BLOCK = 256
MOBA_TOPK = 3
ROPE_THETA = 10000.0
EPS = 1e-6
NEG_BIG = -1e30

LANES = 128
SUBLANES = 8
MXU_W = 256
VMEM_PHYSICAL = 64 * 1024 * 1024
VMEM_CAP = VMEM_PHYSICAL - 8 * 1024 * 1024

SEG_W = 384
SEG_QA, SEG_KA, SEG_VA, SEG_QF, SEG_KF, SEG_VF, SEG_QC, SEG_FC, SEG_IC, SEG_GC = (
    i * SEG_W for i in range(10))
SEG_FF = 10 * SEG_W
D_IN_PAD = SEG_FF + LANES


def _vmem_limit(estimate_bytes):
    return int(min(max(2 * estimate_bytes, 32 * 1024 * 1024), VMEM_CAP))


def _dot(a, b, precision=None):
    return jnp.dot(a, b, preferred_element_type=F32, precision=precision)


def _dot_nt(a, b, precision=None):
    return lax.dot_general(a, b, (((1,), (1,)), ((), ())),
                           preferred_element_type=F32, precision=precision)


def _dot_tn(a, b, precision=None):
    return lax.dot_general(a, b, (((0,), (0,)), ((), ())),
                           preferred_element_type=F32, precision=precision)


def _rms(x, g):
    ms = jnp.mean(x * x, axis=-1, keepdims=True)
    return x * lax.rsqrt(ms + EPS) * g


def _sigmoid(x):
    return 1.0 / (1.0 + jnp.exp(-x))


def _silu(x):
    return x * _sigmoid(x)


def _log_sigmoid(x):
    return jnp.minimum(x, 0.0) - jnp.log1p(jnp.exp(-jnp.abs(x)))


def _bf16_pieces(x):
    x1 = x.astype(BF16)
    r1 = x - x1.astype(F32)
    x2 = r1.astype(BF16)
    return x1, x2, (r1 - x2.astype(F32)).astype(BF16)


def _dot_sel(x, sel):
    return sum(_dot(p, sel) for p in _bf16_pieces(x))


def _sel_dot(sel, x):
    return sum(_dot(sel, p) for p in _bf16_pieces(x))


def _ffn_body(*refs, has_mix, has_final, d_ff):
    it = iter(refs)
    x_ref = next(it)
    if has_mix:
        oa_ref, of_ref, oc_ref, wo_ref = (next(it) for _ in range(4))
    g_ref, wg_ref, wu_ref, wd_ref = (next(it) for _ in range(4))
    gf_ref = next(it) if has_final else None
    out_ref = next(it)

    x = x_ref[...]
    if has_mix:
        x = (x + _dot(oa_ref[...].astype(BF16), wo_ref[:A_W, :])
             + _dot(of_ref[...].astype(BF16), wo_ref[A_W:A_W + B_W, :])
             + _dot(oc_ref[...].astype(BF16), wo_ref[A_W + B_W:, :]))
    h = _rms(x, g_ref[...]).astype(BF16)
    acc = jnp.zeros(x.shape, F32)
    for c0 in range(0, d_ff, D_FF_CHUNK):
        gate = _dot(h, wg_ref[:, c0:c0 + D_FF_CHUNK])
        up = _dot(h, wu_ref[:, c0:c0 + D_FF_CHUNK])
        act = (_silu(gate) * up).astype(BF16)
        acc = acc + _dot(act, wd_ref[c0:c0 + D_FF_CHUNK, :])
    y = x + 0.5 * acc
    if has_final:
        y = _rms(y, gf_ref[...])
    out_ref[...] = y


def _layer_spec(a, layer):
    return pl.BlockSpec((None,) + a.shape[1:], lambda *_: (layer,) + (0,) * (a.ndim - 1),
                        pipeline_mode=pl.Buffered(1))


def _ffn(x, g, wg, wu, wd, layer, mix=None, g_final=None):
    n, d = x.shape
    d_ff = wg.shape[2]
    tm = min(1024 if mix is None else 512, n)
    assert n % tm == 0 and d_ff % D_FF_CHUNK == 0
    row = lambda w: pl.BlockSpec((tm, w), lambda i: (i, 0))
    full = lambda a: pl.BlockSpec(a.shape, lambda i: (0,) * a.ndim, pipeline_mode=pl.Buffered(1))
    args, specs = [x], [row(d)]
    if mix is not None:
        oa, of, oc, wo = mix
        args += [oa, of, oc, wo]
        specs += [row(oa.shape[1]), row(of.shape[1]), row(oc.shape[1]), _layer_spec(wo, layer)]
    args += [g, wg, wu, wd]
    specs += [full(g), _layer_spec(wg, layer), _layer_spec(wu, layer), _layer_spec(wd, layer)]
    if g_final is not None:
        args.append(g_final)
        specs.append(full(g_final))
    weights = 2 * 3 * d * d_ff + (2 * d * d if mix is not None else 0)
    tiles = 2 * 2 * tm * d * 4 * (2 if mix is not None else 1) + 6 * tm * d * 4
    return pl.pallas_call(
        functools.partial(_ffn_body, has_mix=mix is not None, has_final=g_final is not None, d_ff=d_ff),
        out_shape=jax.ShapeDtypeStruct((n, d), F32),
        grid=(n // tm,),
        in_specs=specs,
        out_specs=row(d),
        compiler_params=pltpu.CompilerParams(
            dimension_semantics=("arbitrary",), vmem_limit_bytes=_vmem_limit(weights + tiles)),
        name="ffn_mix" if mix is not None else "ffn",
    )(*args)


def _inproj_body(x_ref, g_ref, w_ref, cos_ref, slo_ref, shi_ref, fb_ref, lbl_ref, *out_refs, layer, transposed,
                 n_carried):
    out_refs = out_refs[n_carried:]
    if transposed:
        (qa_ref, ka_ref, kat_ref, vat_ref, qf_ref, kf_ref, kft_ref, vft_ref, lf_ref,
         qc_ref, kc_ref, lc_ref, ic_ref, gc_ref, stage_ref) = out_refs
    else:
        (qa_ref, ka_ref, va_ref, qf_ref, kf_ref, vf_ref, lf_ref,
         qc_ref, kc_ref, lc_ref, ic_ref, gc_ref) = out_refs
    h = _rms(x_ref[...], g_ref[...]).astype(BF16)

    def proj(c0, width=SEG_W):
        return _dot(h, w_ref[:, c0:c0 + width])

    cos, slo, shi = cos_ref[...], slo_ref[...], shi_ref[...]

    def rope(p):
        parts = []
        for c in range(SEG_W // LANES):
            pc = p[:, c * LANES:(c + 1) * LANES]
            parts.append(pc * cos + pltpu.roll(pc, LANES - HEAD_DIM // 2, 1) * slo
                         + pltpu.roll(pc, HEAD_DIM // 2, 1) * shi)
        return jnp.concatenate(parts, axis=1)

    qa_ref[...] = rope(proj(SEG_QA))
    ka = rope(proj(SEG_KA))
    ka_ref[...] = ka
    if transposed:
        def transposed_tile(p):
            stage_ref[...] = p
            return stage_ref[...].T

        kat_ref[...] = ka.T
        vat_ref[...] = transposed_tile(proj(SEG_VA))
        qf_ref[...] = proj(SEG_QF)
        kf_ref[...] = proj(SEG_KF)
        kft_ref[...] = kf_ref[...].T[:B_W, :]
        vft_ref[...] = transposed_tile(proj(SEG_VF))[:B_W, :]
    else:
        va_ref[...] = proj(SEG_VA)
        qf_ref[...] = proj(SEG_QF)[:, :B_W]
        kf_ref[...] = proj(SEG_KF)[:, :B_W]
        vf_ref[...] = proj(SEG_VF)[:, :B_W]
    lf_ref[...] = _log_sigmoid(proj(SEG_FF, LANES) + fb_ref[...])

    lbl = lbl_ref[...]
    e = jnp.exp(lbl - jnp.max(lbl, axis=0, keepdims=True))
    psm = e / jnp.sum(e, axis=0, keepdims=True)
    lb = jnp.zeros((1, SEG_W), F32)
    for j in range(1, layer + 1):
        lb = lb + psm[j:j + 1, :]
    fc = proj(SEG_FC)
    lc_ref[...] = jnp.log(lb + (1.0 - lb) * _sigmoid(fc))[:, :C_W]
    kc_ref[...] = ((1.0 - lb) * _sigmoid(-fc))[:, :C_W]
    qc_ref[...] = _silu(proj(SEG_QC))[:, :C_W]
    ic_ref[...] = proj(SEG_IC)[:, :C_W]
    gc_ref[...] = _silu(proj(SEG_GC))[:, :C_W]


def _inproj(x, g, w_pad, rope_tabs, fb, lbl, layer, seq_t=None, carried=None, tm=512):
    n, d = x.shape
    depth = w_pad.shape[0]
    tm = min(tm, n)
    assert n % tm == 0 and rope_tabs[0].shape[0] % tm == 0
    row = lambda w: pl.BlockSpec((tm, w), lambda i: (i, 0))
    full = lambda a: pl.BlockSpec(a.shape, lambda i: (0,) * a.ndim, pipeline_mode=pl.Buffered(1))
    n_tab = rope_tabs[0].shape[0] // tm
    tab = pl.BlockSpec((tm, LANES), lambda i: (i % n_tab, 0))
    args = [x, g, w_pad, *rope_tabs, fb, lbl]
    in_specs = [row(d), full(g), _layer_spec(w_pad, layer), tab, tab, tab, full(fb), full(lbl)]
    aliases = {}
    if seq_t is None:
        widths = [A_W, A_W, A_W, B_W, B_W, B_W, LANES, C_W, C_W, C_W, C_W, C_W]
        shapes = [(n, w) for w in widths]
        out_specs = [row(w) for w in widths]
    else:
        assert seq_t % tm == 0
        per_seq = seq_t // tm
        col = lambda w: pl.BlockSpec((None, None, w, tm), lambda i: (layer, i // per_seq, 0, i % per_seq))
        kinds = [(row, A_W), (row, A_W), (col, A_W), (col, A_W), (row, SEG_W), (row, SEG_W), (col, B_W),
                 (col, B_W), (row, LANES), (row, C_W), (row, C_W), (row, C_W), (row, C_W), (row, C_W)]
        widths = [w for _, w in kinds]
        shapes = [(n, w) if f is row else (depth, n // seq_t, w, seq_t) for f, w in kinds]
        out_specs = [f(w) for f, w in kinds]
        if carried is not None:
            stacked = [i for i, (f, _) in enumerate(kinds) if f is col]
            aliases = {len(args) + k: i for k, i in enumerate(stacked)}
            args += list(carried)
            in_specs += [pl.BlockSpec(memory_space=pl.ANY)] * len(carried)
    est = 2 * d * D_IN_PAD + 2 * tm * 4 * (d + sum(widths) + 3 * LANES) + 8 * tm * SEG_W * 4
    return pl.pallas_call(
        functools.partial(_inproj_body, layer=layer, transposed=seq_t is not None, n_carried=len(aliases)),
        out_shape=[jax.ShapeDtypeStruct(s, F32) for s in shapes],
        grid=(n // tm,),
        in_specs=in_specs,
        out_specs=out_specs,
        scratch_shapes=[pltpu.VMEM((tm, SEG_W), F32)] if seq_t is not None else [],
        input_output_aliases=aliases,
        compiler_params=pltpu.CompilerParams(
            dimension_semantics=("arbitrary",), vmem_limit_bytes=_vmem_limit(est)),
        name="inproj",
    )(*args)


def _topk_mask(gate, allowed, topk, axis):
    n = gate.shape[axis]
    pos = lax.broadcasted_iota(jnp.int32, gate.shape, axis).astype(F32)
    g = jnp.where(allowed, gate, NEG_BIG)
    sel = jnp.zeros(gate.shape, F32)
    for _ in range(topk):
        m = jnp.max(g, axis=axis, keepdims=True)
        idx = jnp.min(jnp.where(g == m, pos, float(n)), axis=axis, keepdims=True)
        hit = pos == idx
        sel = jnp.where(hit & allowed, 1.0, sel)
        g = jnp.where(hit, -jnp.inf, g)
    return sel


def _own_lanes(shape, h):
    return lax.broadcasted_iota(jnp.int32, shape, 1) // HEAD_DIM == h % 2


def _flash_step(heads, score_fn, value_fn, s_ref, p_ref, m_ref, l_ref, acc_ref):
    for h in range(heads):
        s_ref[h] = score_fn(h)
    alphas = []
    for h in range(heads):
        m_old = m_ref[h]
        m_new = jnp.maximum(m_old, jnp.max(s_ref[h], axis=0, keepdims=True))
        alpha = jnp.exp(m_old - m_new)
        p = jnp.exp(s_ref[h] - m_new)
        l_ref[h] = alpha * l_ref[h] + jnp.sum(p, axis=0, keepdims=True)
        m_ref[h] = m_new
        p_ref[h] = p.astype(BF16)
        alphas.append(alpha)
    for h in range(heads):
        acc_ref[h] = acc_ref[h] * alphas[h] + _dot(value_fn(h), p_ref[h])


def _flash_init(m_ref, l_ref, acc_ref):
    m_ref[...] = jnp.full(m_ref.shape, -jnp.inf, F32)
    l_ref[...] = jnp.zeros(l_ref.shape, F32)
    acc_ref[...] = jnp.zeros(acc_ref.shape, F32)


def _flash_finish(o_ref, l_ref, acc_ref, heads):
    blk = acc_ref.shape[2]
    parts = [acc_ref[h] / l_ref[h] for h in range(heads)]
    parts += [jnp.zeros((HEAD_DIM, blk), F32)] * (SEG_W // HEAD_DIM - heads)
    o_ref[...] = jnp.concatenate(parts, axis=0).T[:, :o_ref.shape[1]]


def _flash_scratch(heads, blk):
    return [pltpu.VMEM((heads, blk, blk), F32), pltpu.VMEM((heads, blk, blk), BF16),
            pltpu.VMEM((heads, 1, blk), F32), pltpu.VMEM((heads, 1, blk), F32),
            pltpu.VMEM((heads, HEAD_DIM, blk), F32)]


def _moba_prompt_body(q_ref, k_ref, vt_ref, o_ref, kmean_ref, k16_ref, vt16_ref, qm_ref, sel_ref,
                      s_ref, p_ref, m_ref, l_ref, acc_ref, *, blk):
    i = pl.program_id(1)
    nb = k_ref.shape[0] // blk
    scale = HEAD_DIM ** -0.5

    @pl.when(i == 0)
    def _():
        for n in range(nb):
            kb = k_ref[n * blk:(n + 1) * blk, :]
            kmean_ref[n:n + 1, :] = jnp.sum(kb, axis=0, keepdims=True) * (1.0 / blk)
            k16_ref[n * blk:(n + 1) * blk, :] = kb.astype(BF16)
            vt16_ref[n] = vt_ref[:, n * blk:(n + 1) * blk].astype(BF16)

    past = lax.broadcasted_iota(jnp.int32, (nb, blk), 0) < i
    for h in range(A_HEADS):
        c0 = (h // 2) * LANES
        qwin = q_ref[:, c0:c0 + LANES]
        kmean_h = jnp.where(_own_lanes((nb, LANES), h), kmean_ref[:, c0:c0 + LANES], 0.0)
        gate = _dot_nt(kmean_h, qwin, precision=HIGHEST)
        sel_ref[h] = _topk_mask(gate, past, MOBA_TOPK, axis=0)
        qm_ref[h] = jnp.where(_own_lanes((blk, LANES), h), qwin * scale, 0.0).astype(BF16)
    _flash_init(m_ref, l_ref, acc_ref)
    causal = (lax.broadcasted_iota(jnp.int32, (blk, blk), 0)
              <= lax.broadcasted_iota(jnp.int32, (blk, blk), 1))

    def step(n, mask_fn):
        rows = pl.ds(pl.multiple_of(n * blk, blk), blk)
        _flash_step(
            A_HEADS,
            lambda h: mask_fn(h, _dot_nt(k16_ref[rows, (h // 2) * LANES:(h // 2 + 1) * LANES], qm_ref[h])),
            lambda h: vt16_ref[n, h * HEAD_DIM:(h + 1) * HEAD_DIM, :],
            s_ref, p_ref, m_ref, l_ref, acc_ref)

    step(i, lambda h, st: jnp.where(causal, st, NEG_BIG))

    def body(n, carry):
        step(n, lambda h, st: jnp.where(sel_ref[h, pl.ds(n, 1), :] > 0.0, st, NEG_BIG))
        return carry

    lax.fori_loop(0, i, body, 0)
    _flash_finish(o_ref, l_ref, acc_ref, A_HEADS)


def _moba_prompt(q, k, vt, layer, batch, seq):
    blk = MOBA_BLOCK
    assert seq % blk == 0
    nq = seq // blk
    est = 2 * 2 * seq * A_W * 4 + 2 * seq * A_W * 2 + 4 * blk * A_W * 4 + 24 * blk * blk * 4
    return pl.pallas_call(
        functools.partial(_moba_prompt_body, blk=blk),
        out_shape=jax.ShapeDtypeStruct(q.shape, F32),
        grid=(batch, nq),
        in_specs=[pl.BlockSpec((blk, A_W), lambda b, i: (b * nq + i, 0)),
                  pl.BlockSpec((seq, A_W), lambda b, i: (b, 0)),
                  pl.BlockSpec((None, None, A_W, seq), lambda b, i: (layer, b, 0, 0))],
        out_specs=pl.BlockSpec((blk, A_W), lambda b, i: (b * nq + i, 0)),
        scratch_shapes=[pltpu.VMEM((nq, A_W), F32), pltpu.VMEM((seq, A_W), BF16),
                        pltpu.VMEM((nq, A_W, blk), BF16), pltpu.VMEM((A_HEADS, blk, LANES), BF16),
                        pltpu.VMEM((A_HEADS, nq, blk), F32)] + _flash_scratch(A_HEADS, blk),
        compiler_params=pltpu.CompilerParams(
            dimension_semantics=("arbitrary", "arbitrary"), vmem_limit_bytes=_vmem_limit(est)),
        name="moba_prompt",
    )(q, k, vt)


FOX_AUG = 8


EXP_ZERO = 104.0
NORM_SLACK = 1.02


def _fox_prompt_body(q_ref, k_ref, vt_ref, lf_ref, o_ref, bq_ref, kaug_ref, vt16_ref, qaug_ref, c_ref, kn_ref,
                     s_ref, p_ref, m_ref, l_ref, acc_ref, *, blk):
    i = pl.program_id(1)
    nb = k_ref.shape[0] // blk
    scale = HEAD_DIM ** -0.5
    key_i = lax.broadcasted_iota(jnp.int32, (blk, blk), 0)
    qry_i = lax.broadcasted_iota(jnp.int32, (blk, blk), 1)
    lane1 = lax.broadcasted_iota(jnp.int32, (1, LANES), 1)
    head_of_col = (lax.broadcasted_iota(jnp.int32, (SEG_W, LANES), 0) // HEAD_DIM
                   == lax.broadcasted_iota(jnp.int32, (SEG_W, LANES), 1)).astype(BF16)

    def max_norm(x):
        return jnp.sqrt(jnp.max(_dot_sel(x * x, head_of_col), axis=0, keepdims=True))

    @pl.when(i == 0)
    def _():
        tri = (qry_i <= key_i).astype(F32)
        head = lax.broadcasted_iota(jnp.int32, (LANES, LANES), 0)
        lane = lax.broadcasted_iota(jnp.int32, (LANES, LANES), 1)
        place = lambda off: ((lane == head * FOX_AUG + off) & (head < B_HEADS)).astype(BF16)
        used = lane1 < B_HEADS * FOX_AUG
        ones_q = (used & (lane1 % FOX_AUG < 3)).astype(F32)
        ones_k = (used & (lane1 % FOX_AUG >= 3) & (lane1 % FOX_AUG < 6)).astype(F32)
        carry = jnp.zeros((1, LANES), F32)
        for n in range(nb):
            rows = slice(n * blk, (n + 1) * blk)
            c = _dot(tri, lf_ref[rows, :], precision=HIGHEST) + carry
            carry = c[blk - 1:blk, :]
            c_ref[rows, :] = c
            kn_ref[n:n + 1, :] = max_norm(k_ref[rows, :])
            pieces = _bf16_pieces(c)
            bq = ones_q + sum(_dot(pieces[j], place(3 + j)) for j in range(3))
            ak = ones_k - sum(_dot(pieces[j], place(j)) for j in range(3))
            bq_ref[rows, :] = bq.astype(BF16)
            for w in range(SEG_W // LANES):
                kaug_ref[w, rows, :LANES] = k_ref[rows, w * LANES:(w + 1) * LANES].astype(BF16)
                kaug_ref[w, rows, LANES:] = ak.astype(BF16)
            vt16_ref[n] = vt_ref[:, rows].astype(BF16)

    bq_rows = bq_ref[pl.ds(pl.multiple_of(i * blk, blk), blk), :]
    group = lax.broadcasted_iota(jnp.int32, (blk, LANES), 1) // FOX_AUG
    for h in range(B_HEADS):
        c0 = (h // 2) * LANES
        qaug_ref[h, :, :LANES] = jnp.where(_own_lanes((blk, LANES), h), q_ref[:, c0:c0 + LANES] * scale,
                                           0.0).astype(BF16)
        qaug_ref[h, :, LANES:] = jnp.where(group == h, bq_rows, jnp.zeros_like(bq_rows))
    _flash_init(m_ref, l_ref, acc_ref)
    causal = key_i <= qry_i

    def step(n, diag):
        rows = pl.ds(pl.multiple_of(n * blk, blk), blk)

        def scores(h):
            st = _dot_nt(kaug_ref[h // 2, rows, :], qaug_ref[h])
            return jnp.where(causal, st, NEG_BIG) if diag else st

        _flash_step(B_HEADS, scores, lambda h: vt16_ref[n, h * HEAD_DIM:(h + 1) * HEAD_DIM, :],
                    s_ref, p_ref, m_ref, l_ref, acc_ref)

    step(i, True)

    m_low = jnp.zeros((1, LANES), F32)
    for h in range(B_HEADS):
        m_low = jnp.where(lane1 == h, jnp.min(m_ref[h], axis=1, keepdims=True), m_low)
    q_norm = max_norm(q_ref[...] * scale)
    c_first = c_ref[pl.ds(pl.multiple_of(i * blk, blk), 1), :]

    def body(n, carry):
        c_last = c_ref[pl.ds(n * blk + blk - 1, 1), :]
        reach = NORM_SLACK * q_norm * kn_ref[pl.ds(n, 1), :] + (c_first - c_last) - m_low
        live = jnp.max(jnp.where(lane1 < B_HEADS, reach, -jnp.inf)) > -EXP_ZERO

        @pl.when(live)
        def _():
            step(n, False)

        return carry

    lax.fori_loop(0, i, body, 0)
    _flash_finish(o_ref, l_ref, acc_ref, B_HEADS)


def _fox_prompt(q, k, vt, lf, layer, batch, seq, blk=256):
    assert seq % blk == 0
    nq = seq // blk
    est = (2 * seq * (SEG_W + B_W + LANES) * 4 + seq * (3 * 2 * LANES + LANES + B_W) * 2
           + 4 * blk * SEG_W * 4 + 24 * blk * blk * 4)
    return pl.pallas_call(
        functools.partial(_fox_prompt_body, blk=blk),
        out_shape=jax.ShapeDtypeStruct((batch * seq, B_W), F32),
        grid=(batch, nq),
        in_specs=[pl.BlockSpec((blk, SEG_W), lambda b, i: (b * nq + i, 0)),
                  pl.BlockSpec((seq, SEG_W), lambda b, i: (b, 0)),
                  pl.BlockSpec((None, None, B_W, seq), lambda b, i: (layer, b, 0, 0)),
                  pl.BlockSpec((seq, LANES), lambda b, i: (b, 0))],
        out_specs=pl.BlockSpec((blk, B_W), lambda b, i: (b * nq + i, 0)),
        scratch_shapes=[pltpu.VMEM((seq, LANES), BF16), pltpu.VMEM((SEG_W // LANES, seq, 2 * LANES), BF16),
                        pltpu.VMEM((nq, B_W, blk), BF16), pltpu.VMEM((B_HEADS, blk, 2 * LANES), BF16),
                        pltpu.VMEM((seq, LANES), F32), pltpu.VMEM((nq, LANES), F32)]
        + _flash_scratch(B_HEADS, blk),
        compiler_params=pltpu.CompilerParams(
            dimension_semantics=("arbitrary", "arbitrary"), vmem_limit_bytes=_vmem_limit(est)),
        name="fox_prompt",
    )(q, k, vt, lf)


def _hgrn_body(q_ref, k_ref, lf_ref, v_ref, gs_ref, gn_ref, st0_ref, o_ref, sto_ref, st_ref,
               qd_ref, kd_ref, v16_ref, dl_ref, upd_ref, *, sub, tile):
    seq = q_ref.shape[0]
    n_sub = tile // sub
    t_i = lax.broadcasted_iota(jnp.int32, (tile, tile), 0)
    s_i = lax.broadcasted_iota(jnp.int32, (tile, tile), 1)
    same_sub = t_i // sub == s_i // sub
    tri_blk = (same_sub & (s_i <= t_i)).astype(BF16)
    e_i = lax.broadcasted_iota(jnp.int32, (C_W, C_W), 0) // HEAD_DIM
    d_i = lax.broadcasted_iota(jnp.int32, (C_W, C_W), 1) // HEAD_DIM
    same_head = e_i == d_i
    ones_bd = same_head.astype(BF16)
    row_in_sub = lax.broadcasted_iota(jnp.int32, (n_sub, sub, C_W), 1)
    split = lambda a: a.reshape(n_sub, sub, C_W)

    def tile_rows(t):
        return pl.ds(pl.multiple_of(t * tile, tile), tile)

    def decay_pass(t, _):
        rows = tile_rows(t)
        lf, q, k, v = lf_ref[rows, :], q_ref[rows, :], k_ref[rows, :], v_ref[rows, :]
        cum = _sel_dot(tri_blk, lf)
        cum3, q3, k3, v3 = split(cum), split(q), split(k), split(v)
        last3 = cum3[:, sub - 1:sub, :]
        qd_ref[rows, :] = (q * jnp.exp(cum)).astype(BF16)
        kd_ref[rows, :] = (k3 * jnp.exp(last3 - cum3)).reshape(tile, C_W).astype(BF16)
        v16_ref[rows, :] = v.astype(BF16)
        dl_ref[rows, :] = jnp.broadcast_to(jnp.exp(last3), (n_sub, sub, C_W)).reshape(tile, C_W)
        n_t = sub // SUBLANES
        tiles = lambda a: a.reshape(n_sub, n_t, SUBLANES, C_W)
        cum4, q4 = tiles(cum), tiles(q)
        row8 = lax.broadcasted_iota(jnp.int32, (n_sub, SUBLANES, C_W), 1)
        o_tiles = [jnp.zeros((n_sub, SUBLANES, C_W), F32) for _ in range(n_t)]
        for s in range(sub):
            for t in range(s // SUBLANES, n_t):
                dec = jnp.exp(jnp.where(row8 + t * SUBLANES >= s, cum4[:, t] - cum3[:, s:s + 1, :], NEG_BIG))
                m = (q4[:, t] * k3[:, s:s + 1, :] * dec).reshape(n_sub * SUBLANES, C_W).astype(BF16)
                w = jnp.concatenate([_dot(m[:, :MXU_W], ones_bd[:MXU_W, :MXU_W]),
                                     _dot(m[:, MXU_W:], ones_bd[MXU_W:, MXU_W:])], axis=1)
                o_tiles[t] = o_tiles[t] + w.reshape(n_sub, SUBLANES, C_W) * v3[:, s:s + 1, :]
        o_ref[rows, :] = jnp.stack(o_tiles, axis=1).reshape(tile, C_W)
        return 0

    lax.fori_loop(0, seq // tile, decay_pass, 0)
    spread = (lax.broadcasted_iota(jnp.int32, (HEAD_DIM, C_W), 0)
              == lax.broadcasted_iota(jnp.int32, (HEAD_DIM, C_W), 1) % HEAD_DIM).astype(BF16)
    st_ref[...] = jnp.where(same_head, _dot_sel(st0_ref[...], spread), 0.0)

    group = upd_ref.shape[0]

    def chunks(g, _):
        for j in range(group):
            rows = pl.ds(pl.multiple_of((g * group + j) * sub, sub), sub)
            upd_ref[j] = jnp.where(same_head, _dot_tn(v16_ref[rows, :], kd_ref[rows, :]), 0.0)
        for j in range(group):
            c = g * group + j
            rows = pl.ds(pl.multiple_of(c * sub, sub), sub)
            st = st_ref[...]
            o_ref[rows, :] = o_ref[rows, :] + _dot_nt(qd_ref[rows, :], st.astype(BF16))
            st_ref[...] = st * dl_ref[pl.ds(c * sub, 1), :] + upd_ref[j]
        return 0

    lax.fori_loop(0, seq // (sub * group), chunks, 0)

    gn = gn_ref[...]

    def norm_pass(t, _):
        rows = tile_rows(t)
        o = o_ref[rows, :]
        sq = o * o
        hi = sq.astype(BF16)
        lo = (sq - hi.astype(F32)).astype(BF16)
        ms = (_dot(hi, ones_bd) + _dot(lo, ones_bd)) * (1.0 / HEAD_DIM)
        o_ref[rows, :] = o * lax.rsqrt(ms + EPS) * gn * gs_ref[rows, :]
        return 0

    lax.fori_loop(0, seq // tile, norm_pass, 0)
    gather = (lax.broadcasted_iota(jnp.int32, (C_W, HEAD_DIM), 0) % HEAD_DIM
              == lax.broadcasted_iota(jnp.int32, (C_W, HEAD_DIM), 1)).astype(BF16)
    sto_ref[...] = _dot_sel(st_ref[...], gather)


def _hgrn(q, k, lf, v, gs, gn, st0, batch, seq):
    sub = math.gcd(seq, 16)
    tile = min(seq, 256)
    assert seq % tile == 0 and tile % sub == 0
    row = pl.BlockSpec((seq, C_W), lambda b: (b, 0))
    st_spec = pl.BlockSpec((None, C_W, HEAD_DIM), lambda b: (b, 0, 0))
    group = math.gcd(seq // sub, 8)
    est = (2 * 6 * seq * C_W * 4 + (8 + group) * C_W * C_W * 4 + seq * C_W * (3 * 2 + 4)
           + 24 * tile * C_W * 4)
    return pl.pallas_call(
        functools.partial(_hgrn_body, sub=sub, tile=tile),
        out_shape=[jax.ShapeDtypeStruct((batch * seq, C_W), F32),
                   jax.ShapeDtypeStruct((batch, C_W, HEAD_DIM), F32)],
        grid=(batch,),
        in_specs=[row, row, row, row, row, pl.BlockSpec((1, C_W), lambda b: (0, 0)), st_spec],
        out_specs=[row, st_spec],
        scratch_shapes=[pltpu.VMEM((C_W, C_W), F32),
                        pltpu.VMEM((seq, C_W), BF16), pltpu.VMEM((seq, C_W), BF16),
                        pltpu.VMEM((seq, C_W), BF16), pltpu.VMEM((seq, C_W), F32),
                        pltpu.VMEM((group, C_W, C_W), F32)],
        compiler_params=pltpu.CompilerParams(
            dimension_semantics=("arbitrary",), vmem_limit_bytes=_vmem_limit(est)),
        name="hgrn",
    )(q, k, lf, v, gs, gn, st0)


def _block_diag_q(q, heads, scale):
    t, w = q.shape
    rep = jnp.concatenate([q] * heads, axis=0)
    r = lax.broadcasted_iota(jnp.int32, (heads * t, w), 0) // t
    c = lax.broadcasted_iota(jnp.int32, (heads * t, w), 1) // HEAD_DIM
    return jnp.where(r == c, rep * scale, 0.0)


def _collapse_heads(o, heads, t):
    r = lax.broadcasted_iota(jnp.int32, o.shape, 0) // t
    c = lax.broadcasted_iota(jnp.int32, o.shape, 1) // HEAD_DIM
    o = jnp.where(r == c, o, 0.0)
    out = o[0:t, :]
    for h in range(1, heads):
        out = out + o[h * t:(h + 1) * t, :]
    return out


def _rows_per_head(x, heads, t):
    return jnp.concatenate([jnp.broadcast_to(x[h:h + 1, :], (t, x.shape[1])) for h in range(heads)], axis=0)


def _decode_body(pt_ref, q_ref, kn_ref, vn_ref, *rest, heads, moba, pp, n_steps, t_new):
    width = heads * HEAD_DIM
    if moba:
        k_refs, v_refs, rest = rest[:pp], rest[pp:2 * pp], rest[2 * pp:]
    else:
        lfn_ref, rest = rest[0], rest[1:]
        k_refs, v_refs, lf_pool_ref, rest = rest[:pp], rest[pp:2 * pp], rest[2 * pp], rest[2 * pp + 1:]
        lf_ref, rest = rest[-1], rest[:-1]
    o_ref, qbd_ref, aux_ref, m_ref, l_ref, part_ref, s_ref, p_ref = rest
    j = pl.program_id(1)
    page = k_refs[0].shape[-1]
    rows = heads * t_new
    per_blk = MOBA_BLOCK // page
    blk_per_step = pp // per_blk
    n_blk = n_steps * blk_per_step
    scale = HEAD_DIM ** -0.5
    lane = lax.broadcasted_iota(jnp.int32, (rows, LANES), 1)

    @pl.when(j == 0)
    def _():
        qbd_ref[...] = _block_diag_q(q_ref[...], heads, 1.0)
        aux_ref[...] = jnp.zeros(aux_ref.shape, F32)
        m_ref[...] = jnp.zeros(m_ref.shape, F32)
        l_ref[...] = jnp.zeros(l_ref.shape, F32)
        if not moba:
            lf_ref[...] = jnp.zeros(lf_ref.shape, F32)

    qb = (qbd_ref[...] * scale).astype(BF16)
    m_all, l_all = m_ref[...], l_ref[...]
    aux = aux_ref[...]
    if moba:
        lane_w = lax.broadcasted_iota(jnp.int32, (width, LANES), 1)
    else:
        seq_i = pl.program_id(0)
        for jj in range(pp):
            page_id = pt_ref[seq_i, j * pp + jj]
            for h in range(heads):
                lf_ref[jj * SUBLANES + h:jj * SUBLANES + h + 1, :] = lf_pool_ref[h, pl.ds(page_id, 1), :]
        upper = (lax.broadcasted_iota(jnp.int32, (page, page), 0)
                 <= lax.broadcasted_iota(jnp.int32, (page, page), 1)).astype(BF16)
        c_pages = _dot_sel(lf_ref[...], upper)
    for jj in range(pp):
        kt = k_refs[jj][...].reshape(width, page)
        s = _dot(qb, kt.astype(BF16))
        if moba:
            ksum = kt if jj % per_blk == 0 else ksum + kt
            if jj % per_blk == per_blk - 1:
                kmean_n = jnp.sum(ksum, axis=1, keepdims=True) * (1.0 / MOBA_BLOCK)
                aux = aux + jnp.where(lane_w == j * blk_per_step + jj // per_blk, kmean_n, 0.0)
        else:
            c_page = c_pages[jj * SUBLANES:(jj + 1) * SUBLANES, :]
            s = s - _rows_per_head(c_page + aux, heads, t_new)
            aux = aux + c_page[:, page - 1:page]
        s_ref[jj] = s
    for g in range(blk_per_step):
        n = j * blk_per_step + g
        pages = range(g * per_blk, (g + 1) * per_blk)
        m_n = functools.reduce(jnp.maximum, [jnp.max(s_ref[jj], axis=1, keepdims=True) for jj in pages])
        l_n = 0.0
        for jj in pages:
            p = jnp.exp(s_ref[jj] - m_n)
            l_n = l_n + jnp.sum(p, axis=1, keepdims=True)
            p_ref[jj] = p.astype(BF16)
        m_all = jnp.where(lane == n, m_n, m_all)
        l_all = jnp.where(lane == n, l_n, l_all)
    for g in range(blk_per_step):
        part_ref[j * blk_per_step + g] = sum(
            _dot_nt(p_ref[jj], v_refs[jj][...].reshape(width, page).astype(BF16))
            for jj in range(g * per_blk, (g + 1) * per_blk))
    m_ref[...] = m_all
    l_ref[...] = l_all
    aux_ref[...] = aux

    @pl.when(j == n_steps - 1)
    def _():
        qi = lax.broadcasted_iota(jnp.int32, (rows, t_new), 0) % t_new
        kj = lax.broadcasted_iota(jnp.int32, (rows, t_new), 1)
        s_own = _dot_nt(qb, kn_ref[...].astype(BF16))
        if moba:
            gate = _dot(qbd_ref[...], aux, precision=HIGHEST)
            sel = _topk_mask(gate, lane < n_blk, MOBA_TOPK, axis=1) > 0.0
        else:
            r8 = lax.broadcasted_iota(jnp.int32, (t_new, t_new), 0)
            c8 = lax.broadcasted_iota(jnp.int32, (t_new, t_new), 1)
            c_new = (_dot(lfn_ref[...], (r8 <= c8).astype(F32), precision=HIGHEST)
                     + aux[:heads, :])
            s_own = s_own - _rows_per_head(c_new, heads, t_new)
            sel = lane < n_blk
        s_own = jnp.where(kj <= qi, s_own, NEG_BIG)
        m_own = jnp.max(s_own, axis=1, keepdims=True)
        m_tot = jnp.maximum(m_own, jnp.max(jnp.where(sel, m_all, -jnp.inf), axis=1, keepdims=True))
        p_own = jnp.exp(s_own - m_tot)
        w = jnp.where(sel, jnp.exp(m_all - m_tot), 0.0)
        l_tot = jnp.sum(p_own, axis=1, keepdims=True) + jnp.sum(w * l_all, axis=1, keepdims=True)
        acc = _dot(p_own.astype(BF16), vn_ref[...].astype(BF16))
        for n in range(n_blk):
            acc = acc + w[:, n:n + 1] * part_ref[n]
        o_ref[...] = _collapse_heads(acc / l_tot, heads, t_new)


def _decode(page_table, q, k_new, v_new, cache_kt, cache_vt, layer, n_seq, t_new, heads, lf=None, pages_per_step=32):
    n_pages = page_table.shape[1]
    pp = min(pages_per_step, n_pages)
    page = cache_kt.shape[-1]
    width = heads * HEAD_DIM
    per_blk = MOBA_BLOCK // page
    assert MOBA_BLOCK % page == 0 and n_pages % pp == 0 and pp % per_blk == 0
    n_steps = n_pages // pp
    n_blk = n_pages // per_blk
    assert n_blk <= LANES
    rows = heads * t_new
    new = pl.BlockSpec((t_new, width), lambda b, j, pt: (b, 0))

    def pages(block, index_fn):
        return [pl.BlockSpec(block, lambda b, j, pt, jj=jj: index_fn(pt[b, j * pp + jj])) for jj in range(pp)]

    kv_pages = lambda: pages((None, None, heads, HEAD_DIM, page), lambda p: (layer, p, 0, 0, 0))
    args, specs = [q, k_new, v_new], [new, new, new]
    if lf is not None:
        args.append(lf[0])
        specs.append(pl.BlockSpec((None, heads, t_new), lambda b, j, pt: (b, 0, 0)))
    args += [cache_kt] * pp + [cache_vt] * pp
    specs += kv_pages() + kv_pages()
    scratch_tail = []
    est = 2 * 2 * pp * page * (width + 8) * 4 + n_blk * rows * width * 4 + 4 * LANES * width * 4
    if lf is not None:
        args.append(lf[1])
        specs.append(pl.BlockSpec((None,) + lf[1].shape[1:], lambda b, j, pt: (layer, 0, 0, 0),
                                  pipeline_mode=pl.Buffered(1)))
        scratch_tail = [pltpu.VMEM((pp * SUBLANES, page), F32)]
        est += lf[1][0].size * 4
    aux_shape = (SUBLANES, 1) if lf is not None else (width, LANES)
    grid_spec = pltpu.PrefetchScalarGridSpec(
        num_scalar_prefetch=1,
        grid=(n_seq, n_steps),
        in_specs=specs,
        out_specs=new,
        scratch_shapes=[pltpu.VMEM((rows, width), F32), pltpu.VMEM(aux_shape, F32),
                        pltpu.VMEM((rows, LANES), F32), pltpu.VMEM((rows, LANES), F32),
                        pltpu.VMEM((n_blk, rows, width), F32),
                        pltpu.VMEM((pp, rows, page), F32), pltpu.VMEM((pp, rows, page), BF16)] + scratch_tail,
    )
    return pl.pallas_call(
        functools.partial(_decode_body, heads=heads, moba=lf is None, pp=pp, n_steps=n_steps, t_new=t_new),
        out_shape=jax.ShapeDtypeStruct((n_seq * t_new, width), F32),
        grid_spec=grid_spec,
        compiler_params=pltpu.CompilerParams(
            dimension_semantics=("arbitrary", "arbitrary"), vmem_limit_bytes=_vmem_limit(est)),
        name="moba_decode" if lf is None else "fox_decode",
    )(page_table, *args)


def _pad_w_in(w):
    sizes = (A_W,) * 3 + (B_W,) * 3 + (B_HEADS,) + (C_W,) * 4
    offs = np.cumsum((0,) + sizes)
    segs = [SEG_QA, SEG_KA, SEG_VA, SEG_QF, SEG_KF, SEG_VF, SEG_FF, SEG_QC, SEG_FC, SEG_IC, SEG_GC]
    order = np.argsort(segs)
    ends = sorted(segs)[1:] + [D_IN_PAD]
    wb = w.astype(BF16)
    parts = []
    for i, end in zip(order, ends):
        parts.append(wb[:, :, offs[i]:offs[i] + sizes[i]])
        gap = end - segs[i] - sizes[i]
        if gap:
            parts.append(jnp.zeros(w.shape[:2] + (gap,), BF16))
    return jnp.concatenate(parts, axis=2)


def _rope_tables(pos0, t, reps):
    half = HEAD_DIM // 2
    inv = ROPE_THETA ** (-jnp.arange(half, dtype=F32) * 2.0 / HEAD_DIM)
    ang = (pos0 + jnp.arange(t, dtype=F32))[:, None] * inv[None, :]
    cos, sin, zero = jnp.cos(ang), jnp.sin(ang), jnp.zeros((t, half), F32)
    heads_per_vreg = LANES // HEAD_DIM
    tabs = (jnp.concatenate([cos, cos] * heads_per_vreg, axis=1),
            jnp.concatenate([-sin, zero] * heads_per_vreg, axis=1),
            jnp.concatenate([zero, sin] * heads_per_vreg, axis=1))
    return tuple(jnp.tile(tb, (reps, 1)) for tb in tabs)


def _pad_lanes(a, width):
    return jnp.pad(a, ((0, 0), (0, width - a.shape[1])))


def _state_rows(s):
    b, h, dk, dv = s.shape
    return jnp.swapaxes(s, 2, 3).reshape(b, h * dv, dk)


def _rows_state(st, h):
    b, _, dk = st.shape
    return jnp.swapaxes(st.reshape(b, h, -1, dk), 2, 3)


def kernel(x_prompt, x_sample, cache_moba_k, cache_moba_v, cache_fox_k, cache_fox_v, cache_fox_logf, state_hgrn, page_table, norm_ffn1_g, ffn1_w_gate, ffn1_w_up, ffn1_w_down, norm_mix_g, w_in, fox_f_bias, hgrn_lb_logits, hgrn_norm_g, w_out, norm_ffn2_g, ffn2_w_gate, ffn2_w_up, ffn2_w_down, norm_final_g):
    depth = w_in.shape[0]
    batch, seq, d = x_prompt.shape
    n_seq, t_new, _ = x_sample.shape
    page = cache_moba_k.shape[2]
    past_len = page_table.shape[1] * page

    row = lambda a: a.reshape(1, -1)
    bf = lambda w: w.astype(BF16)
    ffn1 = lambda l: (row(norm_ffn1_g[l]), w1g, w1u, w1d, l)
    w1g, w1u, w1d = bf(ffn1_w_gate), bf(ffn1_w_up), bf(ffn1_w_down)
    w2g, w2u, w2d = bf(ffn2_w_gate), bf(ffn2_w_up), bf(ffn2_w_down)
    wo = bf(w_out)
    w_in_pad = _pad_w_in(w_in)
    fb = [_pad_lanes(row(fox_f_bias[l]), LANES) for l in range(depth)]
    lbl = _pad_lanes(hgrn_lb_logits.astype(F32), SEG_W)
    gn = jnp.tile(hgrn_norm_g, (1, C_HEADS))
    to_pages = lambda c: jnp.transpose(c, (0, 1, 3, 4, 2))
    ckt, cvt, fkt, fvt = (to_pages(c) for c in (cache_moba_k, cache_moba_v, cache_fox_k, cache_fox_v))
    flt = jnp.transpose(cache_fox_logf, (0, 3, 1, 2))

    def trunk(x, sample):
        if sample:
            nb, t = n_seq, t_new
            tabs = _rope_tables(float(past_len), t, nb)
        else:
            nb, t = batch, seq
            tabs = _rope_tables(0.0, t, 1)
        new, kv_t = [], None
        for l in range(depth):
            if l == 0:
                x = _ffn(x, *ffn1(l))
            args = (x, row(norm_mix_g[l]), w_in_pad, tabs, fb[l], lbl, l)
            heads_last = lambda a, h: a.reshape(nb, t, h, HEAD_DIM)
            if sample:
                qa, ka, va, qf, kf, vf, lf, qc, kc, lc, ic, gc = _inproj(*args)
                oa = _decode(page_table, qa, ka, va, ckt, cvt, l, nb, t, A_HEADS)
                lf_t = jnp.swapaxes(lf[:, :B_HEADS].reshape(nb, t, B_HEADS), 1, 2)
                of = _decode(page_table, qf, kf, vf, fkt, fvt, l, nb, t, B_HEADS, lf=(lf_t, flt))
                st0 = _state_rows(state_hgrn[l].astype(F32))
                rows_kv = (heads_last(ka, A_HEADS), heads_last(va, A_HEADS),
                           heads_last(kf, B_HEADS), heads_last(vf, B_HEADS))
            else:
                qa, ka, ka_t, va_t, qf, kf, kf_t, vf_t, lf, qc, kc, lc, ic, gc = _inproj(
                    *args, seq_t=t, carried=kv_t)
                kv_t = (ka_t, va_t, kf_t, vf_t)
                oa = _moba_prompt(qa, ka, va_t, l, nb, t)
                of = _fox_prompt(qf, kf, vf_t, lf, l, nb, t)
                st0 = jnp.zeros((nb, C_W, HEAD_DIM), F32)
                rows_kv = ()
            oc, st = _hgrn(qc, kc, lc, ic, gc, row(gn[l]), st0, nb, t)
            last = l == depth - 1
            x = _ffn(x, row(norm_ffn2_g[l]), w2g, w2u, w2d, l, mix=(oa, of, oc, wo),
                     g_final=row(norm_final_g) if last else None)
            if not last:
                x = _ffn(x, *ffn1(l + 1))
            new.append(rows_kv + (lf[:, :B_HEADS].reshape(nb, t, B_HEADS), _rows_state(st, C_HEADS)))
        outs = [jnp.stack(z) for z in zip(*new)]
        if not sample:
            heads_of = (A_HEADS, A_HEADS, B_HEADS, B_HEADS)
            outs = [jnp.transpose(a.reshape(depth, nb, h, HEAD_DIM, t), (0, 1, 4, 2, 3))
                    for a, h in zip(kv_t, heads_of)] + outs
        return x.reshape(nb, t, d), outs

    y_p, (mk_p, mv_p, fk_p, fv_p, fl_p, hs_p) = trunk(x_prompt.reshape(batch * seq, d), False)
    y_s, (mk_s, mv_s, fk_s, fv_s, fl_s, hs_s) = trunk(x_sample.reshape(n_seq * t_new, d), True)
    return (y_p, y_s, mk_p, mv_p, fk_p, fv_p, fl_p, hs_p, mk_s, mv_s, fk_s, fv_s, fl_s, hs_s)
```

```python
import functools
import math

import numpy as np
import jax
import jax.numpy as jnp
from jax import lax
from jax.experimental import pallas as pl
from jax.experimental.pallas import tpu as pltpu

F32 = jnp.float32
BF16 = jnp.bfloat16
HIGHEST = lax.Precision.HIGHEST

HEAD_DIM = 64
A_HEADS = 6
B_HEADS = 5
C_HEADS = 5
A_W = A_HEADS * HEAD_DIM
B_W = B_HEADS * HEAD_DIM
C_W = C_HEADS * HEAD_DIM
D_FF_CHUNK = 256
MOBA_BLOCK = 256
MOBA_TOPK = 3
ROPE_THETA = 10000.0
EPS = 1e-6
NEG_BIG = -1e30

LANES = 128
SUBLANES = 8
MXU_W = 256
VMEM_PHYSICAL = 64 * 1024 * 1024
VMEM_CAP = VMEM_PHYSICAL - 8 * 1024 * 1024

SEG_W = 384
SEG_QA, SEG_KA, SEG_VA, SEG_QF, SEG_KF, SEG_VF, SEG_QC, SEG_FC, SEG_IC, SEG_GC = (
    i * SEG_W for i in range(10))
SEG_FF = 10 * SEG_W
D_IN_PAD = SEG_FF + LANES


def _vmem_limit(estimate_bytes):
    return int(min(max(2 * estimate_bytes, 32 * 1024 * 1024), VMEM_CAP))


def _dot(a, b, precision=None):
    return jnp.dot(a, b, preferred_element_type=F32, precision=precision)


def _dot_nt(a, b, precision=None):
    return lax.dot_general(a, b, (((1,), (1,)), ((), ())),
                           preferred_element_type=F32, precision=precision)


def _dot_tn(a, b, precision=None):
    return lax.dot_general(a, b, (((0,), (0,)), ((), ())),
                           preferred_element_type=F32, precision=precision)


def _rms(x, g):
    ms = jnp.mean(x * x, axis=-1, keepdims=True)
    return x * lax.rsqrt(ms + EPS) * g


def _sigmoid(x):
    return 1.0 / (1.0 + jnp.exp(-x))


def _silu(x):
    return x * _sigmoid(x)


def _log_sigmoid(x):
    return jnp.minimum(x, 0.0) - jnp.log1p(jnp.exp(-jnp.abs(x)))


def _bf16_pieces(x):
    x1 = x.astype(BF16)
    r1 = x - x1.astype(F32)
    x2 = r1.astype(BF16)
    return x1, x2, (r1 - x2.astype(F32)).astype(BF16)


def _dot_sel(x, sel):
    return sum(_dot(p, sel) for p in _bf16_pieces(x))


def _sel_dot(sel, x):
    return sum(_dot(sel, p) for p in _bf16_pieces(x))


def _ffn_body(*refs, has_mix, has_final, d_ff):
    it = iter(refs)
    x_ref = next(it)
    if has_mix:
        oa_ref, of_ref, oc_ref, wo_ref = (next(it) for _ in range(4))
    g_ref, wg_ref, wu_ref, wd_ref = (next(it) for _ in range(4))
    gf_ref = next(it) if has_final else None
    out_ref = next(it)

    x = x_ref[...]
    if has_mix:
        x = (x + _dot(oa_ref[...].astype(BF16), wo_ref[:A_W, :])
             + _dot(of_ref[...].astype(BF16), wo_ref[A_W:A_W + B_W, :])
             + _dot(oc_ref[...].astype(BF16), wo_ref[A_W + B_W:, :]))
    h = _rms(x, g_ref[...]).astype(BF16)
    acc = jnp.zeros(x.shape, F32)
    for c0 in range(0, d_ff, D_FF_CHUNK):
        gate = _dot(h, wg_ref[:, c0:c0 + D_FF_CHUNK])
        up = _dot(h, wu_ref[:, c0:c0 + D_FF_CHUNK])
        act = (_silu(gate) * up).astype(BF16)
        acc = acc + _dot(act, wd_ref[c0:c0 + D_FF_CHUNK, :])
    y = x + 0.5 * acc
    if has_final:
        y = _rms(y, gf_ref[...])
    out_ref[...] = y


def _layer_spec(a, layer):
    return pl.BlockSpec((None,) + a.shape[1:], lambda *_: (layer,) + (0,) * (a.ndim - 1),
                        pipeline_mode=pl.Buffered(1))


def _ffn(x, g, wg, wu, wd, layer, mix=None, g_final=None):
    n, d = x.shape
    d_ff = wg.shape[2]
    tm = min(1024 if mix is None else 512, n)
    assert n % tm == 0 and d_ff % D_FF_CHUNK == 0
    row = lambda w: pl.BlockSpec((tm, w), lambda i: (i, 0))
    full = lambda a: pl.BlockSpec(a.shape, lambda i: (0,) * a.ndim, pipeline_mode=pl.Buffered(1))
    args, specs = [x], [row(d)]
    if mix is not None:
        oa, of, oc, wo = mix
        args += [oa, of, oc, wo]
        specs += [row(oa.shape[1]), row(of.shape[1]), row(oc.shape[1]), _layer_spec(wo, layer)]
    args += [g, wg, wu, wd]
    specs += [full(g), _layer_spec(wg, layer), _layer_spec(wu, layer), _layer_spec(wd, layer)]
    if g_final is not None:
        args.append(g_final)
        specs.append(full(g_final))
    weights = 2 * 3 * d * d_ff + (2 * d * d if mix is not None else 0)
    tiles = 2 * 2 * tm * d * 4 * (2 if mix is not None else 1) + 6 * tm * d * 4
    return pl.pallas_call(
        functools.partial(_ffn_body, has_mix=mix is not None, has_final=g_final is not None, d_ff=d_ff),
        out_shape=jax.ShapeDtypeStruct((n, d), F32),
        grid=(n // tm,),
        in_specs=specs,
        out_specs=row(d),
        compiler_params=pltpu.CompilerParams(
            dimension_semantics=("arbitrary",), vmem_limit_bytes=_vmem_limit(weights + tiles)),
        name="ffn_mix" if mix is not None else "ffn",
    )(*args)


def _inproj_body(x_ref, g_ref, w_ref, cos_ref, slo_ref, shi_ref, fb_ref, lbl_ref, *out_refs, layer, transposed,
                 n_carried):
    out_refs = out_refs[n_carried:]
    if transposed:
        (qa_ref, ka_ref, kat_ref, vat_ref, qf_ref, kf_ref, kft_ref, vft_ref, lf_ref,
         qc_ref, kc_ref, lc_ref, ic_ref, gc_ref, stage_ref) = out_refs
    else:
        (qa_ref, ka_ref, va_ref, qf_ref, kf_ref, vf_ref, lf_ref,
         qc_ref, kc_ref, lc_ref, ic_ref, gc_ref) = out_refs
    h = _rms(x_ref[...], g_ref[...]).astype(BF16)

    def proj(c0, width=SEG_W):
        return _dot(h, w_ref[:, c0:c0 + width])

    cos, slo, shi = cos_ref[...], slo_ref[...], shi_ref[...]

    def rope(p):
        parts = []
        for c in range(SEG_W // LANES):
            pc = p[:, c * LANES:(c + 1) * LANES]
            parts.append(pc * cos + pltpu.roll(pc, LANES - HEAD_DIM // 2, 1) * slo
                         + pltpu.roll(pc, HEAD_DIM // 2, 1) * shi)
        return jnp.concatenate(parts, axis=1)

    qa_ref[...] = rope(proj(SEG_QA))
    ka = rope(proj(SEG_KA))
    ka_ref[...] = ka
    if transposed:
        def transposed_tile(p):
            stage_ref[...] = p
            return stage_ref[...].T

        kat_ref[...] = ka.T
        vat_ref[...] = transposed_tile(proj(SEG_VA))
        qf_ref[...] = proj(SEG_QF)
        kf_ref[...] = proj(SEG_KF)
        kft_ref[...] = kf_ref[...].T[:B_W, :]
        vft_ref[...] = transposed_tile(proj(SEG_VF))[:B_W, :]
    else:
        va_ref[...] = proj(SEG_VA)
        qf_ref[...] = proj(SEG_QF)[:, :B_W]
        kf_ref[...] = proj(SEG_KF)[:, :B_W]
        vf_ref[...] = proj(SEG_VF)[:, :B_W]
    lf_ref[...] = _log_sigmoid(proj(SEG_FF, LANES) + fb_ref[...])

    lbl = lbl_ref[...]
    e = jnp.exp(lbl - jnp.max(lbl, axis=0, keepdims=True))
    psm = e / jnp.sum(e, axis=0, keepdims=True)
    lb = jnp.zeros((1, SEG_W), F32)
    for j in range(1, layer + 1):
        lb = lb + psm[j:j + 1, :]
    fc = proj(SEG_FC)
    lc_ref[...] = jnp.log(lb + (1.0 - lb) * _sigmoid(fc))[:, :C_W]
    kc_ref[...] = ((1.0 - lb) * _sigmoid(-fc))[:, :C_W]
    qc_ref[...] = _silu(proj(SEG_QC))[:, :C_W]
    ic_ref[...] = proj(SEG_IC)[:, :C_W]
    gc_ref[...] = _silu(proj(SEG_GC))[:, :C_W]


def _inproj(x, g, w_pad, rope_tabs, fb, lbl, layer, seq_t=None, carried=None, tm=512):
    n, d = x.shape
    depth = w_pad.shape[0]
    tm = min(tm, n)
    assert n % tm == 0 and rope_tabs[0].shape[0] % tm == 0
    row = lambda w: pl.BlockSpec((tm, w), lambda i: (i, 0))
    full = lambda a: pl.BlockSpec(a.shape, lambda i: (0,) * a.ndim, pipeline_mode=pl.Buffered(1))
    n_tab = rope_tabs[0].shape[0] // tm
    tab = pl.BlockSpec((tm, LANES), lambda i: (i % n_tab, 0))
    args = [x, g, w_pad, *rope_tabs, fb, lbl]
    in_specs = [row(d), full(g), _layer_spec(w_pad, layer), tab, tab, tab, full(fb), full(lbl)]
    aliases = {}
    if seq_t is None:
        widths = [A_W, A_W, A_W, B_W, B_W, B_W, LANES, C_W, C_W, C_W, C_W, C_W]
        shapes = [(n, w) for w in widths]
        out_specs = [row(w) for w in widths]
    else:
        assert seq_t % tm == 0
        per_seq = seq_t // tm
        col = lambda w: pl.BlockSpec((None, None, w, tm), lambda i: (layer, i // per_seq, 0, i % per_seq))
        kinds = [(row, A_W), (row, A_W), (col, A_W), (col, A_W), (row, SEG_W), (row, SEG_W), (col, B_W),
                 (col, B_W), (row, LANES), (row, C_W), (row, C_W), (row, C_W), (row, C_W), (row, C_W)]
        widths = [w for _, w in kinds]
        shapes = [(n, w) if f is row else (depth, n // seq_t, w, seq_t) for f, w in kinds]
        out_specs = [f(w) for f, w in kinds]
        if carried is not None:
            stacked = [i for i, (f, _) in enumerate(kinds) if f is col]
            aliases = {len(args) + k: i for k, i in enumerate(stacked)}
            args += list(carried)
            in_specs += [pl.BlockSpec(memory_space=pl.ANY)] * len(carried)
    est = 2 * d * D_IN_PAD + 2 * tm * 4 * (d + sum(widths) + 3 * LANES) + 8 * tm * SEG_W * 4
    return pl.pallas_call(
        functools.partial(_inproj_body, layer=layer, transposed=seq_t is not None, n_carried=len(aliases)),
        out_shape=[jax.ShapeDtypeStruct(s, F32) for s in shapes],
        grid=(n // tm,),
        in_specs=in_specs,
        out_specs=out_specs,
        scratch_shapes=[pltpu.VMEM((tm, SEG_W), F32)] if seq_t is not None else [],
        input_output_aliases=aliases,
        compiler_params=pltpu.CompilerParams(
            dimension_semantics=("arbitrary",), vmem_limit_bytes=_vmem_limit(est)),
        name="inproj",
    )(*args)


def _topk_mask(gate, allowed, topk, axis):
    n = gate.shape[axis]
    pos = lax.broadcasted_iota(jnp.int32, gate.shape, axis).astype(F32)
    g = jnp.where(allowed, gate, NEG_BIG)
    sel = jnp.zeros(gate.shape, F32)
    for _ in range(topk):
        m = jnp.max(g, axis=axis, keepdims=True)
        idx = jnp.min(jnp.where(g == m, pos, float(n)), axis=axis, keepdims=True)
        hit = pos == idx
        sel = jnp.where(hit & allowed, 1.0, sel)
        g = jnp.where(hit, -jnp.inf, g)
    return sel


def _own_lanes(shape, h):
    return lax.broadcasted_iota(jnp.int32, shape, 1) // HEAD_DIM == h % 2


def _flash_step(heads, score_fn, value_fn, s_ref, p_ref, m_ref, l_ref, acc_ref):
    for h in range(heads):
        s_ref[h] = score_fn(h)
    alphas = []
    for h in range(heads):
        m_old = m_ref[h]
        m_new = jnp.maximum(m_old, jnp.max(s_ref[h], axis=0, keepdims=True))
        alpha = jnp.exp(m_old - m_new)
        p = jnp.exp(s_ref[h] - m_new)
        l_ref[h] = alpha * l_ref[h] + jnp.sum(p, axis=0, keepdims=True)
        m_ref[h] = m_new
        p_ref[h] = p.astype(BF16)
        alphas.append(alpha)
    for h in range(heads):
        acc_ref[h] = acc_ref[h] * alphas[h] + _dot(value_fn(h), p_ref[h])


def _flash_init(m_ref, l_ref, acc_ref):
    m_ref[...] = jnp.full(m_ref.shape, -jnp.inf, F32)
    l_ref[...] = jnp.zeros(l_ref.shape, F32)
    acc_ref[...] = jnp.zeros(acc_ref.shape, F32)


def _flash_finish(o_ref, l_ref, acc_ref, heads):
    blk = acc_ref.shape[2]
    parts = [acc_ref[h] / l_ref[h] for h in range(heads)]
    parts += [jnp.zeros((HEAD_DIM, blk), F32)] * (SEG_W // HEAD_DIM - heads)
    o_ref[...] = jnp.concatenate(parts, axis=0).T[:, :o_ref.shape[1]]


def _flash_scratch(heads, blk):
    return [pltpu.VMEM((heads, blk, blk), F32), pltpu.VMEM((heads, blk, blk), BF16),
            pltpu.VMEM((heads, 1, blk), F32), pltpu.VMEM((heads, 1, blk), F32),
            pltpu.VMEM((heads, HEAD_DIM, blk), F32)]


def _moba_prompt_body(q_ref, k_ref, vt_ref, o_ref, kmean_ref, k16_ref, vt16_ref, qm_ref, sel_ref,
                      s_ref, p_ref, m_ref, l_ref, acc_ref, *, blk):
    i = pl.program_id(1)
    nb = k_ref.shape[0] // blk
    scale = HEAD_DIM ** -0.5

    @pl.when(i == 0)
    def _():
        for n in range(nb):
            kb = k_ref[n * blk:(n + 1) * blk, :]
            kmean_ref[n:n + 1, :] = jnp.sum(kb, axis=0, keepdims=True) * (1.0 / blk)
            k16_ref[n * blk:(n + 1) * blk, :] = kb.astype(BF16)
            vt16_ref[n] = vt_ref[:, n * blk:(n + 1) * blk].astype(BF16)

    past = lax.broadcasted_iota(jnp.int32, (nb, blk), 0) < i
    for h in range(A_HEADS):
        c0 = (h // 2) * LANES
        qwin = q_ref[:, c0:c0 + LANES]
        kmean_h = jnp.where(_own_lanes((nb, LANES), h), kmean_ref[:, c0:c0 + LANES], 0.0)
        gate = _dot_nt(kmean_h, qwin, precision=HIGHEST)
        sel_ref[h] = _topk_mask(gate, past, MOBA_TOPK, axis=0)
        qm_ref[h] = jnp.where(_own_lanes((blk, LANES), h), qwin * scale, 0.0).astype(BF16)
    _flash_init(m_ref, l_ref, acc_ref)
    causal = (lax.broadcasted_iota(jnp.int32, (blk, blk), 0)
              <= lax.broadcasted_iota(jnp.int32, (blk, blk), 1))

    def step(n, mask_fn):
        rows = pl.ds(pl.multiple_of(n * blk, blk), blk)
        _flash_step(
            A_HEADS,
            lambda h: mask_fn(h, _dot_nt(k16_ref[rows, (h // 2) * LANES:(h // 2 + 1) * LANES], qm_ref[h])),
            lambda h: vt16_ref[n, h * HEAD_DIM:(h + 1) * HEAD_DIM, :],
            s_ref, p_ref, m_ref, l_ref, acc_ref)

    step(i, lambda h, st: jnp.where(causal, st, NEG_BIG))

    def body(n, carry):
        step(n, lambda h, st: jnp.where(sel_ref[h, pl.ds(n, 1), :] > 0.0, st, NEG_BIG))
        return carry

    lax.fori_loop(0, i, body, 0)
    _flash_finish(o_ref, l_ref, acc_ref, A_HEADS)


def _moba_prompt(q, k, vt, layer, batch, seq):
    blk = MOBA_BLOCK
    assert seq % blk == 0
    nq = seq // blk
    est = 2 * 2 * seq * A_W * 4 + 2 * seq * A_W * 2 + 4 * blk * A_W * 4 + 24 * blk * blk * 4
    return pl.pallas_call(
        functools.partial(_moba_prompt_body, blk=blk),
        out_shape=jax.ShapeDtypeStruct(q.shape, F32),
        grid=(batch, nq),
        in_specs=[pl.BlockSpec((blk, A_W), lambda b, i: (b * nq + i, 0)),
                  pl.BlockSpec((seq, A_W), lambda b, i: (b, 0)),
                  pl.BlockSpec((None, None, A_W, seq), lambda b, i: (layer, b, 0, 0))],
        out_specs=pl.BlockSpec((blk, A_W), lambda b, i: (b * nq + i, 0)),
        scratch_shapes=[pltpu.VMEM((nq, A_W), F32), pltpu.VMEM((seq, A_W), BF16),
                        pltpu.VMEM((nq, A_W, blk), BF16), pltpu.VMEM((A_HEADS, blk, LANES), BF16),
                        pltpu.VMEM((A_HEADS, nq, blk), F32)] + _flash_scratch(A_HEADS, blk),
        compiler_params=pltpu.CompilerParams(
            dimension_semantics=("arbitrary", "arbitrary"), vmem_limit_bytes=_vmem_limit(est)),
        name="moba_prompt",
    )(q, k, vt)


FOX_AUG = 8


EXP_ZERO = 104.0
NORM_SLACK = 1.02


def _fox_prompt_body(q_ref, k_ref, vt_ref, lf_ref, o_ref, bq_ref, kaug_ref, vt16_ref, qaug_ref, c_ref, kn_ref,
                     s_ref, p_ref, m_ref, l_ref, acc_ref, *, blk):
    i = pl.program_id(1)
    nb = k_ref.shape[0] // blk
    scale = HEAD_DIM ** -0.5
    key_i = lax.broadcasted_iota(jnp.int32, (blk, blk), 0)
    qry_i = lax.broadcasted_iota(jnp.int32, (blk, blk), 1)
    lane1 = lax.broadcasted_iota(jnp.int32, (1, LANES), 1)
    head_of_col = (lax.broadcasted_iota(jnp.int32, (SEG_W, LANES), 0) // HEAD_DIM
                   == lax.broadcasted_iota(jnp.int32, (SEG_W, LANES), 1)).astype(BF16)

    def max_norm(x):
        return jnp.sqrt(jnp.max(_dot_sel(x * x, head_of_col), axis=0, keepdims=True))

    @pl.when(i == 0)
    def _():
        tri = (qry_i <= key_i).astype(F32)
        head = lax.broadcasted_iota(jnp.int32, (LANES, LANES), 0)
        lane = lax.broadcasted_iota(jnp.int32, (LANES, LANES), 1)
        place = lambda off: ((lane == head * FOX_AUG + off) & (head < B_HEADS)).astype(BF16)
        used = lane1 < B_HEADS * FOX_AUG
        ones_q = (used & (lane1 % FOX_AUG < 3)).astype(F32)
        ones_k = (used & (lane1 % FOX_AUG >= 3) & (lane1 % FOX_AUG < 6)).astype(F32)
        carry = jnp.zeros((1, LANES), F32)
        for n in range(nb):
            rows = slice(n * blk, (n + 1) * blk)
            c = _dot(tri, lf_ref[rows, :], precision=HIGHEST) + carry
            carry = c[blk - 1:blk, :]
            c_ref[rows, :] = c
            kn_ref[n:n + 1, :] = max_norm(k_ref[rows, :])
            pieces = _bf16_pieces(c)
            bq = ones_q + sum(_dot(pieces[j], place(3 + j)) for j in range(3))
            ak = ones_k - sum(_dot(pieces[j], place(j)) for j in range(3))
            bq_ref[rows, :] = bq.astype(BF16)
            for w in range(SEG_W // LANES):
                kaug_ref[w, rows, :LANES] = k_ref[rows, w * LANES:(w + 1) * LANES].astype(BF16)
                kaug_ref[w, rows, LANES:] = ak.astype(BF16)
            vt16_ref[n] = vt_ref[:, rows].astype(BF16)

    bq_rows = bq_ref[pl.ds(pl.multiple_of(i * blk, blk), blk), :]
    group = lax.broadcasted_iota(jnp.int32, (blk, LANES), 1) // FOX_AUG
    for h in range(B_HEADS):
        c0 = (h // 2) * LANES
        qaug_ref[h, :, :LANES] = jnp.where(_own_lanes((blk, LANES), h), q_ref[:, c0:c0 + LANES] * scale,
                                           0.0).astype(BF16)
        qaug_ref[h, :, LANES:] = jnp.where(group == h, bq_rows, jnp.zeros_like(bq_rows))
    _flash_init(m_ref, l_ref, acc_ref)
    causal = key_i <= qry_i

    def step(n, diag):
        rows = pl.ds(pl.multiple_of(n * blk, blk), blk)

        def scores(h):
            st = _dot_nt(kaug_ref[h // 2, rows, :], qaug_ref[h])
            return jnp.where(causal, st, NEG_BIG) if diag else st

        _flash_step(B_HEADS, scores, lambda h: vt16_ref[n, h * HEAD_DIM:(h + 1) * HEAD_DIM, :],
                    s_ref, p_ref, m_ref, l_ref, acc_ref)

    step(i, True)

    m_low = jnp.zeros((1, LANES), F32)
    for h in range(B_HEADS):
        m_low = jnp.where(lane1 == h, jnp.min(m_ref[h], axis=1, keepdims=True), m_low)
    q_norm = max_norm(q_ref[...] * scale)
    c_first = c_ref[pl.ds(pl.multiple_of(i * blk, blk), 1), :]

    def body(n, carry):
        c_last = c_ref[pl.ds(n * blk + blk - 1, 1), :]
        reach = NORM_SLACK * q_norm * kn_ref[pl.ds(n, 1), :] + (c_first - c_last) - m_low
        live = jnp.max(jnp.where(lane1 < B_HEADS, reach, -jnp.inf)) > -EXP_ZERO

        @pl.when(live)
        def _():
            step(n, False)

        return carry

    lax.fori_loop(0, i, body, 0)
    _flash_finish(o_ref, l_ref, acc_ref, B_HEADS)


def _fox_prompt(q, k, vt, lf, layer, batch, seq, blk=256):
    assert seq % blk == 0
    nq = seq // blk
    est = (2 * seq * (SEG_W + B_W + LANES) * 4 + seq * (3 * 2 * LANES + LANES + B_W) * 2
           + 4 * blk * SEG_W * 4 + 24 * blk * blk * 4)
    return pl.pallas_call(
        functools.partial(_fox_prompt_body, blk=blk),
        out_shape=jax.ShapeDtypeStruct((batch * seq, B_W), F32),
        grid=(batch, nq),
        in_specs=[pl.BlockSpec((blk, SEG_W), lambda b, i: (b * nq + i, 0)),
                  pl.BlockSpec((seq, SEG_W), lambda b, i: (b, 0)),
                  pl.BlockSpec((None, None, B_W, seq), lambda b, i: (layer, b, 0, 0)),
                  pl.BlockSpec((seq, LANES), lambda b, i: (b, 0))],
        out_specs=pl.BlockSpec((blk, B_W), lambda b, i: (b * nq + i, 0)),
        scratch_shapes=[pltpu.VMEM((seq, LANES), BF16), pltpu.VMEM((SEG_W // LANES, seq, 2 * LANES), BF16),
                        pltpu.VMEM((nq, B_W, blk), BF16), pltpu.VMEM((B_HEADS, blk, 2 * LANES), BF16),
                        pltpu.VMEM((seq, LANES), F32), pltpu.VMEM((nq, LANES), F32)]
        + _flash_scratch(B_HEADS, blk),
        compiler_params=pltpu.CompilerParams(
            dimension_semantics=("arbitrary", "arbitrary"), vmem_limit_bytes=_vmem_limit(est)),
        name="fox_prompt",
    )(q, k, vt, lf)


def _hgrn_body(q_ref, k_ref, lf_ref, v_ref, gs_ref, gn_ref, st0_ref, o_ref, sto_ref, st_ref,
               qd_ref, kd_ref, v16_ref, dl_ref, upd_ref, *, sub, tile):
    seq = q_ref.shape[0]
    n_sub = tile // sub
    t_i = lax.broadcasted_iota(jnp.int32, (tile, tile), 0)
    s_i = lax.broadcasted_iota(jnp.int32, (tile, tile), 1)
    same_sub = t_i // sub == s_i // sub
    tri_blk = (same_sub & (s_i <= t_i)).astype(BF16)
    e_i = lax.broadcasted_iota(jnp.int32, (C_W, C_W), 0) // HEAD_DIM
    d_i = lax.broadcasted_iota(jnp.int32, (C_W, C_W), 1) // HEAD_DIM
    same_head = e_i == d_i
    ones_bd = same_head.astype(BF16)
    row_in_sub = lax.broadcasted_iota(jnp.int32, (n_sub, sub, C_W), 1)
    split = lambda a: a.reshape(n_sub, sub, C_W)

    def tile_rows(t):
        return pl.ds(pl.multiple_of(t * tile, tile), tile)

    def decay_pass(t, _):
        rows = tile_rows(t)
        lf, q, k, v = lf_ref[rows, :], q_ref[rows, :], k_ref[rows, :], v_ref[rows, :]
        cum = _sel_dot(tri_blk, lf)
        cum3, q3, k3, v3 = split(cum), split(q), split(k), split(v)
        last3 = cum3[:, sub - 1:sub, :]
        qd_ref[rows, :] = (q * jnp.exp(cum)).astype(BF16)
        kd_ref[rows, :] = (k3 * jnp.exp(last3 - cum3)).reshape(tile, C_W).astype(BF16)
        v16_ref[rows, :] = v.astype(BF16)
        dl_ref[rows, :] = jnp.broadcast_to(jnp.exp(last3), (n_sub, sub, C_W)).reshape(tile, C_W)
        n_t = sub // SUBLANES
        tiles = lambda a: a.reshape(n_sub, n_t, SUBLANES, C_W)
        cum4, q4 = tiles(cum), tiles(q)
        row8 = lax.broadcasted_iota(jnp.int32, (n_sub, SUBLANES, C_W), 1)
        o_tiles = [jnp.zeros((n_sub, SUBLANES, C_W), F32) for _ in range(n_t)]
        for s in range(sub):
            for t in range(s // SUBLANES, n_t):
                dec = jnp.exp(jnp.where(row8 + t * SUBLANES >= s, cum4[:, t] - cum3[:, s:s + 1, :], NEG_BIG))
                m = (q4[:, t] * k3[:, s:s + 1, :] * dec).reshape(n_sub * SUBLANES, C_W).astype(BF16)
                w = jnp.concatenate([_dot(m[:, :MXU_W], ones_bd[:MXU_W, :MXU_W]),
                                     _dot(m[:, MXU_W:], ones_bd[MXU_W:, MXU_W:])], axis=1)
                o_tiles[t] = o_tiles[t] + w.reshape(n_sub, SUBLANES, C_W) * v3[:, s:s + 1, :]
        o_ref[rows, :] = jnp.stack(o_tiles, axis=1).reshape(tile, C_W)
        return 0

    lax.fori_loop(0, seq // tile, decay_pass, 0)
    spread = (lax.broadcasted_iota(jnp.int32, (HEAD_DIM, C_W), 0)
              == lax.broadcasted_iota(jnp.int32, (HEAD_DIM, C_W), 1) % HEAD_DIM).astype(BF16)
    st_ref[...] = jnp.where(same_head, _dot_sel(st0_ref[...], spread), 0.0)

    group = upd_ref.shape[0]

    def chunks(g, _):
        for j in range(group):
            rows = pl.ds(pl.multiple_of((g * group + j) * sub, sub), sub)
            upd_ref[j] = jnp.where(same_head, _dot_tn(v16_ref[rows, :], kd_ref[rows, :]), 0.0)
        for j in range(group):
            c = g * group + j
            rows = pl.ds(pl.multiple_of(c * sub, sub), sub)
            st = st_ref[...]
            o_ref[rows, :] = o_ref[rows, :] + _dot_nt(qd_ref[rows, :], st.astype(BF16))
            st_ref[...] = st * dl_ref[pl.ds(c * sub, 1), :] + upd_ref[j]
        return 0

    lax.fori_loop(0, seq // (sub * group), chunks, 0)

    gn = gn_ref[...]

    def norm_pass(t, _):
        rows = tile_rows(t)
        o = o_ref[rows, :]
        sq = o * o
        hi = sq.astype(BF16)
        lo = (sq - hi.astype(F32)).astype(BF16)
        ms = (_dot(hi, ones_bd) + _dot(lo, ones_bd)) * (1.0 / HEAD_DIM)
        o_ref[rows, :] = o * lax.rsqrt(ms + EPS) * gn * gs_ref[rows, :]
        return 0

    lax.fori_loop(0, seq // tile, norm_pass, 0)
    gather = (lax.broadcasted_iota(jnp.int32, (C_W, HEAD_DIM), 0) % HEAD_DIM
              == lax.broadcasted_iota(jnp.int32, (C_W, HEAD_DIM), 1)).astype(BF16)
    sto_ref[...] = _dot_sel(st_ref[...], gather)


def _hgrn(q, k, lf, v, gs, gn, st0, batch, seq):
    sub = math.gcd(seq, 16)
    tile = min(seq, 256)
    assert seq % tile == 0 and tile % sub == 0
    row = pl.BlockSpec((seq, C_W), lambda b: (b, 0))
    st_spec = pl.BlockSpec((None, C_W, HEAD_DIM), lambda b: (b, 0, 0))
    group = math.gcd(seq // sub, 8)
    est = (2 * 6 * seq * C_W * 4 + (8 + group) * C_W * C_W * 4 + seq * C_W * (3 * 2 + 4)
           + 24 * tile * C_W * 4)
    return pl.pallas_call(
        functools.partial(_hgrn_body, sub=sub, tile=tile),
        out_shape=[jax.ShapeDtypeStruct((batch * seq, C_W), F32),
                   jax.ShapeDtypeStruct((batch, C_W, HEAD_DIM), F32)],
        grid=(batch,),
        in_specs=[row, row, row, row, row, pl.BlockSpec((1, C_W), lambda b: (0, 0)), st_spec],
        out_specs=[row, st_spec],
        scratch_shapes=[pltpu.VMEM((C_W, C_W), F32),
                        pltpu.VMEM((seq, C_W), BF16), pltpu.VMEM((seq, C_W), BF16),
                        pltpu.VMEM((seq, C_W), BF16), pltpu.VMEM((seq, C_W), F32),
                        pltpu.VMEM((group, C_W, C_W), F32)],
        compiler_params=pltpu.CompilerParams(
            dimension_semantics=("arbitrary",), vmem_limit_bytes=_vmem_limit(est)),
        name="hgrn",
    )(q, k, lf, v, gs, gn, st0)


def _block_diag_q(q, heads, scale):
    t, w = q.shape
    rep = jnp.concatenate([q] * heads, axis=0)
    r = lax.broadcasted_iota(jnp.int32, (heads * t, w), 0) // t
    c = lax.broadcasted_iota(jnp.int32, (heads * t, w), 1) // HEAD_DIM
    return jnp.where(r == c, rep * scale, 0.0)


def _collapse_heads(o, heads, t):
    r = lax.broadcasted_iota(jnp.int32, o.shape, 0) // t
    c = lax.broadcasted_iota(jnp.int32, o.shape, 1) // HEAD_DIM
    o = jnp.where(r == c, o, 0.0)
    out = o[0:t, :]
    for h in range(1, heads):
        out = out + o[h * t:(h + 1) * t, :]
    return out


def _rows_per_head(x, heads, t):
    return jnp.concatenate([jnp.broadcast_to(x[h:h + 1, :], (t, x.shape[1])) for h in range(heads)], axis=0)


def _decode_body(pt_ref, q_ref, kn_ref, vn_ref, *rest, heads, moba, pp, n_steps, t_new):
    width = heads * HEAD_DIM
    if moba:
        k_refs, v_refs, rest = rest[:pp], rest[pp:2 * pp], rest[2 * pp:]
    else:
        lfn_ref, rest = rest[0], rest[1:]
        k_refs, v_refs, lf_pool_ref, rest = rest[:pp], rest[pp:2 * pp], rest[2 * pp], rest[2 * pp + 1:]
        lf_ref, rest = rest[-1], rest[:-1]
    o_ref, qbd_ref, aux_ref, m_ref, l_ref, part_ref, s_ref, p_ref = rest
    j = pl.program_id(1)
    page = k_refs[0].shape[-1]
    rows = heads * t_new
    per_blk = MOBA_BLOCK // page
    blk_per_step = pp // per_blk
    n_blk = n_steps * blk_per_step
    scale = HEAD_DIM ** -0.5
    lane = lax.broadcasted_iota(jnp.int32, (rows, LANES), 1)

    @pl.when(j == 0)
    def _():
        qbd_ref[...] = _block_diag_q(q_ref[...], heads, 1.0)
        aux_ref[...] = jnp.zeros(aux_ref.shape, F32)
        m_ref[...] = jnp.zeros(m_ref.shape, F32)
        l_ref[...] = jnp.zeros(l_ref.shape, F32)
        if not moba:
            lf_ref[...] = jnp.zeros(lf_ref.shape, F32)

    qb = (qbd_ref[...] * scale).astype(BF16)
    m_all, l_all = m_ref[...], l_ref[...]
    aux = aux_ref[...]
    if moba:
        lane_w = lax.broadcasted_iota(jnp.int32, (width, LANES), 1)
    else:
        seq_i = pl.program_id(0)
        for jj in range(pp):
            page_id = pt_ref[seq_i, j * pp + jj]
            for h in range(heads):
                lf_ref[jj * SUBLANES + h:jj * SUBLANES + h + 1, :] = lf_pool_ref[h, pl.ds(page_id, 1), :]
        upper = (lax.broadcasted_iota(jnp.int32, (page, page), 0)
                 <= lax.broadcasted_iota(jnp.int32, (page, page), 1)).astype(BF16)
        c_pages = _dot_sel(lf_ref[...], upper)
    for jj in range(pp):
        kt = k_refs[jj][...].reshape(width, page)
        s = _dot(qb, kt.astype(BF16))
        if moba:
            ksum = kt if jj % per_blk == 0 else ksum + kt
            if jj % per_blk == per_blk - 1:
                kmean_n = jnp.sum(ksum, axis=1, keepdims=True) * (1.0 / MOBA_BLOCK)
                aux = aux + jnp.where(lane_w == j * blk_per_step + jj // per_blk, kmean_n, 0.0)
        else:
            c_page = c_pages[jj * SUBLANES:(jj + 1) * SUBLANES, :]
            s = s - _rows_per_head(c_page + aux, heads, t_new)
            aux = aux + c_page[:, page - 1:page]
        s_ref[jj] = s
    for g in range(blk_per_step):
        n = j * blk_per_step + g
        pages = range(g * per_blk, (g + 1) * per_blk)
        m_n = functools.reduce(jnp.maximum, [jnp.max(s_ref[jj], axis=1, keepdims=True) for jj in pages])
        l_n = 0.0
        for jj in pages:
            p = jnp.exp(s_ref[jj] - m_n)
            l_n = l_n + jnp.sum(p, axis=1, keepdims=True)
            p_ref[jj] = p.astype(BF16)
        m_all = jnp.where(lane == n, m_n, m_all)
        l_all = jnp.where(lane == n, l_n, l_all)
    for g in range(blk_per_step):
        part_ref[j * blk_per_step + g] = sum(
            _dot_nt(p_ref[jj], v_refs[jj][...].reshape(width, page).astype(BF16))
            for jj in range(g * per_blk, (g + 1) * per_blk))
    m_ref[...] = m_all
    l_ref[...] = l_all
    aux_ref[...] = aux

    @pl.when(j == n_steps - 1)
    def _():
        qi = lax.broadcasted_iota(jnp.int32, (rows, t_new), 0) % t_new
        kj = lax.broadcasted_iota(jnp.int32, (rows, t_new), 1)
        s_own = _dot_nt(qb, kn_ref[...].astype(BF16))
        if moba:
            gate = _dot(qbd_ref[...], aux, precision=HIGHEST)
            sel = _topk_mask(gate, lane < n_blk, MOBA_TOPK, axis=1) > 0.0
        else:
            r8 = lax.broadcasted_iota(jnp.int32, (t_new, t_new), 0)
            c8 = lax.broadcasted_iota(jnp.int32, (t_new, t_new), 1)
            c_new = (_dot(lfn_ref[...], (r8 <= c8).astype(F32), precision=HIGHEST)
                     + aux[:heads, :])
            s_own = s_own - _rows_per_head(c_new, heads, t_new)
            sel = lane < n_blk
        s_own = jnp.where(kj <= qi, s_own, NEG_BIG)
        m_own = jnp.max(s_own, axis=1, keepdims=True)
        m_tot = jnp.maximum(m_own, jnp.max(jnp.where(sel, m_all, -jnp.inf), axis=1, keepdims=True))
        p_own = jnp.exp(s_own - m_tot)
        w = jnp.where(sel, jnp.exp(m_all - m_tot), 0.0)
        l_tot = jnp.sum(p_own, axis=1, keepdims=True) + jnp.sum(w * l_all, axis=1, keepdims=True)
        acc = _dot(p_own.astype(BF16), vn_ref[...].astype(BF16))
        for n in range(n_blk):
            acc = acc + w[:, n:n + 1] * part_ref[n]
        o_ref[...] = _collapse_heads(acc / l_tot, heads, t_new)


def _decode_part(page_table, q, k_new, v_new, cache_kt, cache_vt, layer, n_seq, t_new, heads, lf, pp):
    n_pages = page_table.shape[1]
    page = cache_kt.shape[-1]
    width = heads * HEAD_DIM
    per_blk = MOBA_BLOCK // page
    assert MOBA_BLOCK % page == 0 and n_pages % pp == 0 and pp % per_blk == 0
    n_steps = n_pages // pp
    n_blk = n_pages // per_blk
    assert n_blk <= LANES
    rows = heads * t_new
    new = pl.BlockSpec((t_new, width), lambda b, j, pt: (b, 0))

    def pages(block, index_fn):
        return [pl.BlockSpec(block, lambda b, j, pt, jj=jj: index_fn(pt[b, j * pp + jj])) for jj in range(pp)]

    kv_pages = lambda: pages((None, None, heads, HEAD_DIM, page), lambda p: (layer, p, 0, 0, 0))
    args, specs = [q, k_new, v_new], [new, new, new]
    if lf is not None:
        args.append(lf[0])
        specs.append(pl.BlockSpec((None, heads, t_new), lambda b, j, pt: (b, 0, 0)))
    args += [cache_kt] * pp + [cache_vt] * pp
    specs += kv_pages() + kv_pages()
    scratch_tail = []
    est = 2 * 2 * pp * page * (width + 8) * 4 + n_blk * rows * width * 4 + 4 * LANES * width * 4
    if lf is not None:
        args.append(lf[1])
        specs.append(pl.BlockSpec((None,) + lf[1].shape[1:], lambda b, j, pt: (layer, 0, 0, 0),
                                  pipeline_mode=pl.Buffered(1)))
        scratch_tail = [pltpu.VMEM((pp * SUBLANES, page), F32)]
        est += lf[1][0].size * 4
    aux_shape = (SUBLANES, 1) if lf is not None else (width, LANES)
    scratch = [pltpu.VMEM((rows, width), F32), pltpu.VMEM(aux_shape, F32),
               pltpu.VMEM((rows, LANES), F32), pltpu.VMEM((rows, LANES), F32),
               pltpu.VMEM((n_blk, rows, width), F32),
               pltpu.VMEM((pp, rows, page), F32), pltpu.VMEM((pp, rows, page), BF16)] + scratch_tail
    body = functools.partial(_decode_body, heads=heads, moba=lf is None, pp=pp, n_steps=n_steps, t_new=t_new)
    return dict(args=args, specs=specs, out_spec=new, out_shape=jax.ShapeDtypeStruct((n_seq * t_new, width), F32),
                scratch=scratch, body=body, est=est, n_steps=n_steps)


def _decode_pair(page_table, layer, n_seq, t_new, moba, fox, pages_per_step=16):
    pp = min(pages_per_step, page_table.shape[1])
    a = _decode_part(page_table, *moba, layer, n_seq, t_new, A_HEADS, None, pp)
    b = _decode_part(page_table, *fox[:5], layer, n_seq, t_new, B_HEADS, fox[5], pp)
    n_a, n_b, s_a = len(a["args"]), len(b["args"]), len(a["scratch"])

    def body(pt_ref, *refs):
        out_a, out_b = refs[n_a + n_b], refs[n_a + n_b + 1]
        scratch = refs[n_a + n_b + 2:]
        a["body"](pt_ref, *refs[:n_a], out_a, *scratch[:s_a])
        b["body"](pt_ref, *refs[n_a:n_a + n_b], out_b, *scratch[s_a:])

    grid_spec = pltpu.PrefetchScalarGridSpec(
        num_scalar_prefetch=1,
        grid=(n_seq, a["n_steps"]),
        in_specs=a["specs"] + b["specs"],
        out_specs=[a["out_spec"], b["out_spec"]],
        scratch_shapes=a["scratch"] + b["scratch"],
    )
    return pl.pallas_call(
        body,
        out_shape=[a["out_shape"], b["out_shape"]],
        grid_spec=grid_spec,
        compiler_params=pltpu.CompilerParams(
            dimension_semantics=("arbitrary", "arbitrary"), vmem_limit_bytes=_vmem_limit(a["est"] + b["est"])),
        name="decode",
    )(page_table, *a["args"], *b["args"])


def _pad_w_in(w):
    sizes = (A_W,) * 3 + (B_W,) * 3 + (B_HEADS,) + (C_W,) * 4
    offs = np.cumsum((0,) + sizes)
    segs = [SEG_QA, SEG_KA, SEG_VA, SEG_QF, SEG_KF, SEG_VF, SEG_FF, SEG_QC, SEG_FC, SEG_IC, SEG_GC]
    order = np.argsort(segs)
    ends = sorted(segs)[1:] + [D_IN_PAD]
    wb = w.astype(BF16)
    parts = []
    for i, end in zip(order, ends):
        parts.append(wb[:, :, offs[i]:offs[i] + sizes[i]])
        gap = end - segs[i] - sizes[i]
        if gap:
            parts.append(jnp.zeros(w.shape[:2] + (gap,), BF16))
    return jnp.concatenate(parts, axis=2)


def _rope_tables(pos0, t, reps):
    half = HEAD_DIM // 2
    inv = ROPE_THETA ** (-jnp.arange(half, dtype=F32) * 2.0 / HEAD_DIM)
    ang = (pos0 + jnp.arange(t, dtype=F32))[:, None] * inv[None, :]
    cos, sin, zero = jnp.cos(ang), jnp.sin(ang), jnp.zeros((t, half), F32)
    heads_per_vreg = LANES // HEAD_DIM
    tabs = (jnp.concatenate([cos, cos] * heads_per_vreg, axis=1),
            jnp.concatenate([-sin, zero] * heads_per_vreg, axis=1),
            jnp.concatenate([zero, sin] * heads_per_vreg, axis=1))
    return tuple(jnp.tile(tb, (reps, 1)) for tb in tabs)


def _pad_lanes(a, width):
    return jnp.pad(a, ((0, 0), (0, width - a.shape[1])))


def _state_rows(s):
    b, h, dk, dv = s.shape
    return jnp.swapaxes(s, 2, 3).reshape(b, h * dv, dk)


def _rows_state(st, h):
    b, _, dk = st.shape
    return jnp.swapaxes(st.reshape(b, h, -1, dk), 2, 3)


def kernel(x_prompt, x_sample, cache_moba_k, cache_moba_v, cache_fox_k, cache_fox_v, cache_fox_logf, state_hgrn, page_table, norm_ffn1_g, ffn1_w_gate, ffn1_w_up, ffn1_w_down, norm_mix_g, w_in, fox_f_bias, hgrn_lb_logits, hgrn_norm_g, w_out, norm_ffn2_g, ffn2_w_gate, ffn2_w_up, ffn2_w_down, norm_final_g):
    depth = w_in.shape[0]
    batch, seq, d = x_prompt.shape
    n_seq, t_new, _ = x_sample.shape
    page = cache_moba_k.shape[2]
    past_len = page_table.shape[1] * page

    row = lambda a: a.reshape(1, -1)
    bf = lambda w: w.astype(BF16)
    ffn1 = lambda l: (row(norm_ffn1_g[l]), w1g, w1u, w1d, l)
    w1g, w1u, w1d = bf(ffn1_w_gate), bf(ffn1_w_up), bf(ffn1_w_down)
    w2g, w2u, w2d = bf(ffn2_w_gate), bf(ffn2_w_up), bf(ffn2_w_down)
    wo = bf(w_out)
    w_in_pad = _pad_w_in(w_in)
    fb = [_pad_lanes(row(fox_f_bias[l]), LANES) for l in range(depth)]
    lbl = _pad_lanes(hgrn_lb_logits.astype(F32), SEG_W)
    gn = jnp.tile(hgrn_norm_g, (1, C_HEADS))
    to_pages = lambda c: jnp.transpose(c, (0, 1, 3, 4, 2))
    ckt, cvt, fkt, fvt = (to_pages(c) for c in (cache_moba_k, cache_moba_v, cache_fox_k, cache_fox_v))
    flt = jnp.transpose(cache_fox_logf, (0, 3, 1, 2))

    def trunk(x, sample):
        if sample:
            nb, t = n_seq, t_new
            tabs = _rope_tables(float(past_len), t, nb)
        else:
            nb, t = batch, seq
            tabs = _rope_tables(0.0, t, 1)
        new, kv_t = [], None
        for l in range(depth):
            if l == 0:
                x = _ffn(x, *ffn1(l))
            args = (x, row(norm_mix_g[l]), w_in_pad, tabs, fb[l], lbl, l)
            heads_last = lambda a, h: a.reshape(nb, t, h, HEAD_DIM)
            if sample:
                qa, ka, va, qf, kf, vf, lf, qc, kc, lc, ic, gc = _inproj(*args)
                lf_t = jnp.swapaxes(lf[:, :B_HEADS].reshape(nb, t, B_HEADS), 1, 2)
                oa, of = _decode_pair(page_table, l, nb, t, (qa, ka, va, ckt, cvt),
                                      (qf, kf, vf, fkt, fvt, (lf_t, flt)))
                st0 = _state_rows(state_hgrn[l].astype(F32))
                rows_kv = (heads_last(ka, A_HEADS), heads_last(va, A_HEADS),
                           heads_last(kf, B_HEADS), heads_last(vf, B_HEADS))
            else:
                qa, ka, ka_t, va_t, qf, kf, kf_t, vf_t, lf, qc, kc, lc, ic, gc = _inproj(
                    *args, seq_t=t, carried=kv_t)
                kv_t = (ka_t, va_t, kf_t, vf_t)
                oa = _moba_prompt(qa, ka, va_t, l, nb, t)
                of = _fox_prompt(qf, kf, vf_t, lf, l, nb, t)
                st0 = jnp.zeros((nb, C_W, HEAD_DIM), F32)
                rows_kv = ()
            oc, st = _hgrn(qc, kc, lc, ic, gc, row(gn[l]), st0, nb, t)
            last = l == depth - 1
            x = _ffn(x, row(norm_ffn2_g[l]), w2g, w2u, w2d, l, mix=(oa, of, oc, wo),
                     g_final=row(norm_final_g) if last else None)
            if not last:
                x = _ffn(x, *ffn1(l + 1))
            new.append(rows_kv + (lf[:, :B_HEADS].reshape(nb, t, B_HEADS), _rows_state(st, C_HEADS)))
        outs = [jnp.stack(z) for z in zip(*new)]
        if not sample:
            heads_of = (A_HEADS, A_HEADS, B_HEADS, B_HEADS)
            outs = [jnp.transpose(a.reshape(depth, nb, h, HEAD_DIM, t), (0, 1, 4, 2, 3))
                    for a, h in zip(kv_t, heads_of)] + outs
        return x.reshape(nb, t, d), outs

    y_p, (mk_p, mv_p, fk_p, fv_p, fl_p, hs_p) = trunk(x_prompt.reshape(batch * seq, d), False)
    y_s, (mk_s, mv_s, fk_s, fv_s, fl_s, hs_s) = trunk(x_sample.reshape(n_seq * t_new, d), True)
    return (y_p, y_s, mk_p, mv_p, fk_p, fv_p, fl_p, hs_p, mk_s, mv_s, fk_s, fv_s, fl_s, hs_s)
```
